```python
import math
import jax, jax.numpy as jnp
from jax import lax
import numpy as np

D_MODEL = 1024
BATCH = 4
SEQ = 8192
DEPTH = 1

N_META = 16
MIX_WIDTH = D_MODEL
DN_HEADS = 4
DN_HEAD_DIM = 128
DN_WIDTH = DN_HEADS * DN_HEAD_DIM
SHORT_CONV = 4
CHUNK = 64
CONF_WIDTH = MIX_WIDTH - DN_WIDTH
CONF_KERNEL = 31
IN_PROJ_WIDTH = 4 * DN_WIDTH + 2 * DN_HEADS + 2 * CONF_WIDTH
N_GROUPS = 4
EXPERTS_PER_GROUP = 8
N_EXPERTS = N_GROUPS * EXPERTS_PER_GROUP
TOP_K = 2
EXPERT_FF = 256
MOE_BLOCK = 128
DEEPNORM_ALPHA = (2.0 * DEPTH) ** 0.25
DEEPNORM_BETA = (8.0 * DEPTH) ** -0.25
NORM_EPS = 1e-5

kernel_name = "hybrid_deltanet_conformer_hmoe_deepnorm"


def layer_norm(x, g, b):
    xf = x.astype(jnp.float32)
    mu = jnp.mean(xf, axis=-1, keepdims=True)
    var = jnp.mean(jnp.square(xf - mu), axis=-1, keepdims=True)
    y = (xf - mu) * lax.rsqrt(var + NORM_EPS)
    return (y * g.astype(jnp.float32) + b.astype(jnp.float32)).astype(x.dtype)


def l2norm(x):
    xf = x.astype(jnp.float32)
    return xf * lax.rsqrt(jnp.sum(jnp.square(xf), axis=-1, keepdims=True) + 1e-6)


def gated_rms_norm(o, z, g):
    of = o.astype(jnp.float32)
    r = of * lax.rsqrt(jnp.mean(jnp.square(of), axis=-1, keepdims=True) + 1e-6)
    return (r * g.astype(jnp.float32) * jax.nn.silu(z.astype(jnp.float32))).astype(z.dtype)


def causal_depthwise_conv(x, w):
    k_len, c = w.shape
    return lax.conv_general_dilated(
        x, w[:, None, :].astype(x.dtype), window_strides=(1,), padding=[(k_len - 1, 0)],
        dimension_numbers=('NWC', 'WIO', 'NWC'), feature_group_count=c)


def gated_delta_chunked(q, k, v, g, beta):
    bsz, seq_len, n_heads, dk = q.shape
    dv = v.shape[-1]
    nc = seq_len // CHUNK

    def to_chunks(t):
        t = t.astype(jnp.float32).reshape((bsz, nc, CHUNK, n_heads) + t.shape[3:])
        return jnp.moveaxis(t, 3, 1)

    q, k, v, g, beta = (to_chunks(t) for t in (q, k, v, g, beta))
    gc = jnp.cumsum(g, axis=-1)
    idx = jnp.arange(CHUNK)
    causal = idx[:, None] >= idx[None, :]
    strict = idx[:, None] > idx[None, :]
    decay = jnp.exp(jnp.where(causal, gc[..., :, None] - gc[..., None, :], -jnp.inf))
    k_beta = k * beta[..., None]
    v_beta = v * beta[..., None]
    a_low = jnp.where(strict, jnp.einsum('bhncd,bhnsd->bhncs', k_beta, k) * decay, 0.0)
    eye = jnp.eye(CHUNK, dtype=jnp.float32)
    t_mat = lax.linalg.triangular_solve(eye + a_low, jnp.broadcast_to(eye, a_low.shape),
                                        left_side=True, lower=True, unit_diagonal=True)
    u = t_mat @ v_beta
    w = t_mat @ (k_beta * jnp.exp(gc)[..., None])
    attn = jnp.einsum('bhncd,bhnsd->bhncs', q, k) * decay
    g_last = gc[..., -1]
    q_dec = q * jnp.exp(gc)[..., None]
    k_dec = k * jnp.exp(g_last[..., None] - gc)[..., None]

    def step(state, xs):
        u_c, w_c, attn_c, qd_c, kd_c, gl_c = xs
        v_new = u_c - jnp.einsum('bhcd,bhde->bhce', w_c, state)
        o = jnp.einsum('bhcd,bhde->bhce', qd_c, state) + jnp.einsum('bhcs,bhse->bhce', attn_c, v_new)
        state = state * jnp.exp(gl_c)[..., None, None] + jnp.einsum('bhcd,bhce->bhde', kd_c, v_new)
        return state, o

    xs = tuple(jnp.moveaxis(t, 2, 0) for t in (u, w, attn, q_dec, k_dec, g_last))
    s0 = jnp.zeros((bsz, n_heads, dk, dv), jnp.float32)
    _, o = lax.scan(step, s0, xs)
    return jnp.transpose(o, (1, 0, 3, 2, 4)).reshape(bsz, seq_len, n_heads, dv)


def token_mixers(h, w_in, conv_qkv_w, a_log, dt_bias, dn_norm_g, conv_dw_w, conv_dw_b,
                 cv_norm_g, cv_norm_b, w_out):
    bsz, seq_len, _ = h.shape
    proj = h @ w_in
    qkv, z, b_logit, a_logit, glu_in = jnp.split(
        proj, [3 * DN_WIDTH, 4 * DN_WIDTH, 4 * DN_WIDTH + DN_HEADS, 4 * DN_WIDTH + 2 * DN_HEADS], axis=-1)

    qkv = jax.nn.silu(causal_depthwise_conv(qkv, conv_qkv_w))
    q, k, v = jnp.split(qkv, 3, axis=-1)
    q = l2norm(q.reshape(bsz, seq_len, DN_HEADS, DN_HEAD_DIM)) * (DN_HEAD_DIM ** -0.5)
    k = l2norm(k.reshape(bsz, seq_len, DN_HEADS, DN_HEAD_DIM))
    v = v.reshape(bsz, seq_len, DN_HEADS, DN_HEAD_DIM)
    beta = jax.nn.sigmoid(b_logit.astype(jnp.float32))
    g = -jnp.exp(a_log.astype(jnp.float32)) * jax.nn.softplus(
        a_logit.astype(jnp.float32) + dt_bias.astype(jnp.float32))
    pad = CHUNK - N_META

    def pad_front(t):
        return jnp.pad(t, [(0, 0), (pad, 0)] + [(0, 0)] * (t.ndim - 2))

    o = gated_delta_chunked(*(pad_front(t) for t in (q, k, v, g, beta)))[:, pad:]
    o = gated_rms_norm(o, z.reshape(bsz, seq_len, DN_HEADS, DN_HEAD_DIM), dn_norm_g)
    o = o.reshape(bsz, seq_len, DN_WIDTH)

    val, gate = jnp.split(glu_in, 2, axis=-1)
    c = val * jax.nn.sigmoid(gate)
    c = causal_depthwise_conv(c, conv_dw_w) + conv_dw_b
    c = jax.nn.silu(layer_norm(c, cv_norm_g, cv_norm_b))

    mix = jnp.concatenate([o.astype(h.dtype), c.astype(h.dtype)], axis=-1)
    return mix @ w_out


def hierarchical_moe(h, w_group, b_group, w_router, b_router, w_gate, w_up, w_down):
    bsz, seq_len, d = h.shape
    xt = h.reshape(-1, d)
    n = xt.shape[0]
    rows = jnp.arange(n)
    group_prob = jax.nn.softmax((xt @ w_group + b_group).astype(jnp.float32), axis=-1)
    g_sel = jnp.argmax(group_prob, axis=-1).astype(jnp.int32)
    p_group = group_prob[rows, g_sel]
    logits = (xt @ w_router + b_router).astype(jnp.float32).reshape(n, N_GROUPS, EXPERTS_PER_GROUP)
    in_group = logits[rows, g_sel]
    top_logit, top_idx = lax.top_k(in_group, TOP_K)
    gates = p_group[:, None] * jax.nn.softmax(top_logit, axis=-1)
    expert_id = (g_sel[:, None] * EXPERTS_PER_GROUP + top_idx.astype(jnp.int32)).reshape(-1)

    nk = n * TOP_K
    token_id = jnp.repeat(jnp.arange(n, dtype=jnp.int32), TOP_K)
    gate_flat = gates.reshape(-1)
    order = jnp.argsort(expert_id)
    e_sorted, tok_sorted, w_sorted = expert_id[order], token_id[order], gate_flat[order]
    counts = jnp.bincount(expert_id, length=N_EXPERTS).astype(jnp.int32)
    padded = (counts + MOE_BLOCK - 1) // MOE_BLOCK * MOE_BLOCK
    pad_end = jnp.cumsum(padded)
    pad_start = pad_end - padded
    cnt_start = jnp.cumsum(counts) - counts
    dest = pad_start[e_sorted] + jnp.arange(nk, dtype=jnp.int32) - cnt_start[e_sorted]
    n_blocks = (nk + MOE_BLOCK - 1) // MOE_BLOCK + N_EXPERTS
    cap = n_blocks * MOE_BLOCK
    buf_tok = jnp.full((cap,), n, jnp.int32).at[dest].set(tok_sorted)
    buf_w = jnp.zeros((cap,), jnp.float32).at[dest].set(w_sorted)
    block_expert = jnp.minimum(
        jnp.searchsorted(pad_end, jnp.arange(n_blocks, dtype=jnp.int32) * MOE_BLOCK, side='right'),
        N_EXPERTS - 1)
    x_pad = jnp.concatenate([xt, jnp.zeros((1, d), xt.dtype)], axis=0)
    xb = x_pad[buf_tok].reshape(n_blocks, MOE_BLOCK, d)

    def expert_block(args):
        xblk, e = args
        hid = jax.nn.silu(xblk @ w_gate[e]) * (xblk @ w_up[e])
        return hid @ w_down[e]

    yb = lax.map(expert_block, (xb, block_expert))
    y = yb.reshape(cap, d) * buf_w[:, None].astype(xt.dtype)
    out = jnp.zeros((n + 1, d), xt.dtype).at[buf_tok].add(y)[:n]
    return out.reshape(bsz, seq_len, d)


def setup_inputs(seed: int = 0) -> dict:
    key = jax.random.key(seed)
    ks = jax.random.split(key, 26)
    f32 = jnp.float32

    def nrm(k, shape, scale):
        return jax.random.normal(k, shape, f32) * scale

    x = nrm(ks[0], (BATCH, SEQ, D_MODEL), 1.0)
    meta_tokens = nrm(ks[1], (N_META, D_MODEL), 1.0)
    ln_emb_g = 1.0 + nrm(ks[2], (D_MODEL,), 0.02)
    ln_emb_b = nrm(ks[3], (D_MODEL,), 0.02)
    w_in = nrm(ks[4], (DEPTH, D_MODEL, IN_PROJ_WIDTH), D_MODEL ** -0.5)
    conv_qkv_w = nrm(ks[5], (DEPTH, SHORT_CONV, 3 * DN_WIDTH), SHORT_CONV ** -0.5)
    a_log = jnp.log(jax.random.uniform(ks[6], (DEPTH, DN_HEADS), f32, 1.0, 16.0))
    dt = jnp.exp(jax.random.uniform(ks[7], (DEPTH, DN_HEADS), f32, math.log(1e-3), math.log(1e-1)))
    dt_bias = dt + jnp.log(-jnp.expm1(-dt))
    dn_norm_g = 1.0 + nrm(ks[8], (DEPTH, DN_HEAD_DIM), 0.02)
    conv_dw_w = nrm(ks[9], (DEPTH, CONF_KERNEL, CONF_WIDTH), CONF_KERNEL ** -0.5)
    conv_dw_b = nrm(ks[10], (DEPTH, CONF_WIDTH), 0.02)
    cv_norm_g = 1.0 + nrm(ks[11], (DEPTH, CONF_WIDTH), 0.02)
    cv_norm_b = nrm(ks[12], (DEPTH, CONF_WIDTH), 0.02)
    w_out = nrm(ks[13], (DEPTH, MIX_WIDTH, D_MODEL), DEEPNORM_BETA * MIX_WIDTH ** -0.5)
    ln1_g = 1.0 + nrm(ks[14], (DEPTH, D_MODEL), 0.02)
    ln1_b = nrm(ks[15], (DEPTH, D_MODEL), 0.02)
    w_group = nrm(ks[16], (DEPTH, D_MODEL, N_GROUPS), D_MODEL ** -0.5)
    b_group = nrm(ks[17], (DEPTH, N_GROUPS), 0.01)
    w_router = nrm(ks[18], (DEPTH, D_MODEL, N_EXPERTS), D_MODEL ** -0.5)
    b_router = nrm(ks[19], (DEPTH, N_EXPERTS), 0.01)
    w_exp_gate = nrm(ks[20], (DEPTH, N_EXPERTS, D_MODEL, EXPERT_FF), D_MODEL ** -0.5)
    w_exp_up = nrm(ks[21], (DEPTH, N_EXPERTS, D_MODEL, EXPERT_FF), D_MODEL ** -0.5)
    w_exp_down = nrm(ks[22], (DEPTH, N_EXPERTS, EXPERT_FF, D_MODEL), DEEPNORM_BETA * EXPERT_FF ** -0.5)
    ln2_g = 1.0 + nrm(ks[23], (DEPTH, D_MODEL), 0.02)
    ln2_b = nrm(ks[24], (DEPTH, D_MODEL), 0.02)
    return {
        'x': x, 'meta_tokens': meta_tokens, 'ln_emb_g': ln_emb_g, 'ln_emb_b': ln_emb_b,
        'w_in': w_in, 'conv_qkv_w': conv_qkv_w, 'a_log': a_log, 'dt_bias': dt_bias,
        'dn_norm_g': dn_norm_g, 'conv_dw_w': conv_dw_w, 'conv_dw_b': conv_dw_b,
        'cv_norm_g': cv_norm_g, 'cv_norm_b': cv_norm_b, 'w_out': w_out,
        'ln1_g': ln1_g, 'ln1_b': ln1_b, 'w_group': w_group, 'b_group': b_group,
        'w_router': w_router, 'b_router': b_router, 'w_exp_gate': w_exp_gate,
        'w_exp_up': w_exp_up, 'w_exp_down': w_exp_down, 'ln2_g': ln2_g, 'ln2_b': ln2_b,
    }


def reference(x, meta_tokens, ln_emb_g, ln_emb_b, w_in, conv_qkv_w, a_log, dt_bias, dn_norm_g,
              conv_dw_w, conv_dw_b, cv_norm_g, cv_norm_b, w_out, ln1_g, ln1_b, w_group, b_group,
              w_router, b_router, w_exp_gate, w_exp_up, w_exp_down, ln2_g, ln2_b):
    bsz = x.shape[0]
    meta = jnp.broadcast_to(meta_tokens[None].astype(x.dtype), (bsz, N_META, x.shape[-1]))
    h = jnp.concatenate([meta, x], axis=1)
    h = layer_norm(h, ln_emb_g, ln_emb_b)
    for layer in range(DEPTH):
        mix = token_mixers(h, w_in[layer], conv_qkv_w[layer], a_log[layer], dt_bias[layer],
                           dn_norm_g[layer], conv_dw_w[layer], conv_dw_b[layer],
                           cv_norm_g[layer], cv_norm_b[layer], w_out[layer])
        h = layer_norm(DEEPNORM_ALPHA * h + mix, ln1_g[layer], ln1_b[layer])
        ffn = hierarchical_moe(h, w_group[layer], b_group[layer], w_router[layer], b_router[layer],
                               w_exp_gate[layer], w_exp_up[layer], w_exp_down[layer])
        h = layer_norm(DEEPNORM_ALPHA * h + ffn, ln2_g[layer], ln2_b[layer])
    return h[:, N_META:]
```

```python
import functools

import jax
import jax.numpy as jnp
from jax import lax
from jax.experimental import pallas as pl
from jax.experimental.pallas import tpu as pltpu

F32 = jnp.float32
BF16 = jnp.bfloat16

NORM_EPS = 1e-5
N_META = 16
DN_HEADS = 4
HEAD_DIM = 128
DN_WIDTH = DN_HEADS * HEAD_DIM
CHUNK = 64
SHORT_CONV = 4
CONF_KERNEL = 31
N_GROUPS = 4
EXPERTS_PER_GROUP = 8
N_EXPERTS = N_GROUPS * EXPERTS_PER_GROUP
TOP_K = 2
LANES = 128
QKV_HALO = 8
CONF_HALO = 32
VMEM_LIMIT = 56 * 1024 * 1024

TM_IN = 256
DELTA_CHUNKS = 8
TM_OUT = 256
BM_EXPERT = 256
TM_COMBINE = 256


def _dot(a, b):
    return jnp.dot(a, b, preferred_element_type=F32)


def _split2(x):
    hi = x.astype(BF16)
    lo = (x - hi.astype(F32)).astype(BF16)
    return hi, lo


def _dot_hilo(a, b):
    ah, al = _split2(a)
    bh, bl = _split2(b)
    return _dot(ah, bh) + _dot(al, bh) + _dot(ah, bl)


def _dot_exact01(m01, x):
    x1 = x.astype(BF16)
    r1 = x - x1.astype(F32)
    x2 = r1.astype(BF16)
    x3 = (r1 - x2.astype(F32)).astype(BF16)
    return _dot(m01, x1) + _dot(m01, x2) + _dot(m01, x3)


def _sigmoid(x):
    return 1.0 / (1.0 + jnp.exp(-x))


def _silu(x):
    return x * _sigmoid(x)


def _layer_norm(x, g, b):
    mu = jnp.mean(x, axis=-1, keepdims=True)
    xc = x - mu
    var = jnp.mean(xc * xc, axis=-1, keepdims=True)
    return xc * lax.rsqrt(var + NORM_EPS) * g + b


def _full_spec(shape):
    nd = len(shape)
    return pl.BlockSpec(shape, lambda *_: (0,) * nd)


def _mix_in_kernel(x_ref, lng_ref, lnb_ref, wqkv_ref, wz_ref, wglu_ref, wba_ref, cw_ref, nega_ref,
                   dtb_ref, dww_ref, dwb_ref, cvg_ref, cvb_ref, hq_in_ref, hc_in_ref,
                   q_ref, k_ref, v_ref, z_ref, c_ref, bg_ref, hq_out_ref, hc_out_ref,
                   qkv_ext, c_ext):
    tm = x_ref.shape[0]

    @pl.when(pl.program_id(1) == 0)
    def _():
        qkv_ext[0:QKV_HALO, :] = hq_in_ref[...]
        c_ext[0:CONF_HALO, :] = hc_in_ref[...]

    h = _layer_norm(x_ref[...], lng_ref[...], lnb_ref[...])
    hb = h.astype(BF16)

    qkv_ext[QKV_HALO:QKV_HALO + tm, :] = _dot(hb, wqkv_ref[...])
    acc = None
    for tap in range(SHORT_CONV):
        term = qkv_ext[pl.ds(QKV_HALO - (SHORT_CONV - 1) + tap, tm), :] * cw_ref[tap:tap + 1, :]
        acc = term if acc is None else acc + term
    qkv = _silu(acc)
    for hd in range(DN_HEADS):
        lo = hd * HEAD_DIM
        qh = qkv[:, lo:lo + HEAD_DIM]
        kh = qkv[:, DN_WIDTH + lo:DN_WIDTH + lo + HEAD_DIM]
        q_ref[:, lo:lo + HEAD_DIM] = qh * (lax.rsqrt(jnp.sum(qh * qh, axis=-1, keepdims=True) + 1e-6)
                                           * (HEAD_DIM ** -0.5))
        k_ref[:, lo:lo + HEAD_DIM] = kh * lax.rsqrt(jnp.sum(kh * kh, axis=-1, keepdims=True) + 1e-6)
    v_ref[...] = qkv[:, 2 * DN_WIDTH:]
    z_ref[...] = _dot(hb, wz_ref[...])

    ba = _dot(hb, wba_ref[...])
    lane = lax.broadcasted_iota(jnp.int32, ba.shape, 1)
    sp_in = ba + dtb_ref[...]
    softplus = jnp.maximum(sp_in, 0.0) + jnp.log(1.0 + jnp.exp(-jnp.abs(sp_in)))
    bg_ref[...] = jnp.where(lane < DN_HEADS, _sigmoid(ba), nega_ref[...] * softplus)

    glu = _dot(hb, wglu_ref[...])
    cw = glu.shape[1] // 2
    c_ext[CONF_HALO:CONF_HALO + tm, :] = glu[:, :cw] * _sigmoid(glu[:, cw:])
    acc = None
    for tap in range(CONF_KERNEL):
        term = c_ext[pl.ds(CONF_HALO - (CONF_KERNEL - 1) + tap, tm), :] * dww_ref[tap:tap + 1, :]
        acc = term if acc is None else acc + term
    conv = acc + dwb_ref[...]
    c_ref[...] = _silu(_layer_norm(conv, cvg_ref[...], cvb_ref[...]))

    q_tail = qkv_ext[tm:tm + QKV_HALO, :]
    c_tail = c_ext[tm:tm + CONF_HALO, :]
    qkv_ext[0:QKV_HALO, :] = q_tail
    c_ext[0:CONF_HALO, :] = c_tail
    hq_out_ref[...] = q_tail
    hc_out_ref[...] = c_tail


def _mix_in(x, p, halo_q, halo_c, tm):
    bsz, seq, d = x.shape
    assert seq % tm == 0
    qkv_w = 3 * DN_WIDTH
    conf_w = p['dw_w'].shape[1]

    def row(width):
        return pl.BlockSpec((None, tm, width), lambda b, t: (b, t, 0))

    def per_batch(rows, width):
        return pl.BlockSpec((None, rows, width), lambda b, t: (b, 0, 0))

    consts = [p['ln_emb_g'], p['ln_emb_b'], p['w_qkv'], p['w_z'], p['w_glu'], p['w_ba'], p['conv_w'],
              p['neg_a'], p['dt_b'], p['dw_w'], p['dw_b'], p['cv_g'], p['cv_b'], halo_q, halo_c]
    sds = jax.ShapeDtypeStruct
    out_shape = ([sds((bsz, seq, DN_WIDTH), F32)] * 4 + [sds((bsz, seq, conf_w), F32),
                 sds((bsz, seq, LANES), F32), sds((bsz, QKV_HALO, qkv_w), F32),
                 sds((bsz, CONF_HALO, conf_w), F32)])
    out_specs = ([row(DN_WIDTH)] * 4 + [row(conf_w), row(LANES), per_batch(QKV_HALO, qkv_w),
                 per_batch(CONF_HALO, conf_w)])
    return pl.pallas_call(
        _mix_in_kernel,
        grid=(bsz, seq // tm),
        in_specs=[row(d)] + [_full_spec(c.shape) for c in consts],
        out_specs=out_specs,
        out_shape=out_shape,
        scratch_shapes=[pltpu.VMEM((QKV_HALO + tm, qkv_w), F32), pltpu.VMEM((CONF_HALO + tm, conf_w), F32)],
        compiler_params=pltpu.CompilerParams(dimension_semantics=("parallel", "arbitrary"),
                                             vmem_limit_bytes=VMEM_LIMIT),
        name="mix_in",
    )(x, *consts)


def _chunk_head(q, k, v, z, bet, gc, gcr, state, gain, causal, strict, eye):
    decay = jnp.exp(jnp.where(causal, gc - gcr, -jnp.inf))
    kb = k * bet
    kbf = k.astype(BF16)
    g_all = lax.dot_general(jnp.concatenate([kb, q], axis=0).astype(BF16), kbf,
                            (((1,), (1,)), ((), ())), preferred_element_type=F32)
    a_low = jnp.where(strict, g_all[:CHUNK] * decay, 0.0)
    attn = g_all[CHUNK:] * decay

    l_mat = eye + a_low
    l_bf = l_mat.astype(BF16)
    t_mat = eye - a_low
    for _ in range(4):
        res = eye - _dot(l_bf, t_mat.astype(BF16))
        t_mat = t_mat + _dot(t_mat.astype(BF16), res.astype(BF16))
    res = eye - _dot_hilo(l_mat, t_mat)
    t_mat = t_mat + _dot(t_mat.astype(BF16), res.astype(BF16))

    eg = jnp.exp(gc)
    uw = _dot(t_mat.astype(BF16), jnp.concatenate([v * bet, kb * eg], axis=1).astype(BF16))
    u = uw[:, :HEAD_DIM]
    w = uw[:, HEAD_DIM:]
    wq = _dot(jnp.concatenate([w, q * eg], axis=0).astype(BF16), state.astype(BF16))
    v_new = u - wq[:CHUNK]
    v_new_bf = v_new.astype(BF16)
    o = wq[CHUNK:] + _dot(attn.astype(BF16), v_new_bf)
    g_last = gc[CHUNK - 1:CHUNK, :]
    k_dec = (k * jnp.exp(g_last - gc)).astype(BF16)
    new_state = state * jnp.exp(g_last) + lax.dot_general(
        k_dec, v_new_bf, (((0,), (0,)), ((), ())), preferred_element_type=F32)
    r = o * lax.rsqrt(jnp.mean(o * o, axis=-1, keepdims=True) + 1e-6)
    return r * gain * _silu(z), new_state


def _delta_kernel(q_ref, k_ref, v_ref, z_ref, bg_ref, s0_ref, gain_ref, o_ref, sfin_ref, s_ref, *, chunks):
    j = pl.program_id(1)

    @pl.when(j == 0)
    def _():
        s_ref[...] = s0_ref[...]

    ii = lax.broadcasted_iota(jnp.int32, (CHUNK, CHUNK), 0)
    jj = lax.broadcasted_iota(jnp.int32, (CHUNK, CHUNK), 1)
    causal = ii >= jj
    strict = ii > jj
    eye = (ii == jj).astype(F32)
    tril = causal.astype(BF16)
    gain = gain_ref[...]

    def body(c, carry):
        r0 = pl.multiple_of(c * CHUNK, CHUNK)
        rows = pl.ds(r0, CHUNK)
        bgc = bg_ref[rows, :]
        gc_all = _dot_exact01(tril, bgc)
        gc_t = gc_all.T
        for hd in range(DN_HEADS):
            lo = hd * HEAD_DIM
            cols = slice(lo, lo + HEAD_DIM)
            out, new_state = _chunk_head(
                q_ref[rows, cols], k_ref[rows, cols], v_ref[rows, cols], z_ref[rows, cols],
                bgc[:, hd:hd + 1], gc_all[:, DN_HEADS + hd:DN_HEADS + hd + 1],
                gc_t[DN_HEADS + hd:DN_HEADS + hd + 1, :], s_ref[hd], gain, causal, strict, eye)
            o_ref[rows, cols] = out
            s_ref[hd] = new_state
        return carry

    lax.fori_loop(0, chunks, body, 0)

    @pl.when(j == pl.num_programs(1) - 1)
    def _():
        sfin_ref[...] = s_ref[...]


def _delta(q, k, v, z, bg, s0, gain, chunks):
    bsz, seq, _ = q.shape
    rows = chunks * CHUNK
    assert seq % rows == 0

    def row(width):
        return pl.BlockSpec((None, rows, width), lambda b, j: (b, j, 0))

    state_shape = (DN_HEADS, HEAD_DIM, HEAD_DIM)
    return pl.pallas_call(
        functools.partial(_delta_kernel, chunks=chunks),
        grid=(bsz, seq // rows),
        in_specs=[row(DN_WIDTH)] * 4 + [row(LANES), _full_spec(state_shape), _full_spec(gain.shape)],
        out_specs=[row(DN_WIDTH), pl.BlockSpec((None,) + state_shape, lambda b, j: (b, 0, 0, 0))],
        out_shape=[jax.ShapeDtypeStruct((bsz, seq, DN_WIDTH), F32),
                   jax.ShapeDtypeStruct((bsz,) + state_shape, F32)],
        scratch_shapes=[pltpu.VMEM(state_shape, F32)],
        compiler_params=pltpu.CompilerParams(dimension_semantics=("parallel", "arbitrary"),
                                             vmem_limit_bytes=VMEM_LIMIT),
        name="delta",
    )(q, k, v, z, bg, s0, gain)


def _mix_out_kernel(x_ref, o_ref, c_ref, lng_ref, lnb_ref, wo_ref, g1_ref, b1_ref, wrh_ref, wrl_ref, br_ref,
                    h1_ref, route_ref, *, alpha):
    h = _layer_norm(x_ref[...], lng_ref[...], lnb_ref[...])
    dn = o_ref.shape[1]
    mix = _dot(o_ref[...].astype(BF16), wo_ref[0:dn, :]) + _dot(c_ref[...].astype(BF16), wo_ref[dn:, :])
    h1 = _layer_norm(alpha * h + mix, g1_ref[...], b1_ref[...])
    h1_ref[...] = h1

    hh, hl = _split2(h1)
    logits = _dot(hh, wrh_ref[...]) + _dot(hl, wrh_ref[...]) + _dot(hh, wrl_ref[...]) + br_ref[...]
    lane = lax.broadcasted_iota(jnp.int32, logits.shape, 1).astype(F32)
    big = float(LANES)
    neg = -jnp.inf

    def first_argmax(vals):
        top = jnp.max(vals, axis=-1, keepdims=True)
        return top, jnp.min(jnp.where(vals == top, lane, big), axis=-1, keepdims=True)

    grp = jnp.where(lane < N_GROUPS, logits, neg)
    g_top, g_sel = first_argmax(grp)
    p_group = 1.0 / jnp.sum(jnp.exp(grp - g_top), axis=-1, keepdims=True)
    lo = N_GROUPS + EXPERTS_PER_GROUP * g_sel
    in_grp = jnp.where((lane >= lo) & (lane < lo + EXPERTS_PER_GROUP), logits, neg)
    m1, i1 = first_argmax(in_grp)
    m2, i2 = first_argmax(jnp.where(lane == i1, neg, in_grp))
    s = jnp.exp(m2 - m1)
    w1 = p_group / (1.0 + s)
    w2 = p_group * s / (1.0 + s)
    route = jnp.where(lane == 0, i1 - N_GROUPS,
                      jnp.where(lane == 1, i2 - N_GROUPS,
                                jnp.where(lane == 2, w1, jnp.where(lane == 3, w2, 0.0))))
    route_ref[...] = route


def _mix_out(x2d, o2d, c2d, p, tm, alpha):
    n, d = x2d.shape
    assert n % tm == 0

    def row(width):
        return pl.BlockSpec((tm, width), lambda i: (i, 0))

    consts = [p['ln_emb_g'], p['ln_emb_b'], p['w_out'], p['ln1_g'], p['ln1_b'], p['w_r_hi'], p['w_r_lo'], p['b_r']]
    return pl.pallas_call(
        functools.partial(_mix_out_kernel, alpha=alpha),
        grid=(n // tm,),
        in_specs=[row(d), row(o2d.shape[1]), row(c2d.shape[1])] + [_full_spec(c.shape) for c in consts],
        out_specs=[row(d), row(LANES)],
        out_shape=[jax.ShapeDtypeStruct((n, d), F32), jax.ShapeDtypeStruct((n, LANES), F32)],
        compiler_params=pltpu.CompilerParams(dimension_semantics=("parallel",), vmem_limit_bytes=VMEM_LIMIT),
        name="mix_out",
    )(x2d, o2d, c2d, *consts)


def _expert_kernel(be_ref, tok_ref, h1_hbm, wg_ref, wu_ref, wd_ref, y_ref, xbuf, sem, *, bm):
    del be_ref
    i = pl.program_id(0)
    nb = pl.num_programs(0)

    def row_copy(blk, slot, r):
        tok = tok_ref[blk * bm + r]
        return pltpu.make_async_copy(h1_hbm.at[pl.ds(tok, 1), :], xbuf.at[slot, pl.ds(r, 1), :], sem.at[slot])

    def issue(blk, slot):
        def body(r, carry):
            row_copy(blk, slot, r).start()
            return carry
        lax.fori_loop(0, bm, body, 0)

    @pl.when(i == 0)
    def _():
        issue(0, 0)

    @pl.when(i + 1 < nb)
    def _():
        issue(i + 1, (i + 1) % 2)

    slot = i % 2

    def wait_body(r, carry):
        row_copy(i, slot, r).wait()
        return carry
    lax.fori_loop(0, bm, wait_body, 0)

    xb = xbuf[slot].astype(BF16)
    hid = _silu(_dot(xb, wg_ref[...])) * _dot(xb, wu_ref[...])
    y_ref[...] = _dot(hid.astype(BF16), wd_ref[...])


def _experts(block_expert, buf_tok, h1, w_gate, w_up, w_down, bm):
    n_blocks = block_expert.shape[0]
    d = h1.shape[1]
    ff = w_gate.shape[2]
    grid_spec = pltpu.PrefetchScalarGridSpec(
        num_scalar_prefetch=2,
        grid=(n_blocks,),
        in_specs=[pl.BlockSpec(memory_space=pl.ANY),
                  pl.BlockSpec((None, d, ff), lambda i, be, tok: (be[i], 0, 0)),
                  pl.BlockSpec((None, d, ff), lambda i, be, tok: (be[i], 0, 0)),
                  pl.BlockSpec((None, ff, d), lambda i, be, tok: (be[i], 0, 0))],
        out_specs=pl.BlockSpec((bm, d), lambda i, be, tok: (i, 0)),
        scratch_shapes=[pltpu.VMEM((2, bm, d), F32), pltpu.SemaphoreType.DMA((2,))],
    )
    return pl.pallas_call(
        functools.partial(_expert_kernel, bm=bm),
        grid_spec=grid_spec,
        out_shape=jax.ShapeDtypeStruct((n_blocks * bm, d), F32),
        compiler_params=pltpu.CompilerParams(dimension_semantics=("arbitrary",), vmem_limit_bytes=VMEM_LIMIT),
        name="experts",
    )(block_expert, buf_tok, h1, w_gate, w_up, w_down)


def _combine_kernel(dest_ref, y_hbm, h1_ref, route_ref, g2_ref, b2_ref, out_ref, ybuf, sem, *, tm, alpha):
    i = pl.program_id(0)
    nb = pl.num_programs(0)

    def row_copy(blk, slot, r, k):
        dst_row = dest_ref[(blk * tm + r) * TOP_K + k]
        return pltpu.make_async_copy(y_hbm.at[pl.ds(dst_row, 1), :], ybuf.at[slot, k, pl.ds(r, 1), :],
                                     sem.at[slot])

    def issue(blk, slot):
        def body(r, carry):
            for k in range(TOP_K):
                row_copy(blk, slot, r, k).start()
            return carry
        lax.fori_loop(0, tm, body, 0)

    @pl.when(i == 0)
    def _():
        issue(0, 0)

    @pl.when(i + 1 < nb)
    def _():
        issue(i + 1, (i + 1) % 2)

    slot = i % 2

    def wait_body(r, carry):
        for k in range(TOP_K):
            row_copy(i, slot, r, k).wait()
        return carry
    lax.fori_loop(0, tm, wait_body, 0)

    route = route_ref[...]
    ffn = ybuf[slot, 0] * route[:, 2:3] + ybuf[slot, 1] * route[:, 3:4]
    out_ref[...] = _layer_norm(alpha * h1_ref[...] + ffn, g2_ref[...], b2_ref[...])


def _combine(dest, y_sorted, h1, route, ln2_g, ln2_b, tm, alpha):
    n, d = h1.shape
    assert n % tm == 0
    grid_spec = pltpu.PrefetchScalarGridSpec(
        num_scalar_prefetch=1,
        grid=(n // tm,),
        in_specs=[pl.BlockSpec(memory_space=pl.ANY),
                  pl.BlockSpec((tm, d), lambda i, dest: (i, 0)),
                  pl.BlockSpec((tm, LANES), lambda i, dest: (i, 0)),
                  pl.BlockSpec((1, d), lambda i, dest: (0, 0)),
                  pl.BlockSpec((1, d), lambda i, dest: (0, 0))],
        out_specs=pl.BlockSpec((tm, d), lambda i, dest: (i, 0)),
        scratch_shapes=[pltpu.VMEM((2, TOP_K, tm, d), F32), pltpu.SemaphoreType.DMA((2,))],
    )
    return pl.pallas_call(
        functools.partial(_combine_kernel, tm=tm, alpha=alpha),
        grid_spec=grid_spec,
        out_shape=jax.ShapeDtypeStruct((n, d), F32),
        compiler_params=pltpu.CompilerParams(dimension_semantics=("arbitrary",), vmem_limit_bytes=VMEM_LIMIT),
        name="combine",
    )(dest, y_sorted, h1, route, ln2_g, ln2_b)


def _dispatch_plan(expert_id, bm):
    n = expert_id.shape[0]
    nk = n * TOP_K
    flat = expert_id.reshape(-1)
    onehot = (flat[:, None] == jnp.arange(N_EXPERTS, dtype=jnp.int32)[None, :]).astype(jnp.int32)
    csum = jnp.cumsum(onehot, axis=0)
    rank = jnp.take_along_axis(csum, flat[:, None], axis=1)[:, 0] - 1
    counts = csum[-1]
    padded = (counts + bm - 1) // bm * bm
    pad_end = jnp.cumsum(padded)
    pad_start = pad_end - padded
    dest = pad_start[flat] + rank
    n_blocks = (nk + bm - 1) // bm + N_EXPERTS
    token_id = jnp.arange(nk, dtype=jnp.int32) // TOP_K
    buf_tok = jnp.zeros((n_blocks * bm,), jnp.int32).at[dest].set(token_id)
    block_expert = jnp.minimum(
        jnp.searchsorted(pad_end, jnp.arange(n_blocks, dtype=jnp.int32) * bm, side='right'),
        N_EXPERTS - 1).astype(jnp.int32)
    return dest.astype(jnp.int32), buf_tok, block_expert


def _pad_lanes(w, width=LANES):
    return jnp.pad(w, [(0, 0)] * (w.ndim - 1) + [(0, width - w.shape[-1])])


def kernel(x, meta_tokens, ln_emb_g, ln_emb_b, w_in, conv_qkv_w, a_log, dt_bias, dn_norm_g, conv_dw_w, conv_dw_b, cv_norm_g, cv_norm_b, w_out, ln1_g, ln1_b, w_group, b_group, w_router, b_router, w_exp_gate, w_exp_up, w_exp_down, ln2_g, ln2_b):
    depth = w_in.shape[0]
    assert depth == 1, "single-layer block"
    bsz, seq, d = x.shape
    alpha = (2.0 * depth) ** 0.25
    qkv_w = 3 * DN_WIDTH
    w_in0 = w_in[0]
    glu_off = 4 * DN_WIDTH + 2 * DN_HEADS
    row = lambda a: a.reshape(1, -1).astype(F32)
    p = {
        'ln_emb_g': row(ln_emb_g), 'ln_emb_b': row(ln_emb_b),
        'w_qkv': w_in0[:, :qkv_w].astype(BF16),
        'w_z': w_in0[:, qkv_w:4 * DN_WIDTH].astype(BF16),
        'w_ba': _pad_lanes(w_in0[:, 4 * DN_WIDTH:glu_off]).astype(BF16),
        'w_glu': w_in0[:, glu_off:].astype(BF16),
        'conv_w': conv_qkv_w[0].astype(F32),
        'neg_a': _pad_lanes(jnp.concatenate([jnp.zeros((DN_HEADS,), F32), -jnp.exp(a_log[0].astype(F32))])[None]),
        'dt_b': _pad_lanes(jnp.concatenate([jnp.zeros((DN_HEADS,), F32), dt_bias[0].astype(F32)])[None]),
        'dw_w': conv_dw_w[0].astype(F32), 'dw_b': row(conv_dw_b[0]),
        'cv_g': row(cv_norm_g[0]), 'cv_b': row(cv_norm_b[0]),
        'w_out': w_out[0].astype(BF16), 'ln1_g': row(ln1_g[0]), 'ln1_b': row(ln1_b[0]),
    }
    w_r = _pad_lanes(jnp.concatenate([w_group[0], w_router[0]], axis=1).astype(F32))
    p['w_r_hi'] = w_r.astype(BF16)
    p['w_r_lo'] = (w_r - p['w_r_hi'].astype(F32)).astype(BF16)
    p['b_r'] = _pad_lanes(jnp.concatenate([b_group[0], b_router[0]])[None].astype(F32))
    gain = row(dn_norm_g[0])

    conf_w = p['dw_w'].shape[1]
    zero_hq = jnp.zeros((QKV_HALO, qkv_w), F32)
    zero_hc = jnp.zeros((CONF_HALO, conf_w), F32)
    mq, mk, mv, mz, _, mbg, halo_q, halo_c = _mix_in(meta_tokens[None].astype(F32), p, zero_hq, zero_hc, N_META)
    front = lambda a: jnp.pad(a, [(0, 0), (CHUNK - N_META, 0), (0, 0)])
    s_zero = jnp.zeros((DN_HEADS, HEAD_DIM, HEAD_DIM), F32)
    _, s_meta = _delta(front(mq), front(mk), front(mv), front(mz), front(mbg), s_zero, gain, 1)

    q, k, v, z, c, bg, _, _ = _mix_in(x, p, halo_q[0], halo_c[0], TM_IN)
    o, _ = _delta(q, k, v, z, bg, s_meta[0], gain, DELTA_CHUNKS)

    n = bsz * seq
    h1, route = _mix_out(x.reshape(n, d), o.reshape(n, DN_WIDTH), c.reshape(n, conf_w), p, TM_OUT, alpha)

    expert_id = route[:, :TOP_K].astype(jnp.int32)
    dest, buf_tok, block_expert = _dispatch_plan(expert_id, BM_EXPERT)
    y_sorted = _experts(block_expert, buf_tok, h1, w_exp_gate[0].astype(BF16), w_exp_up[0].astype(BF16),
                        w_exp_down[0].astype(BF16), BM_EXPERT)
    out = _combine(dest, y_sorted, h1, route, row(ln2_g[0]), row(ln2_b[0]), TM_COMBINE, alpha)
    return out.reshape(bsz, seq, d)
```

```python
import functools

import jax
import jax.numpy as jnp
from jax import lax
from jax.experimental import pallas as pl
from jax.experimental.pallas import tpu as pltpu

F32 = jnp.float32
BF16 = jnp.bfloat16

NORM_EPS = 1e-5
N_META = 16
DN_HEADS = 4
HEAD_DIM = 128
DN_WIDTH = DN_HEADS * HEAD_DIM
CHUNK = 64
SHORT_CONV = 4
CONF_KERNEL = 31
N_GROUPS = 4
EXPERTS_PER_GROUP = 8
N_EXPERTS = N_GROUPS * EXPERTS_PER_GROUP
TOP_K = 2
LANES = 128
QKV_HALO = 8
CONF_HALO = 32
VMEM_LIMIT = 56 * 1024 * 1024

TM_IN = 256
DELTA_CHUNKS = 8
TM_OUT = 256
BM_EXPERT = 256
TM_COMBINE = 256
DMA_UNROLL = 8


def _dot(a, b):
    return jnp.dot(a, b, preferred_element_type=F32)


def _split2(x):
    hi = x.astype(BF16)
    lo = (x - hi.astype(F32)).astype(BF16)
    return hi, lo


def _dot_hilo(a, b):
    ah, al = _split2(a)
    bh, bl = _split2(b)
    return _dot(ah, bh) + _dot(al, bh) + _dot(ah, bl)


def _dot_exact01(m01, x):
    x1 = x.astype(BF16)
    r1 = x - x1.astype(F32)
    x2 = r1.astype(BF16)
    x3 = (r1 - x2.astype(F32)).astype(BF16)
    return _dot(m01, x1) + _dot(m01, x2) + _dot(m01, x3)


def _sigmoid(x):
    return 1.0 / (1.0 + jnp.exp(-x))


def _silu(x):
    return x * _sigmoid(x)


def _layer_norm(x, g, b):
    mu = jnp.mean(x, axis=-1, keepdims=True)
    xc = x - mu
    var = jnp.mean(xc * xc, axis=-1, keepdims=True)
    return xc * lax.rsqrt(var + NORM_EPS) * g + b


def _full_spec(shape):
    nd = len(shape)
    return pl.BlockSpec(shape, lambda *_: (0,) * nd)


def _mix_in_kernel(x_ref, lng_ref, lnb_ref, wqkv_ref, wz_ref, wglu_ref, wba_ref, cw_ref, nega_ref,
                   dtb_ref, dww_ref, dwb_ref, cvg_ref, cvb_ref, hq_in_ref, hc_in_ref,
                   q_ref, k_ref, v_ref, z_ref, c_ref, bg_ref, hq_out_ref, hc_out_ref,
                   qkv_ext, c_ext):
    tm = x_ref.shape[0]

    @pl.when(pl.program_id(1) == 0)
    def _():
        qkv_ext[0:QKV_HALO, :] = hq_in_ref[...]
        c_ext[0:CONF_HALO, :] = hc_in_ref[...]

    h = _layer_norm(x_ref[...], lng_ref[...], lnb_ref[...])
    hb = h.astype(BF16)

    qkv_ext[QKV_HALO:QKV_HALO + tm, :] = _dot(hb, wqkv_ref[...])
    acc = None
    for tap in range(SHORT_CONV):
        term = qkv_ext[pl.ds(QKV_HALO - (SHORT_CONV - 1) + tap, tm), :] * cw_ref[tap:tap + 1, :]
        acc = term if acc is None else acc + term
    qkv = _silu(acc)
    for hd in range(DN_HEADS):
        lo = hd * HEAD_DIM
        qh = qkv[:, lo:lo + HEAD_DIM]
        kh = qkv[:, DN_WIDTH + lo:DN_WIDTH + lo + HEAD_DIM]
        q_ref[:, lo:lo + HEAD_DIM] = qh * (lax.rsqrt(jnp.sum(qh * qh, axis=-1, keepdims=True) + 1e-6)
                                           * (HEAD_DIM ** -0.5))
        k_ref[:, lo:lo + HEAD_DIM] = kh * lax.rsqrt(jnp.sum(kh * kh, axis=-1, keepdims=True) + 1e-6)
    v_ref[...] = qkv[:, 2 * DN_WIDTH:]
    z_ref[...] = _dot(hb, wz_ref[...])

    ba = _dot(hb, wba_ref[...])
    lane = lax.broadcasted_iota(jnp.int32, ba.shape, 1)
    sp_in = ba + dtb_ref[...]
    softplus = jnp.maximum(sp_in, 0.0) + jnp.log(1.0 + jnp.exp(-jnp.abs(sp_in)))
    bg_ref[...] = jnp.where(lane < DN_HEADS, _sigmoid(ba), nega_ref[...] * softplus)

    glu = _dot(hb, wglu_ref[...])
    cw = glu.shape[1] // 2
    c_ext[CONF_HALO:CONF_HALO + tm, :] = glu[:, :cw] * _sigmoid(glu[:, cw:])
    acc = None
    for tap in range(CONF_KERNEL):
        term = c_ext[pl.ds(CONF_HALO - (CONF_KERNEL - 1) + tap, tm), :] * dww_ref[tap:tap + 1, :]
        acc = term if acc is None else acc + term
    conv = acc + dwb_ref[...]
    c_ref[...] = _silu(_layer_norm(conv, cvg_ref[...], cvb_ref[...]))

    q_tail = qkv_ext[tm:tm + QKV_HALO, :]
    c_tail = c_ext[tm:tm + CONF_HALO, :]
    qkv_ext[0:QKV_HALO, :] = q_tail
    c_ext[0:CONF_HALO, :] = c_tail
    hq_out_ref[...] = q_tail
    hc_out_ref[...] = c_tail


def _mix_in(x, p, halo_q, halo_c, tm):
    bsz, seq, d = x.shape
    assert seq % tm == 0
    qkv_w = 3 * DN_WIDTH
    conf_w = p['dw_w'].shape[1]

    def row(width):
        return pl.BlockSpec((None, tm, width), lambda b, t: (b, t, 0))

    def per_batch(rows, width):
        return pl.BlockSpec((None, rows, width), lambda b, t: (b, 0, 0))

    consts = [p['ln_emb_g'], p['ln_emb_b'], p['w_qkv'], p['w_z'], p['w_glu'], p['w_ba'], p['conv_w'],
              p['neg_a'], p['dt_b'], p['dw_w'], p['dw_b'], p['cv_g'], p['cv_b'], halo_q, halo_c]
    sds = jax.ShapeDtypeStruct
    out_shape = ([sds((bsz, seq, DN_WIDTH), F32)] * 4 + [sds((bsz, seq, conf_w), F32),
                 sds((bsz, seq, LANES), F32), sds((bsz, QKV_HALO, qkv_w), F32),
                 sds((bsz, CONF_HALO, conf_w), F32)])
    out_specs = ([row(DN_WIDTH)] * 4 + [row(conf_w), row(LANES), per_batch(QKV_HALO, qkv_w),
                 per_batch(CONF_HALO, conf_w)])
    return pl.pallas_call(
        _mix_in_kernel,
        grid=(bsz, seq // tm),
        in_specs=[row(d)] + [_full_spec(c.shape) for c in consts],
        out_specs=out_specs,
        out_shape=out_shape,
        scratch_shapes=[pltpu.VMEM((QKV_HALO + tm, qkv_w), F32), pltpu.VMEM((CONF_HALO + tm, conf_w), F32)],
        compiler_params=pltpu.CompilerParams(dimension_semantics=("parallel", "arbitrary"),
                                             vmem_limit_bytes=VMEM_LIMIT),
        name="mix_in",
    )(x, *consts)


def _bmm(a, b):
    return jnp.einsum('nij,njk->nik', a, b, preferred_element_type=F32)


def _delta_kernel(q_ref, k_ref, v_ref, z_ref, bg_ref, s0_ref, gain_ref, o_ref, sfin_ref, s_ref, *, chunks):
    j = pl.program_id(1)

    @pl.when(j == 0)
    def _():
        s_ref[...] = s0_ref[...]

    ii = lax.broadcasted_iota(jnp.int32, (CHUNK, CHUNK), 0)
    jj = lax.broadcasted_iota(jnp.int32, (CHUNK, CHUNK), 1)
    causal = ii >= jj
    strict = ii > jj
    eye = (ii == jj).astype(F32)
    gain = gain_ref[...]

    bg3 = bg_ref[...].reshape(chunks, CHUNK, LANES)
    tril_b = jnp.broadcast_to(causal.astype(BF16), (chunks, CHUNK, CHUNK))
    p1 = bg3.astype(BF16)
    r1 = bg3 - p1.astype(F32)
    p2 = r1.astype(BF16)
    p3 = (r1 - p2.astype(F32)).astype(BF16)
    gc3 = _bmm(tril_b, p1) + _bmm(tril_b, p2) + _bmm(tril_b, p3)

    def heads(ref):
        return jnp.concatenate([ref[:, hd * HEAD_DIM:(hd + 1) * HEAD_DIM].reshape(chunks, CHUNK, HEAD_DIM)
                                for hd in range(DN_HEADS)], axis=0)
    q = heads(q_ref)
    k = heads(k_ref)
    v = heads(v_ref)
    bet = jnp.concatenate([bg3[:, :, hd:hd + 1] for hd in range(DN_HEADS)], axis=0)
    gc = jnp.concatenate([gc3[:, :, DN_HEADS + hd:DN_HEADS + hd + 1] for hd in range(DN_HEADS)], axis=0)
    gc_t = [gc3[c].T for c in range(chunks)]
    decay = jnp.stack([
        jnp.exp(jnp.where(causal, gc3[c][:, DN_HEADS + hd:DN_HEADS + hd + 1]
                          - gc_t[c][DN_HEADS + hd:DN_HEADS + hd + 1, :], -jnp.inf))
        for hd in range(DN_HEADS) for c in range(chunks)], axis=0)

    kb = k * bet
    g_all = jnp.einsum('nid,njd->nij', jnp.concatenate([kb, q], axis=1).astype(BF16), k.astype(BF16),
                       preferred_element_type=F32)
    a_low = jnp.where(strict, g_all[:, :CHUNK] * decay, 0.0)
    attn = (g_all[:, CHUNK:] * decay).astype(BF16)

    l_mat = eye + a_low
    l_bf = l_mat.astype(BF16)
    t_mat = eye - a_low
    for _ in range(4):
        res = eye - _bmm(l_bf, t_mat.astype(BF16))
        t_mat = t_mat + _bmm(t_mat.astype(BF16), res.astype(BF16))
    l_lo = (l_mat - l_bf.astype(F32)).astype(BF16)
    t_hi = t_mat.astype(BF16)
    t_lo = (t_mat - t_hi.astype(F32)).astype(BF16)
    res = eye - (_bmm(l_bf, t_hi) + _bmm(l_lo, t_hi) + _bmm(l_bf, t_lo))
    t_mat = t_mat + _bmm(t_hi, res.astype(BF16))

    eg = jnp.exp(gc)
    uw = _bmm(t_mat.astype(BF16), jnp.concatenate([v * bet, kb * eg], axis=2).astype(BF16))
    u = uw[:, :, :HEAD_DIM]
    wq_lhs = jnp.concatenate([uw[:, :, HEAD_DIM:], q * eg], axis=1).astype(BF16)
    g_last = gc[:, CHUNK - 1:CHUNK, :]
    k_dec = (k * jnp.exp(g_last - gc)).astype(BF16)
    eg_last = jnp.exp(g_last)

    state = [s_ref[hd] for hd in range(DN_HEADS)]
    for c in range(chunks):
        idx = [hd * chunks + c for hd in range(DN_HEADS)]
        wq = [_dot(wq_lhs[n], state[hd].astype(BF16)) for hd, n in enumerate(idx)]
        v_new = [(u[n] - wq[hd][:CHUNK]).astype(BF16) for hd, n in enumerate(idx)]
        o = [wq[hd][CHUNK:] + _dot(attn[n], v_new[hd]) for hd, n in enumerate(idx)]
        state = [state[hd] * eg_last[n] + lax.dot_general(k_dec[n], v_new[hd], (((0,), (0,)), ((), ())),
                                                          preferred_element_type=F32)
                 for hd, n in enumerate(idx)]
        for hd in range(DN_HEADS):
            cols = slice(hd * HEAD_DIM, (hd + 1) * HEAD_DIM)
            rows = slice(c * CHUNK, (c + 1) * CHUNK)
            r = o[hd] * lax.rsqrt(jnp.mean(o[hd] * o[hd], axis=-1, keepdims=True) + 1e-6)
            o_ref[rows, cols] = r * gain * _silu(z_ref[rows, cols])
    for hd in range(DN_HEADS):
        s_ref[hd] = state[hd]

    @pl.when(j == pl.num_programs(1) - 1)
    def _():
        sfin_ref[...] = s_ref[...]


def _delta(q, k, v, z, bg, s0, gain, chunks):
    bsz, seq, _ = q.shape
    rows = chunks * CHUNK
    assert seq % rows == 0

    def row(width):
        return pl.BlockSpec((None, rows, width), lambda b, j: (b, j, 0))

    state_shape = (DN_HEADS, HEAD_DIM, HEAD_DIM)
    return pl.pallas_call(
        functools.partial(_delta_kernel, chunks=chunks),
        grid=(bsz, seq // rows),
        in_specs=[row(DN_WIDTH)] * 4 + [row(LANES), _full_spec(state_shape), _full_spec(gain.shape)],
        out_specs=[row(DN_WIDTH), pl.BlockSpec((None,) + state_shape, lambda b, j: (b, 0, 0, 0))],
        out_shape=[jax.ShapeDtypeStruct((bsz, seq, DN_WIDTH), F32),
                   jax.ShapeDtypeStruct((bsz,) + state_shape, F32)],
        scratch_shapes=[pltpu.VMEM(state_shape, F32)],
        compiler_params=pltpu.CompilerParams(dimension_semantics=("parallel", "arbitrary"),
                                             vmem_limit_bytes=VMEM_LIMIT),
        name="delta",
    )(q, k, v, z, bg, s0, gain)


def _mix_out_kernel(x_ref, o_ref, c_ref, lng_ref, lnb_ref, wo_ref, g1_ref, b1_ref, wrh_ref, wrl_ref, br_ref,
                    h1_ref, route_ref, *, alpha):
    h = _layer_norm(x_ref[...], lng_ref[...], lnb_ref[...])
    dn = o_ref.shape[1]
    mix = _dot(o_ref[...].astype(BF16), wo_ref[0:dn, :]) + _dot(c_ref[...].astype(BF16), wo_ref[dn:, :])
    h1 = _layer_norm(alpha * h + mix, g1_ref[...], b1_ref[...])
    h1_ref[...] = h1

    hh, hl = _split2(h1)
    logits = _dot(hh, wrh_ref[...]) + _dot(hl, wrh_ref[...]) + _dot(hh, wrl_ref[...]) + br_ref[...]
    lane = lax.broadcasted_iota(jnp.int32, logits.shape, 1).astype(F32)
    big = float(LANES)
    neg = -jnp.inf

    def first_argmax(vals):
        top = jnp.max(vals, axis=-1, keepdims=True)
        return top, jnp.min(jnp.where(vals == top, lane, big), axis=-1, keepdims=True)

    grp = jnp.where(lane < N_GROUPS, logits, neg)
    g_top, g_sel = first_argmax(grp)
    p_group = 1.0 / jnp.sum(jnp.exp(grp - g_top), axis=-1, keepdims=True)
    lo = N_GROUPS + EXPERTS_PER_GROUP * g_sel
    in_grp = jnp.where((lane >= lo) & (lane < lo + EXPERTS_PER_GROUP), logits, neg)
    m1, i1 = first_argmax(in_grp)
    m2, i2 = first_argmax(jnp.where(lane == i1, neg, in_grp))
    s = jnp.exp(m2 - m1)
    w1 = p_group / (1.0 + s)
    w2 = p_group * s / (1.0 + s)
    route = jnp.where(lane == 0, i1 - N_GROUPS,
                      jnp.where(lane == 1, i2 - N_GROUPS,
                                jnp.where(lane == 2, w1, jnp.where(lane == 3, w2, 0.0))))
    route_ref[...] = route


def _mix_out(x2d, o2d, c2d, p, tm, alpha):
    n, d = x2d.shape
    assert n % tm == 0

    def row(width):
        return pl.BlockSpec((tm, width), lambda i: (i, 0))

    consts = [p['ln_emb_g'], p['ln_emb_b'], p['w_out'], p['ln1_g'], p['ln1_b'], p['w_r_hi'], p['w_r_lo'], p['b_r']]
    return pl.pallas_call(
        functools.partial(_mix_out_kernel, alpha=alpha),
        grid=(n // tm,),
        in_specs=[row(d), row(o2d.shape[1]), row(c2d.shape[1])] + [_full_spec(c.shape) for c in consts],
        out_specs=[row(d), row(LANES)],
        out_shape=[jax.ShapeDtypeStruct((n, d), F32), jax.ShapeDtypeStruct((n, LANES), F32)],
        compiler_params=pltpu.CompilerParams(dimension_semantics=("parallel",), vmem_limit_bytes=VMEM_LIMIT),
        name="mix_out",
    )(x2d, o2d, c2d, *consts)


def _expert_kernel(be_ref, tok_ref, h1_hbm, wg_ref, wu_ref, wd_ref, y_ref, xbuf, sem, *, bm):
    del be_ref
    i = pl.program_id(0)
    nb = pl.num_programs(0)

    def row_copy(blk, slot, r):
        tok = tok_ref[blk * bm + r]
        return pltpu.make_async_copy(h1_hbm.at[pl.ds(tok, 1), :], xbuf.at[slot, pl.ds(r, 1), :], sem.at[slot])

    def issue(blk, slot):
        def body(r, carry):
            row_copy(blk, slot, r).start()
            return carry
        lax.fori_loop(0, bm, body, 0, unroll=DMA_UNROLL)

    @pl.when(i == 0)
    def _():
        issue(0, 0)

    @pl.when(i + 1 < nb)
    def _():
        issue(i + 1, (i + 1) % 2)

    slot = i % 2
    pltpu.make_async_copy(h1_hbm.at[pl.ds(0, bm), :], xbuf.at[slot], sem.at[slot]).wait()

    xb = xbuf[slot].astype(BF16)
    hid = _silu(_dot(xb, wg_ref[...])) * _dot(xb, wu_ref[...])
    y_ref[...] = _dot(hid.astype(BF16), wd_ref[...])


def _experts(block_expert, buf_tok, h1, w_gate, w_up, w_down, bm):
    n_blocks = block_expert.shape[0]
    d = h1.shape[1]
    ff = w_gate.shape[2]
    grid_spec = pltpu.PrefetchScalarGridSpec(
        num_scalar_prefetch=2,
        grid=(n_blocks,),
        in_specs=[pl.BlockSpec(memory_space=pl.ANY),
                  pl.BlockSpec((None, d, ff), lambda i, be, tok: (be[i], 0, 0)),
                  pl.BlockSpec((None, d, ff), lambda i, be, tok: (be[i], 0, 0)),
                  pl.BlockSpec((None, ff, d), lambda i, be, tok: (be[i], 0, 0))],
        out_specs=pl.BlockSpec((bm, d), lambda i, be, tok: (i, 0)),
        scratch_shapes=[pltpu.VMEM((2, bm, d), F32), pltpu.SemaphoreType.DMA((2,))],
    )
    return pl.pallas_call(
        functools.partial(_expert_kernel, bm=bm),
        grid_spec=grid_spec,
        out_shape=jax.ShapeDtypeStruct((n_blocks * bm, d), F32),
        compiler_params=pltpu.CompilerParams(dimension_semantics=("arbitrary",), vmem_limit_bytes=VMEM_LIMIT,
                                             disable_bounds_checks=True),
        name="experts",
    )(block_expert, buf_tok, h1, w_gate, w_up, w_down)


def _combine_kernel(dest_ref, y_hbm, h1_ref, route_ref, g2_ref, b2_ref, out_ref, ybuf, sem, *, tm, alpha):
    i = pl.program_id(0)
    nb = pl.num_programs(0)

    def row_copy(blk, slot, r, k):
        dst_row = dest_ref[(blk * tm + r) * TOP_K + k]
        return pltpu.make_async_copy(y_hbm.at[pl.ds(dst_row, 1), :], ybuf.at[slot, k, pl.ds(r, 1), :],
                                     sem.at[slot])

    def issue(blk, slot):
        def body(r, carry):
            for k in range(TOP_K):
                row_copy(blk, slot, r, k).start()
            return carry
        lax.fori_loop(0, tm, body, 0, unroll=DMA_UNROLL)

    @pl.when(i == 0)
    def _():
        issue(0, 0)

    @pl.when(i + 1 < nb)
    def _():
        issue(i + 1, (i + 1) % 2)

    slot = i % 2
    for k in range(TOP_K):
        pltpu.make_async_copy(y_hbm.at[pl.ds(0, tm), :], ybuf.at[slot, k], sem.at[slot]).wait()

    route = route_ref[...]
    ffn = ybuf[slot, 0] * route[:, 2:3] + ybuf[slot, 1] * route[:, 3:4]
    out_ref[...] = _layer_norm(alpha * h1_ref[...] + ffn, g2_ref[...], b2_ref[...])


def _combine(dest, y_sorted, h1, route, ln2_g, ln2_b, tm, alpha):
    n, d = h1.shape
    assert n % tm == 0
    grid_spec = pltpu.PrefetchScalarGridSpec(
        num_scalar_prefetch=1,
        grid=(n // tm,),
        in_specs=[pl.BlockSpec(memory_space=pl.ANY),
                  pl.BlockSpec((tm, d), lambda i, dest: (i, 0)),
                  pl.BlockSpec((tm, LANES), lambda i, dest: (i, 0)),
                  pl.BlockSpec((1, d), lambda i, dest: (0, 0)),
                  pl.BlockSpec((1, d), lambda i, dest: (0, 0))],
        out_specs=pl.BlockSpec((tm, d), lambda i, dest: (i, 0)),
        scratch_shapes=[pltpu.VMEM((2, TOP_K, tm, d), F32), pltpu.SemaphoreType.DMA((2,))],
    )
    return pl.pallas_call(
        functools.partial(_combine_kernel, tm=tm, alpha=alpha),
        grid_spec=grid_spec,
        out_shape=jax.ShapeDtypeStruct((n, d), F32),
        compiler_params=pltpu.CompilerParams(dimension_semantics=("arbitrary",), vmem_limit_bytes=VMEM_LIMIT,
                                             disable_bounds_checks=True),
        name="combine",
    )(dest, y_sorted, h1, route, ln2_g, ln2_b)


def _dispatch_plan(expert_id, bm):
    n = expert_id.shape[0]
    nk = n * TOP_K
    flat = expert_id.reshape(-1)
    onehot = (flat[:, None] == jnp.arange(N_EXPERTS, dtype=jnp.int32)[None, :]).astype(jnp.int32)
    csum = jnp.cumsum(onehot, axis=0)
    rank = jnp.take_along_axis(csum, flat[:, None], axis=1)[:, 0] - 1
    counts = csum[-1]
    padded = (counts + bm - 1) // bm * bm
    pad_end = jnp.cumsum(padded)
    pad_start = pad_end - padded
    dest = pad_start[flat] + rank
    n_blocks = (nk + bm - 1) // bm + N_EXPERTS
    token_id = jnp.arange(nk, dtype=jnp.int32) // TOP_K
    buf_tok = jnp.zeros((n_blocks * bm,), jnp.int32).at[dest].set(token_id)
    block_expert = jnp.minimum(
        jnp.searchsorted(pad_end, jnp.arange(n_blocks, dtype=jnp.int32) * bm, side='right'),
        N_EXPERTS - 1).astype(jnp.int32)
    return dest.astype(jnp.int32), buf_tok, block_expert


def _pad_lanes(w, width=LANES):
    return jnp.pad(w, [(0, 0)] * (w.ndim - 1) + [(0, width - w.shape[-1])])


def kernel(x, meta_tokens, ln_emb_g, ln_emb_b, w_in, conv_qkv_w, a_log, dt_bias, dn_norm_g, conv_dw_w, conv_dw_b, cv_norm_g, cv_norm_b, w_out, ln1_g, ln1_b, w_group, b_group, w_router, b_router, w_exp_gate, w_exp_up, w_exp_down, ln2_g, ln2_b):
    depth = w_in.shape[0]
    assert depth == 1, "single-layer block"
    bsz, seq, d = x.shape
    alpha = (2.0 * depth) ** 0.25
    qkv_w = 3 * DN_WIDTH
    w_in0 = w_in[0]
    glu_off = 4 * DN_WIDTH + 2 * DN_HEADS
    row = lambda a: a.reshape(1, -1).astype(F32)
    p = {
        'ln_emb_g': row(ln_emb_g), 'ln_emb_b': row(ln_emb_b),
        'w_qkv': w_in0[:, :qkv_w].astype(BF16),
        'w_z': w_in0[:, qkv_w:4 * DN_WIDTH].astype(BF16),
        'w_ba': _pad_lanes(w_in0[:, 4 * DN_WIDTH:glu_off]).astype(BF16),
        'w_glu': w_in0[:, glu_off:].astype(BF16),
        'conv_w': conv_qkv_w[0].astype(F32),
        'neg_a': _pad_lanes(jnp.concatenate([jnp.zeros((DN_HEADS,), F32), -jnp.exp(a_log[0].astype(F32))])[None]),
        'dt_b': _pad_lanes(jnp.concatenate([jnp.zeros((DN_HEADS,), F32), dt_bias[0].astype(F32)])[None]),
        'dw_w': conv_dw_w[0].astype(F32), 'dw_b': row(conv_dw_b[0]),
        'cv_g': row(cv_norm_g[0]), 'cv_b': row(cv_norm_b[0]),
        'w_out': w_out[0].astype(BF16), 'ln1_g': row(ln1_g[0]), 'ln1_b': row(ln1_b[0]),
    }
    w_r = _pad_lanes(jnp.concatenate([w_group[0], w_router[0]], axis=1).astype(F32))
    p['w_r_hi'] = w_r.astype(BF16)
    p['w_r_lo'] = (w_r - p['w_r_hi'].astype(F32)).astype(BF16)
    p['b_r'] = _pad_lanes(jnp.concatenate([b_group[0], b_router[0]])[None].astype(F32))
    gain = row(dn_norm_g[0])

    conf_w = p['dw_w'].shape[1]
    zero_hq = jnp.zeros((QKV_HALO, qkv_w), F32)
    zero_hc = jnp.zeros((CONF_HALO, conf_w), F32)
    mq, mk, mv, mz, _, mbg, halo_q, halo_c = _mix_in(meta_tokens[None].astype(F32), p, zero_hq, zero_hc, N_META)
    front = lambda a: jnp.pad(a, [(0, 0), (CHUNK - N_META, 0), (0, 0)])
    s_zero = jnp.zeros((DN_HEADS, HEAD_DIM, HEAD_DIM), F32)
    _, s_meta = _delta(front(mq), front(mk), front(mv), front(mz), front(mbg), s_zero, gain, 1)

    q, k, v, z, c, bg, _, _ = _mix_in(x, p, halo_q[0], halo_c[0], TM_IN)
    o, _ = _delta(q, k, v, z, bg, s_meta[0], gain, DELTA_CHUNKS)

    n = bsz * seq
    h1, route = _mix_out(x.reshape(n, d), o.reshape(n, DN_WIDTH), c.reshape(n, conf_w), p, TM_OUT, alpha)

    expert_id = route[:, :TOP_K].astype(jnp.int32)
    dest, buf_tok, block_expert = _dispatch_plan(expert_id, BM_EXPERT)
    y_sorted = _experts(block_expert, buf_tok, h1, w_exp_gate[0].astype(BF16), w_exp_up[0].astype(BF16),
                        w_exp_down[0].astype(BF16), BM_EXPERT)
    out = _combine(dest, y_sorted, h1, route, row(ln2_g[0]), row(ln2_b[0]), TM_COMBINE, alpha)
    return out.reshape(bsz, seq, d)
```

```python
import functools

import jax
import jax.numpy as jnp
from jax import lax
from jax.experimental import pallas as pl
from jax.experimental.pallas import tpu as pltpu

F32 = jnp.float32
BF16 = jnp.bfloat16

NORM_EPS = 1e-5
N_META = 16
DN_HEADS = 4
HEAD_DIM = 128
DN_WIDTH = DN_HEADS * HEAD_DIM
CHUNK = 64
SHORT_CONV = 4
CONF_KERNEL = 31
N_GROUPS = 4
EXPERTS_PER_GROUP = 8
N_EXPERTS = N_GROUPS * EXPERTS_PER_GROUP
TOP_K = 2
LANES = 128
SUBLANES = 8
QKV_HALO = 8
CONF_HALO = 32
VMEM_LIMIT = 56 * 1024 * 1024

TM_IN = 256
DELTA_CHUNKS = 8
TM_OUT = 256
BM_EXPERT = 256
TM_COMBINE = 256
TD_DISPATCH = 512
DMA_UNROLL = 8


def _dot(a, b):
    return jnp.dot(a, b, preferred_element_type=F32)


def _split2(x):
    hi = x.astype(BF16)
    lo = (x - hi.astype(F32)).astype(BF16)
    return hi, lo


def _dot_hilo(a, b):
    ah, al = _split2(a)
    bh, bl = _split2(b)
    return _dot(ah, bh) + _dot(al, bh) + _dot(ah, bl)


def _dot_exact01(m01, x):
    x1 = x.astype(BF16)
    r1 = x - x1.astype(F32)
    x2 = r1.astype(BF16)
    x3 = (r1 - x2.astype(F32)).astype(BF16)
    return _dot(m01, x1) + _dot(m01, x2) + _dot(m01, x3)


def _sigmoid(x):
    return 1.0 / (1.0 + jnp.exp(-x))


def _silu(x):
    return x * _sigmoid(x)


def _layer_norm(x, g, b):
    mu = jnp.mean(x, axis=-1, keepdims=True)
    xc = x - mu
    var = jnp.mean(xc * xc, axis=-1, keepdims=True)
    return xc * lax.rsqrt(var + NORM_EPS) * g + b


def _store_slabs(ref, val):
    rows, d = val.shape
    n_slabs = d // LANES
    for s in range(n_slabs):
        ref[pl.ds(s, rows, stride=n_slabs), :] = val[:, s * LANES:(s + 1) * LANES]


def _load_slabs(ref, rows, n_slabs, base=0):
    return jnp.concatenate([ref[pl.ds(base + s, rows, stride=n_slabs), :] for s in range(n_slabs)], axis=1)


def _full_spec(shape):
    nd = len(shape)
    return pl.BlockSpec(shape, lambda *_: (0,) * nd)


def _mix_in_kernel(x_ref, lng_ref, lnb_ref, wqkv_ref, wz_ref, wglu_ref, wba_ref, cw_ref, nega_ref,
                   dtb_ref, dww_ref, dwb_ref, cvg_ref, cvb_ref, hq_in_ref, hc_in_ref,
                   q_ref, k_ref, v_ref, z_ref, c_ref, bg_ref, hq_out_ref, hc_out_ref,
                   qkv_ext, c_ext):
    tm = x_ref.shape[0]

    @pl.when(pl.program_id(1) == 0)
    def _():
        qkv_ext[0:QKV_HALO, :] = hq_in_ref[...]
        c_ext[0:CONF_HALO, :] = hc_in_ref[...]

    h = _layer_norm(x_ref[...], lng_ref[...], lnb_ref[...])
    hb = h.astype(BF16)

    qkv_ext[QKV_HALO:QKV_HALO + tm, :] = _dot(hb, wqkv_ref[...])
    acc = None
    for tap in range(SHORT_CONV):
        term = qkv_ext[pl.ds(QKV_HALO - (SHORT_CONV - 1) + tap, tm), :] * cw_ref[tap:tap + 1, :]
        acc = term if acc is None else acc + term
    qkv = _silu(acc)
    for hd in range(DN_HEADS):
        lo = hd * HEAD_DIM
        qh = qkv[:, lo:lo + HEAD_DIM]
        kh = qkv[:, DN_WIDTH + lo:DN_WIDTH + lo + HEAD_DIM]
        q_ref[:, lo:lo + HEAD_DIM] = qh * (lax.rsqrt(jnp.sum(qh * qh, axis=-1, keepdims=True) + 1e-6)
                                           * (HEAD_DIM ** -0.5))
        k_ref[:, lo:lo + HEAD_DIM] = kh * lax.rsqrt(jnp.sum(kh * kh, axis=-1, keepdims=True) + 1e-6)
    v_ref[...] = qkv[:, 2 * DN_WIDTH:]
    z_ref[...] = _dot(hb, wz_ref[...])

    ba = _dot(hb, wba_ref[...])
    lane = lax.broadcasted_iota(jnp.int32, ba.shape, 1)
    sp_in = ba + dtb_ref[...]
    softplus = jnp.maximum(sp_in, 0.0) + jnp.log(1.0 + jnp.exp(-jnp.abs(sp_in)))
    bg_ref[...] = jnp.where(lane < DN_HEADS, _sigmoid(ba), nega_ref[...] * softplus)

    glu = _dot(hb, wglu_ref[...])
    cw = glu.shape[1] // 2
    c_ext[CONF_HALO:CONF_HALO + tm, :] = glu[:, :cw] * _sigmoid(glu[:, cw:])
    acc = None
    for tap in range(CONF_KERNEL):
        term = c_ext[pl.ds(CONF_HALO - (CONF_KERNEL - 1) + tap, tm), :] * dww_ref[tap:tap + 1, :]
        acc = term if acc is None else acc + term
    conv = acc + dwb_ref[...]
    c_ref[...] = _silu(_layer_norm(conv, cvg_ref[...], cvb_ref[...]))

    q_tail = qkv_ext[tm:tm + QKV_HALO, :]
    c_tail = c_ext[tm:tm + CONF_HALO, :]
    qkv_ext[0:QKV_HALO, :] = q_tail
    c_ext[0:CONF_HALO, :] = c_tail
    hq_out_ref[...] = q_tail
    hc_out_ref[...] = c_tail


def _mix_in(x, p, halo_q, halo_c, tm):
    bsz, seq, d = x.shape
    assert seq % tm == 0
    qkv_w = 3 * DN_WIDTH
    conf_w = p['dw_w'].shape[1]

    def row(width):
        return pl.BlockSpec((None, tm, width), lambda b, t: (b, t, 0))

    def per_batch(rows, width):
        return pl.BlockSpec((None, rows, width), lambda b, t: (b, 0, 0))

    consts = [p['ln_emb_g'], p['ln_emb_b'], p['w_qkv'], p['w_z'], p['w_glu'], p['w_ba'], p['conv_w'],
              p['neg_a'], p['dt_b'], p['dw_w'], p['dw_b'], p['cv_g'], p['cv_b'], halo_q, halo_c]
    sds = jax.ShapeDtypeStruct
    out_shape = ([sds((bsz, seq, DN_WIDTH), F32)] * 4 + [sds((bsz, seq, conf_w), F32),
                 sds((bsz, seq, LANES), F32), sds((bsz, QKV_HALO, qkv_w), F32),
                 sds((bsz, CONF_HALO, conf_w), F32)])
    out_specs = ([row(DN_WIDTH)] * 4 + [row(conf_w), row(LANES), per_batch(QKV_HALO, qkv_w),
                 per_batch(CONF_HALO, conf_w)])
    return pl.pallas_call(
        _mix_in_kernel,
        grid=(bsz, seq // tm),
        in_specs=[row(d)] + [_full_spec(c.shape) for c in consts],
        out_specs=out_specs,
        out_shape=out_shape,
        scratch_shapes=[pltpu.VMEM((QKV_HALO + tm, qkv_w), F32), pltpu.VMEM((CONF_HALO + tm, conf_w), F32)],
        compiler_params=pltpu.CompilerParams(dimension_semantics=("parallel", "arbitrary"),
                                             vmem_limit_bytes=VMEM_LIMIT),
        name="mix_in",
    )(x, *consts)


def _bmm(a, b):
    return jnp.einsum('nij,njk->nik', a, b, preferred_element_type=F32)


def _delta_kernel(q_ref, k_ref, v_ref, z_ref, bg_ref, s0_ref, gain_ref, o_ref, sfin_ref, s_ref, *, chunks):
    j = pl.program_id(1)

    @pl.when(j == 0)
    def _():
        s_ref[...] = s0_ref[...]

    ii = lax.broadcasted_iota(jnp.int32, (CHUNK, CHUNK), 0)
    jj = lax.broadcasted_iota(jnp.int32, (CHUNK, CHUNK), 1)
    causal = ii >= jj
    strict = ii > jj
    eye = (ii == jj).astype(F32)
    gain = gain_ref[...]

    bg3 = bg_ref[...].reshape(chunks, CHUNK, LANES)
    tril_b = jnp.broadcast_to(causal.astype(BF16), (chunks, CHUNK, CHUNK))
    p1 = bg3.astype(BF16)
    r1 = bg3 - p1.astype(F32)
    p2 = r1.astype(BF16)
    p3 = (r1 - p2.astype(F32)).astype(BF16)
    gc3 = _bmm(tril_b, p1) + _bmm(tril_b, p2) + _bmm(tril_b, p3)

    def heads(ref):
        return jnp.concatenate([ref[:, hd * HEAD_DIM:(hd + 1) * HEAD_DIM].reshape(chunks, CHUNK, HEAD_DIM)
                                for hd in range(DN_HEADS)], axis=0)
    q = heads(q_ref)
    k = heads(k_ref)
    v = heads(v_ref)
    bet = jnp.concatenate([bg3[:, :, hd:hd + 1] for hd in range(DN_HEADS)], axis=0)
    gc = jnp.concatenate([gc3[:, :, DN_HEADS + hd:DN_HEADS + hd + 1] for hd in range(DN_HEADS)], axis=0)
    gc_t = [gc3[c].T for c in range(chunks)]
    decay = jnp.stack([
        jnp.exp(jnp.where(causal, gc3[c][:, DN_HEADS + hd:DN_HEADS + hd + 1]
                          - gc_t[c][DN_HEADS + hd:DN_HEADS + hd + 1, :], -jnp.inf))
        for hd in range(DN_HEADS) for c in range(chunks)], axis=0)

    kb = k * bet
    g_all = jnp.einsum('nid,njd->nij', jnp.concatenate([kb, q], axis=1).astype(BF16), k.astype(BF16),
                       preferred_element_type=F32)
    a_low = jnp.where(strict, g_all[:, :CHUNK] * decay, 0.0)
    attn = (g_all[:, CHUNK:] * decay).astype(BF16)

    l_mat = eye + a_low
    l_bf = l_mat.astype(BF16)
    t_mat = eye - a_low
    for _ in range(4):
        res = eye - _bmm(l_bf, t_mat.astype(BF16))
        t_mat = t_mat + _bmm(t_mat.astype(BF16), res.astype(BF16))
    l_lo = (l_mat - l_bf.astype(F32)).astype(BF16)
    t_hi = t_mat.astype(BF16)
    t_lo = (t_mat - t_hi.astype(F32)).astype(BF16)
    res = eye - (_bmm(l_bf, t_hi) + _bmm(l_lo, t_hi) + _bmm(l_bf, t_lo))
    t_mat = t_mat + _bmm(t_hi, res.astype(BF16))

    eg = jnp.exp(gc)
    uw = _bmm(t_mat.astype(BF16), jnp.concatenate([v * bet, kb * eg], axis=2).astype(BF16))
    u = uw[:, :, :HEAD_DIM]
    wq_lhs = jnp.concatenate([uw[:, :, HEAD_DIM:], q * eg], axis=1).astype(BF16)
    g_last = gc[:, CHUNK - 1:CHUNK, :]
    k_dec = (k * jnp.exp(g_last - gc)).astype(BF16)
    eg_last = jnp.exp(g_last)

    state = [s_ref[hd] for hd in range(DN_HEADS)]
    for c in range(chunks):
        idx = [hd * chunks + c for hd in range(DN_HEADS)]
        wq = [_dot(wq_lhs[n], state[hd].astype(BF16)) for hd, n in enumerate(idx)]
        v_new = [(u[n] - wq[hd][:CHUNK]).astype(BF16) for hd, n in enumerate(idx)]
        o = [wq[hd][CHUNK:] + _dot(attn[n], v_new[hd]) for hd, n in enumerate(idx)]
        state = [state[hd] * eg_last[n] + lax.dot_general(k_dec[n], v_new[hd], (((0,), (0,)), ((), ())),
                                                          preferred_element_type=F32)
                 for hd, n in enumerate(idx)]
        for hd in range(DN_HEADS):
            cols = slice(hd * HEAD_DIM, (hd + 1) * HEAD_DIM)
            rows = slice(c * CHUNK, (c + 1) * CHUNK)
            r = o[hd] * lax.rsqrt(jnp.mean(o[hd] * o[hd], axis=-1, keepdims=True) + 1e-6)
            o_ref[rows, cols] = r * gain * _silu(z_ref[rows, cols])
    for hd in range(DN_HEADS):
        s_ref[hd] = state[hd]

    @pl.when(j == pl.num_programs(1) - 1)
    def _():
        sfin_ref[...] = s_ref[...]


def _delta(q, k, v, z, bg, s0, gain, chunks):
    bsz, seq, _ = q.shape
    rows = chunks * CHUNK
    assert seq % rows == 0

    def row(width):
        return pl.BlockSpec((None, rows, width), lambda b, j: (b, j, 0))

    state_shape = (DN_HEADS, HEAD_DIM, HEAD_DIM)
    return pl.pallas_call(
        functools.partial(_delta_kernel, chunks=chunks),
        grid=(bsz, seq // rows),
        in_specs=[row(DN_WIDTH)] * 4 + [row(LANES), _full_spec(state_shape), _full_spec(gain.shape)],
        out_specs=[row(DN_WIDTH), pl.BlockSpec((None,) + state_shape, lambda b, j: (b, 0, 0, 0))],
        out_shape=[jax.ShapeDtypeStruct((bsz, seq, DN_WIDTH), F32),
                   jax.ShapeDtypeStruct((bsz,) + state_shape, F32)],
        scratch_shapes=[pltpu.VMEM(state_shape, F32)],
        compiler_params=pltpu.CompilerParams(dimension_semantics=("parallel", "arbitrary"),
                                             vmem_limit_bytes=VMEM_LIMIT),
        name="delta",
    )(q, k, v, z, bg, s0, gain)


def _mix_out_kernel(x_ref, o_ref, c_ref, lng_ref, lnb_ref, wo_ref, g1_ref, b1_ref, wrh_ref, wrl_ref, br_ref,
                    h1s_ref, route_ref, cnt_out_ref, cnt_ref, *, alpha):
    @pl.when(pl.program_id(0) == 0)
    def _():
        cnt_ref[...] = jnp.zeros_like(cnt_ref)

    h = _layer_norm(x_ref[...], lng_ref[...], lnb_ref[...])
    dn = o_ref.shape[1]
    mix = _dot(o_ref[...].astype(BF16), wo_ref[0:dn, :]) + _dot(c_ref[...].astype(BF16), wo_ref[dn:, :])
    h1 = _layer_norm(alpha * h + mix, g1_ref[...], b1_ref[...])
    _store_slabs(h1s_ref, h1)

    hh, hl = _split2(h1)
    logits = _dot(hh, wrh_ref[...]) + _dot(hl, wrh_ref[...]) + _dot(hh, wrl_ref[...]) + br_ref[...]
    lane = lax.broadcasted_iota(jnp.int32, logits.shape, 1).astype(F32)
    big = float(LANES)
    neg = -jnp.inf

    def first_argmax(vals):
        top = jnp.max(vals, axis=-1, keepdims=True)
        return top, jnp.min(jnp.where(vals == top, lane, big), axis=-1, keepdims=True)

    grp = jnp.where(lane < N_GROUPS, logits, neg)
    g_top, g_sel = first_argmax(grp)
    p_group = 1.0 / jnp.sum(jnp.exp(grp - g_top), axis=-1, keepdims=True)
    lo = N_GROUPS + EXPERTS_PER_GROUP * g_sel
    in_grp = jnp.where((lane >= lo) & (lane < lo + EXPERTS_PER_GROUP), logits, neg)
    m1, i1 = first_argmax(in_grp)
    m2, i2 = first_argmax(jnp.where(lane == i1, neg, in_grp))
    s = jnp.exp(m2 - m1)
    w1 = p_group / (1.0 + s)
    w2 = p_group * s / (1.0 + s)
    e1 = i1 - N_GROUPS
    e2 = i2 - N_GROUPS

    tm = logits.shape[0]
    oh1 = (lane == e1).astype(F32)
    oh2 = (lane == e2).astype(F32)
    both = oh1 + oh2
    ti = lax.broadcasted_iota(jnp.int32, (tm, tm), 0)
    tj = lax.broadcasted_iota(jnp.int32, (tm, tm), 1)
    base = _dot((ti > tj).astype(BF16), both.astype(BF16)) + cnt_ref[...]
    r1 = jnp.sum(oh1 * base, axis=-1, keepdims=True)
    r2 = jnp.sum(oh2 * base, axis=-1, keepdims=True)
    cnt_ref[...] = cnt_ref[...] + jnp.sum(both, axis=0, keepdims=True)
    cnt_out_ref[...] = jnp.broadcast_to(cnt_ref[...], cnt_out_ref.shape)

    vals = (e1, e2, w1, w2, r1, r2)
    route = jnp.zeros_like(logits)
    for idx, val in enumerate(vals):
        route = jnp.where(lane == idx, val, route)
    route_ref[...] = route


def _mix_out(x2d, o2d, c2d, p, tm, alpha):
    n, d = x2d.shape
    assert n % tm == 0
    slabs = d // LANES

    def row(width):
        return pl.BlockSpec((tm, width), lambda i: (i, 0))

    consts = [p['ln_emb_g'], p['ln_emb_b'], p['w_out'], p['ln1_g'], p['ln1_b'], p['w_r_hi'], p['w_r_lo'], p['b_r']]
    return pl.pallas_call(
        functools.partial(_mix_out_kernel, alpha=alpha),
        grid=(n // tm,),
        in_specs=[row(d), row(o2d.shape[1]), row(c2d.shape[1])] + [_full_spec(c.shape) for c in consts],
        out_specs=[pl.BlockSpec((tm * slabs, LANES), lambda i: (i, 0)), row(LANES), _full_spec((SUBLANES, LANES))],
        out_shape=[jax.ShapeDtypeStruct((n * slabs, LANES), F32), jax.ShapeDtypeStruct((n, LANES), F32),
                   jax.ShapeDtypeStruct((SUBLANES, LANES), F32)],
        scratch_shapes=[pltpu.VMEM((1, LANES), F32)],
        compiler_params=pltpu.CompilerParams(dimension_semantics=("arbitrary",), vmem_limit_bytes=VMEM_LIMIT),
        name="mix_out",
    )(x2d, o2d, c2d, *consts)


def _dispatch_kernel(dest_ref, pad_lo_ref, pad_hi_ref, h1s_hbm, xs_hbm, zslab, sem, zsem, *, td, n_slabs):
    i = pl.program_id(0)
    nb = pl.num_programs(0)
    slot = i % 2

    def pad_copy(row):
        return pltpu.make_async_copy(zslab, xs_hbm.at[pl.ds(pl.multiple_of(row * n_slabs, n_slabs), n_slabs), :],
                                     zsem.at[0])

    def for_each_pad_row(fn):
        def per_expert(e, carry):
            def per_row(row, c2):
                fn(row)
                return c2
            return lax.fori_loop(pad_lo_ref[e], pad_hi_ref[e], per_row, carry)
        lax.fori_loop(0, pad_lo_ref.shape[0], per_expert, 0)

    @pl.when(i == 0)
    def _():
        zslab[...] = jnp.zeros_like(zslab)
        for_each_pad_row(lambda row: pad_copy(row).start())

    def issue_body(r, carry):
        tok = i * td + r
        src = h1s_hbm.at[pl.ds(pl.multiple_of(tok * n_slabs, n_slabs), n_slabs), :]
        for k in range(TOP_K):
            dst_row = pl.multiple_of(dest_ref[tok * TOP_K + k], n_slabs)
            pltpu.make_async_copy(src, xs_hbm.at[pl.ds(dst_row, n_slabs), :], sem.at[slot]).start()
        return carry
    lax.fori_loop(0, td, issue_body, 0, unroll=DMA_UNROLL)

    def wait_step(s):
        rows = TOP_K * td * n_slabs
        pltpu.make_async_copy(h1s_hbm.at[pl.ds(0, rows), :], xs_hbm.at[pl.ds(0, rows), :], sem.at[s]).wait()

    @pl.when(i > 0)
    def _():
        wait_step(1 - slot)

    @pl.when(i == nb - 1)
    def _():
        wait_step(slot)
        for_each_pad_row(lambda row: pad_copy(row).wait())


def _dispatch(dest_rows, pad_lo, pad_hi, h1s, cap_rows, td, n_slabs):
    n = h1s.shape[0] // n_slabs
    assert n % td == 0
    grid_spec = pltpu.PrefetchScalarGridSpec(
        num_scalar_prefetch=3,
        grid=(n // td,),
        in_specs=[pl.BlockSpec(memory_space=pl.ANY)],
        out_specs=pl.BlockSpec(memory_space=pl.ANY),
        scratch_shapes=[pltpu.VMEM((n_slabs, LANES), F32), pltpu.SemaphoreType.DMA((2,)),
                        pltpu.SemaphoreType.DMA((1,))],
    )
    return pl.pallas_call(
        functools.partial(_dispatch_kernel, td=td, n_slabs=n_slabs),
        grid_spec=grid_spec,
        out_shape=jax.ShapeDtypeStruct((cap_rows * n_slabs, LANES), F32),
        compiler_params=pltpu.CompilerParams(dimension_semantics=("arbitrary",), disable_bounds_checks=True),
        name="dispatch",
    )(dest_rows, pad_lo, pad_hi, h1s)


def _expert_kernel(be_ref, nu_ref, xs_ref, wg_ref, wu_ref, wd_ref, y_ref, *, bm, n_slabs):
    del be_ref
    used = pl.program_id(0) < nu_ref[0]

    @pl.when(used)
    def _():
        xb = _load_slabs(xs_ref, bm, n_slabs).astype(BF16)
        hid = _silu(_dot(xb, wg_ref[...])) * _dot(xb, wu_ref[...])
        _store_slabs(y_ref, _dot(hid.astype(BF16), wd_ref[...]))

    @pl.when(jnp.logical_not(used))
    def _():
        y_ref[...] = jnp.zeros_like(y_ref)


def _experts(block_expert, n_used, xs, w_gate, w_up, w_down, bm):
    n_blocks = block_expert.shape[0]
    d = w_gate.shape[1]
    ff = w_gate.shape[2]
    n_slabs = d // LANES

    def blk(i, be, nu):
        return jnp.minimum(i, nu[0] - 1)

    grid_spec = pltpu.PrefetchScalarGridSpec(
        num_scalar_prefetch=2,
        grid=(n_blocks,),
        in_specs=[pl.BlockSpec((bm * n_slabs, LANES), lambda i, be, nu: (blk(i, be, nu), 0)),
                  pl.BlockSpec((None, d, ff), lambda i, be, nu: (be[blk(i, be, nu)], 0, 0)),
                  pl.BlockSpec((None, d, ff), lambda i, be, nu: (be[blk(i, be, nu)], 0, 0)),
                  pl.BlockSpec((None, ff, d), lambda i, be, nu: (be[blk(i, be, nu)], 0, 0))],
        out_specs=pl.BlockSpec((bm * n_slabs, LANES), lambda i, be, nu: (i, 0)),
    )
    return pl.pallas_call(
        functools.partial(_expert_kernel, bm=bm, n_slabs=n_slabs),
        grid_spec=grid_spec,
        out_shape=jax.ShapeDtypeStruct(xs.shape, F32),
        compiler_params=pltpu.CompilerParams(dimension_semantics=("arbitrary",), vmem_limit_bytes=VMEM_LIMIT),
        name="experts",
    )(block_expert, n_used, xs, w_gate, w_up, w_down)


def _combine_kernel(dest_ref, y_hbm, h1s_ref, route_ref, g2_ref, b2_ref, out_ref, ybuf, sem, *, tm, n_slabs, alpha):
    i = pl.program_id(0)
    nb = pl.num_programs(0)
    part = tm * n_slabs

    def issue(blk, slot):
        def body(r, carry):
            for k in range(TOP_K):
                src_row = pl.multiple_of(dest_ref[(blk * tm + r) * TOP_K + k], n_slabs)
                dst_row = pl.multiple_of((slot * TOP_K + k) * part + r * n_slabs, n_slabs)
                pltpu.make_async_copy(y_hbm.at[pl.ds(src_row, n_slabs), :], ybuf.at[pl.ds(dst_row, n_slabs), :],
                                      sem.at[slot]).start()
            return carry
        lax.fori_loop(0, tm, body, 0, unroll=DMA_UNROLL)

    @pl.when(i == 0)
    def _():
        issue(0, 0)

    @pl.when(i + 1 < nb)
    def _():
        issue(i + 1, (i + 1) % 2)

    slot = i % 2
    base = pl.multiple_of(slot * (TOP_K * part), TOP_K * part)
    pltpu.make_async_copy(y_hbm.at[pl.ds(0, TOP_K * part), :], ybuf.at[pl.ds(base, TOP_K * part), :],
                          sem.at[slot]).wait()

    route = route_ref[...]
    ffn = (_load_slabs(ybuf, tm, n_slabs, base) * route[:, 2:3]
           + _load_slabs(ybuf, tm, n_slabs, base + part) * route[:, 3:4])
    h1 = _load_slabs(h1s_ref, tm, n_slabs)
    out_ref[...] = _layer_norm(alpha * h1 + ffn, g2_ref[...], b2_ref[...])


def _combine(dest_rows, y_sorted, h1s, route, ln2_g, ln2_b, tm, alpha):
    d = ln2_g.shape[1]
    n_slabs = d // LANES
    n = h1s.shape[0] // n_slabs
    assert n % tm == 0
    grid_spec = pltpu.PrefetchScalarGridSpec(
        num_scalar_prefetch=1,
        grid=(n // tm,),
        in_specs=[pl.BlockSpec(memory_space=pl.ANY),
                  pl.BlockSpec((tm * n_slabs, LANES), lambda i, dest: (i, 0)),
                  pl.BlockSpec((tm, LANES), lambda i, dest: (i, 0)),
                  pl.BlockSpec((1, d), lambda i, dest: (0, 0)),
                  pl.BlockSpec((1, d), lambda i, dest: (0, 0))],
        out_specs=pl.BlockSpec((tm, d), lambda i, dest: (i, 0)),
        scratch_shapes=[pltpu.VMEM((2 * TOP_K * tm * n_slabs, LANES), F32), pltpu.SemaphoreType.DMA((2,))],
    )
    return pl.pallas_call(
        functools.partial(_combine_kernel, tm=tm, n_slabs=n_slabs, alpha=alpha),
        grid_spec=grid_spec,
        out_shape=jax.ShapeDtypeStruct((n, d), F32),
        compiler_params=pltpu.CompilerParams(dimension_semantics=("arbitrary",), vmem_limit_bytes=VMEM_LIMIT,
                                             disable_bounds_checks=True),
        name="combine",
    )(dest_rows, y_sorted, h1s, route, ln2_g, ln2_b)


def _dispatch_plan(route, counts, bm, n_slabs):
    n = route.shape[0]
    expert_id = route[:, 0:TOP_K].astype(jnp.int32)
    rank = route[:, 4:4 + TOP_K].astype(jnp.int32)
    padded = (counts + bm - 1) // bm * bm
    pad_end = jnp.cumsum(padded)
    pad_start = pad_end - padded
    dest = jnp.take(pad_start, expert_id) + rank
    n_blocks = (n * TOP_K + bm - 1) // bm + N_EXPERTS
    block_expert = jnp.minimum(
        jnp.searchsorted(pad_end, jnp.arange(n_blocks, dtype=jnp.int32) * bm, side='right'),
        N_EXPERTS - 1).astype(jnp.int32)
    n_used = (pad_end[-1:] // bm).astype(jnp.int32)
    dest_rows = (dest * n_slabs).reshape(-1).astype(jnp.int32)
    pad_lo = jnp.concatenate([pad_start + counts, pad_end[-1:]]).astype(jnp.int32)
    pad_hi = jnp.concatenate([pad_end, jnp.full((1,), n_blocks * bm, pad_end.dtype)]).astype(jnp.int32)
    return dest_rows, pad_lo, pad_hi, block_expert, n_used, n_blocks


def _pad_lanes(w, width=LANES):
    return jnp.pad(w, [(0, 0)] * (w.ndim - 1) + [(0, width - w.shape[-1])])


def kernel(x, meta_tokens, ln_emb_g, ln_emb_b, w_in, conv_qkv_w, a_log, dt_bias, dn_norm_g, conv_dw_w, conv_dw_b, cv_norm_g, cv_norm_b, w_out, ln1_g, ln1_b, w_group, b_group, w_router, b_router, w_exp_gate, w_exp_up, w_exp_down, ln2_g, ln2_b):
    depth = w_in.shape[0]
    assert depth == 1, "single-layer block"
    bsz, seq, d = x.shape
    alpha = (2.0 * depth) ** 0.25
    qkv_w = 3 * DN_WIDTH
    w_in0 = w_in[0]
    glu_off = 4 * DN_WIDTH + 2 * DN_HEADS
    row = lambda a: a.reshape(1, -1).astype(F32)
    p = {
        'ln_emb_g': row(ln_emb_g), 'ln_emb_b': row(ln_emb_b),
        'w_qkv': w_in0[:, :qkv_w].astype(BF16),
        'w_z': w_in0[:, qkv_w:4 * DN_WIDTH].astype(BF16),
        'w_ba': _pad_lanes(w_in0[:, 4 * DN_WIDTH:glu_off]).astype(BF16),
        'w_glu': w_in0[:, glu_off:].astype(BF16),
        'conv_w': conv_qkv_w[0].astype(F32),
        'neg_a': _pad_lanes(jnp.concatenate([jnp.zeros((DN_HEADS,), F32), -jnp.exp(a_log[0].astype(F32))])[None]),
        'dt_b': _pad_lanes(jnp.concatenate([jnp.zeros((DN_HEADS,), F32), dt_bias[0].astype(F32)])[None]),
        'dw_w': conv_dw_w[0].astype(F32), 'dw_b': row(conv_dw_b[0]),
        'cv_g': row(cv_norm_g[0]), 'cv_b': row(cv_norm_b[0]),
        'w_out': w_out[0].astype(BF16), 'ln1_g': row(ln1_g[0]), 'ln1_b': row(ln1_b[0]),
    }
    w_r = _pad_lanes(jnp.concatenate([w_group[0], w_router[0]], axis=1).astype(F32))
    p['w_r_hi'] = w_r.astype(BF16)
    p['w_r_lo'] = (w_r - p['w_r_hi'].astype(F32)).astype(BF16)
    p['b_r'] = _pad_lanes(jnp.concatenate([b_group[0], b_router[0]])[None].astype(F32))
    gain = row(dn_norm_g[0])

    conf_w = p['dw_w'].shape[1]
    zero_hq = jnp.zeros((QKV_HALO, qkv_w), F32)
    zero_hc = jnp.zeros((CONF_HALO, conf_w), F32)
    mq, mk, mv, mz, _, mbg, halo_q, halo_c = _mix_in(meta_tokens[None].astype(F32), p, zero_hq, zero_hc, N_META)
    front = lambda a: jnp.pad(a, [(0, 0), (CHUNK - N_META, 0), (0, 0)])
    s_zero = jnp.zeros((DN_HEADS, HEAD_DIM, HEAD_DIM), F32)
    _, s_meta = _delta(front(mq), front(mk), front(mv), front(mz), front(mbg), s_zero, gain, 1)

    q, k, v, z, c, bg, _, _ = _mix_in(x, p, halo_q[0], halo_c[0], TM_IN)
    o, _ = _delta(q, k, v, z, bg, s_meta[0], gain, DELTA_CHUNKS)

    n = bsz * seq
    h1s, route, cnt = _mix_out(x.reshape(n, d), o.reshape(n, DN_WIDTH), c.reshape(n, conf_w), p, TM_OUT, alpha)

    n_slabs = d // LANES
    counts = cnt[0, :N_EXPERTS].astype(jnp.int32)
    dest_rows, pad_lo, pad_hi, block_expert, n_used, n_blocks = _dispatch_plan(route, counts, BM_EXPERT, n_slabs)
    xs = _dispatch(dest_rows, pad_lo, pad_hi, h1s, n_blocks * BM_EXPERT, TD_DISPATCH, n_slabs)
    y_sorted = _experts(block_expert, n_used, xs, w_exp_gate[0].astype(BF16), w_exp_up[0].astype(BF16),
                        w_exp_down[0].astype(BF16), BM_EXPERT)
    out = _combine(dest_rows, y_sorted, h1s, route, row(ln2_g[0]), row(ln2_b[0]), TM_COMBINE, alpha)
    return out.reshape(bsz, seq, d)
```

```python
import functools

import jax
import jax.numpy as jnp
from jax import lax
from jax.experimental import pallas as pl
from jax.experimental.pallas import tpu as pltpu

F32 = jnp.float32
BF16 = jnp.bfloat16

NORM_EPS = 1e-5
N_META = 16
DN_HEADS = 4
HEAD_DIM = 128
DN_WIDTH = DN_HEADS * HEAD_DIM
CHUNK = 64
SHORT_CONV = 4
CONF_KERNEL = 31
N_GROUPS = 4
EXPERTS_PER_GROUP = 8
N_EXPERTS = N_GROUPS * EXPERTS_PER_GROUP
TOP_K = 2
LANES = 128
SUBLANES = 8
QKV_HALO = 8
CONF_HALO = 32
VMEM_LIMIT = 56 * 1024 * 1024

TM_IN = 256
DELTA_CHUNKS = 8
TM_OUT = 256
BM_EXPERT = 256
TM_COMBINE = 256
TD_DISPATCH = 256
RING = 3
DMA_UNROLL = 8


def _dot(a, b):
    return jnp.dot(a, b, preferred_element_type=F32)


def _split2(x):
    hi = x.astype(BF16)
    lo = (x - hi.astype(F32)).astype(BF16)
    return hi, lo


def _dot_hilo(a, b):
    ah, al = _split2(a)
    bh, bl = _split2(b)
    return _dot(ah, bh) + _dot(al, bh) + _dot(ah, bl)


def _dot_exact01(m01, x):
    x1 = x.astype(BF16)
    r1 = x - x1.astype(F32)
    x2 = r1.astype(BF16)
    x3 = (r1 - x2.astype(F32)).astype(BF16)
    return _dot(m01, x1) + _dot(m01, x2) + _dot(m01, x3)


def _sigmoid(x):
    return 1.0 / (1.0 + jnp.exp(-x))


def _silu(x):
    return x * _sigmoid(x)


def _layer_norm(x, g, b):
    mu = jnp.mean(x, axis=-1, keepdims=True)
    xc = x - mu
    var = jnp.mean(xc * xc, axis=-1, keepdims=True)
    return xc * lax.rsqrt(var + NORM_EPS) * g + b


def _store_slabs(ref, val):
    rows, d = val.shape
    n_slabs = d // LANES
    for s in range(n_slabs):
        ref[pl.ds(s, rows, stride=n_slabs), :] = val[:, s * LANES:(s + 1) * LANES]


def _load_slabs(ref, rows, n_slabs, base=0):
    return jnp.concatenate([ref[pl.ds(base + s, rows, stride=n_slabs), :] for s in range(n_slabs)], axis=1)


def _full_spec(shape):
    nd = len(shape)
    return pl.BlockSpec(shape, lambda *_: (0,) * nd)


def _mix_in_kernel(x_ref, lng_ref, lnb_ref, wqkv_ref, wz_ref, wglu_ref, wba_ref, cw_ref, nega_ref,
                   dtb_ref, dww_ref, dwb_ref, cvg_ref, cvb_ref, hq_in_ref, hc_in_ref,
                   q_ref, k_ref, v_ref, z_ref, c_ref, bg_ref, hq_out_ref, hc_out_ref,
                   qkv_ext, c_ext):
    tm = x_ref.shape[0]

    @pl.when(pl.program_id(1) == 0)
    def _():
        qkv_ext[0:QKV_HALO, :] = hq_in_ref[...]
        c_ext[0:CONF_HALO, :] = hc_in_ref[...]

    h = _layer_norm(x_ref[...], lng_ref[...], lnb_ref[...])
    hb = h.astype(BF16)

    qkv_ext[QKV_HALO:QKV_HALO + tm, :] = _dot(hb, wqkv_ref[...])
    acc = None
    for tap in range(SHORT_CONV):
        term = qkv_ext[pl.ds(QKV_HALO - (SHORT_CONV - 1) + tap, tm), :] * cw_ref[tap:tap + 1, :]
        acc = term if acc is None else acc + term
    qkv = _silu(acc)
    for hd in range(DN_HEADS):
        lo = hd * HEAD_DIM
        qh = qkv[:, lo:lo + HEAD_DIM]
        kh = qkv[:, DN_WIDTH + lo:DN_WIDTH + lo + HEAD_DIM]
        q_ref[:, lo:lo + HEAD_DIM] = qh * (lax.rsqrt(jnp.sum(qh * qh, axis=-1, keepdims=True) + 1e-6)
                                           * (HEAD_DIM ** -0.5))
        k_ref[:, lo:lo + HEAD_DIM] = kh * lax.rsqrt(jnp.sum(kh * kh, axis=-1, keepdims=True) + 1e-6)
    v_ref[...] = qkv[:, 2 * DN_WIDTH:]
    z_ref[...] = _dot(hb, wz_ref[...])

    ba = _dot(hb, wba_ref[...])
    lane = lax.broadcasted_iota(jnp.int32, ba.shape, 1)
    sp_in = ba + dtb_ref[...]
    softplus = jnp.maximum(sp_in, 0.0) + jnp.log(1.0 + jnp.exp(-jnp.abs(sp_in)))
    bg_ref[...] = jnp.where(lane < DN_HEADS, _sigmoid(ba), nega_ref[...] * softplus)

    glu = _dot(hb, wglu_ref[...])
    cw = glu.shape[1] // 2
    c_ext[CONF_HALO:CONF_HALO + tm, :] = glu[:, :cw] * _sigmoid(glu[:, cw:])
    acc = None
    for tap in range(CONF_KERNEL):
        term = c_ext[pl.ds(CONF_HALO - (CONF_KERNEL - 1) + tap, tm), :] * dww_ref[tap:tap + 1, :]
        acc = term if acc is None else acc + term
    conv = acc + dwb_ref[...]
    c_ref[...] = _silu(_layer_norm(conv, cvg_ref[...], cvb_ref[...]))

    q_tail = qkv_ext[tm:tm + QKV_HALO, :]
    c_tail = c_ext[tm:tm + CONF_HALO, :]
    qkv_ext[0:QKV_HALO, :] = q_tail
    c_ext[0:CONF_HALO, :] = c_tail
    hq_out_ref[...] = q_tail
    hc_out_ref[...] = c_tail


def _mix_in(x, p, halo_q, halo_c, tm):
    bsz, seq, d = x.shape
    assert seq % tm == 0
    qkv_w = 3 * DN_WIDTH
    conf_w = p['dw_w'].shape[1]

    def row(width):
        return pl.BlockSpec((None, tm, width), lambda b, t: (b, t, 0))

    def per_batch(rows, width):
        return pl.BlockSpec((None, rows, width), lambda b, t: (b, 0, 0))

    consts = [p['ln_emb_g'], p['ln_emb_b'], p['w_qkv'], p['w_z'], p['w_glu'], p['w_ba'], p['conv_w'],
              p['neg_a'], p['dt_b'], p['dw_w'], p['dw_b'], p['cv_g'], p['cv_b'], halo_q, halo_c]
    sds = jax.ShapeDtypeStruct
    out_shape = ([sds((bsz, seq, DN_WIDTH), F32)] * 4 + [sds((bsz, seq, conf_w), F32),
                 sds((bsz, seq, LANES), F32), sds((bsz, QKV_HALO, qkv_w), F32),
                 sds((bsz, CONF_HALO, conf_w), F32)])
    out_specs = ([row(DN_WIDTH)] * 4 + [row(conf_w), row(LANES), per_batch(QKV_HALO, qkv_w),
                 per_batch(CONF_HALO, conf_w)])
    return pl.pallas_call(
        _mix_in_kernel,
        grid=(bsz, seq // tm),
        in_specs=[row(d)] + [_full_spec(c.shape) for c in consts],
        out_specs=out_specs,
        out_shape=out_shape,
        scratch_shapes=[pltpu.VMEM((QKV_HALO + tm, qkv_w), F32), pltpu.VMEM((CONF_HALO + tm, conf_w), F32)],
        compiler_params=pltpu.CompilerParams(dimension_semantics=("parallel", "arbitrary"),
                                             vmem_limit_bytes=VMEM_LIMIT),
        name="mix_in",
    )(x, *consts)


def _bmm(a, b):
    return jnp.einsum('nij,njk->nik', a, b, preferred_element_type=F32)


def _delta_kernel(q_ref, k_ref, v_ref, z_ref, bg_ref, s0_ref, gain_ref, o_ref, sfin_ref, s_ref, *, chunks):
    j = pl.program_id(1)

    @pl.when(j == 0)
    def _():
        s_ref[...] = s0_ref[...]

    ii = lax.broadcasted_iota(jnp.int32, (CHUNK, CHUNK), 0)
    jj = lax.broadcasted_iota(jnp.int32, (CHUNK, CHUNK), 1)
    causal = ii >= jj
    strict = ii > jj
    eye = (ii == jj).astype(F32)
    gain = gain_ref[...]

    bg3 = bg_ref[...].reshape(chunks, CHUNK, LANES)
    tril_b = jnp.broadcast_to(causal.astype(BF16), (chunks, CHUNK, CHUNK))
    p1 = bg3.astype(BF16)
    r1 = bg3 - p1.astype(F32)
    p2 = r1.astype(BF16)
    p3 = (r1 - p2.astype(F32)).astype(BF16)
    gc3 = _bmm(tril_b, p1) + _bmm(tril_b, p2) + _bmm(tril_b, p3)

    def heads(ref):
        return jnp.concatenate([ref[:, hd * HEAD_DIM:(hd + 1) * HEAD_DIM].reshape(chunks, CHUNK, HEAD_DIM)
                                for hd in range(DN_HEADS)], axis=0)
    q = heads(q_ref)
    k = heads(k_ref)
    v = heads(v_ref)
    bet = jnp.concatenate([bg3[:, :, hd:hd + 1] for hd in range(DN_HEADS)], axis=0)
    gc = jnp.concatenate([gc3[:, :, DN_HEADS + hd:DN_HEADS + hd + 1] for hd in range(DN_HEADS)], axis=0)
    gc_t = [gc3[c].T for c in range(chunks)]
    decay = jnp.stack([
        jnp.exp(jnp.where(causal, gc3[c][:, DN_HEADS + hd:DN_HEADS + hd + 1]
                          - gc_t[c][DN_HEADS + hd:DN_HEADS + hd + 1, :], -jnp.inf))
        for hd in range(DN_HEADS) for c in range(chunks)], axis=0)

    kb = k * bet
    g_all = jnp.einsum('nid,njd->nij', jnp.concatenate([kb, q], axis=1).astype(BF16), k.astype(BF16),
                       preferred_element_type=F32)
    a_low = jnp.where(strict, g_all[:, :CHUNK] * decay, 0.0)
    attn = (g_all[:, CHUNK:] * decay).astype(BF16)

    l_mat = eye + a_low
    l_bf = l_mat.astype(BF16)
    t_mat = eye - a_low
    for _ in range(4):
        res = eye - _bmm(l_bf, t_mat.astype(BF16))
        t_mat = t_mat + _bmm(t_mat.astype(BF16), res.astype(BF16))
    l_lo = (l_mat - l_bf.astype(F32)).astype(BF16)
    t_hi = t_mat.astype(BF16)
    t_lo = (t_mat - t_hi.astype(F32)).astype(BF16)
    res = eye - (_bmm(l_bf, t_hi) + _bmm(l_lo, t_hi) + _bmm(l_bf, t_lo))
    t_mat = t_mat + _bmm(t_hi, res.astype(BF16))

    eg = jnp.exp(gc)
    uw = _bmm(t_mat.astype(BF16), jnp.concatenate([v * bet, kb * eg], axis=2).astype(BF16))
    u = uw[:, :, :HEAD_DIM]
    wq_lhs = jnp.concatenate([uw[:, :, HEAD_DIM:], q * eg], axis=1).astype(BF16)
    g_last = gc[:, CHUNK - 1:CHUNK, :]
    k_dec = (k * jnp.exp(g_last - gc)).astype(BF16)
    eg_last = jnp.exp(g_last)

    state = [s_ref[hd] for hd in range(DN_HEADS)]
    for c in range(chunks):
        idx = [hd * chunks + c for hd in range(DN_HEADS)]
        wq = [_dot(wq_lhs[n], state[hd].astype(BF16)) for hd, n in enumerate(idx)]
        v_new = [(u[n] - wq[hd][:CHUNK]).astype(BF16) for hd, n in enumerate(idx)]
        o = [wq[hd][CHUNK:] + _dot(attn[n], v_new[hd]) for hd, n in enumerate(idx)]
        state = [state[hd] * eg_last[n] + lax.dot_general(k_dec[n], v_new[hd], (((0,), (0,)), ((), ())),
                                                          preferred_element_type=F32)
                 for hd, n in enumerate(idx)]
        for hd in range(DN_HEADS):
            cols = slice(hd * HEAD_DIM, (hd + 1) * HEAD_DIM)
            rows = slice(c * CHUNK, (c + 1) * CHUNK)
            r = o[hd] * lax.rsqrt(jnp.mean(o[hd] * o[hd], axis=-1, keepdims=True) + 1e-6)
            o_ref[rows, cols] = r * gain * _silu(z_ref[rows, cols])
    for hd in range(DN_HEADS):
        s_ref[hd] = state[hd]

    @pl.when(j == pl.num_programs(1) - 1)
    def _():
        sfin_ref[...] = s_ref[...]


def _delta(q, k, v, z, bg, s0, gain, chunks):
    bsz, seq, _ = q.shape
    rows = chunks * CHUNK
    assert seq % rows == 0

    def row(width):
        return pl.BlockSpec((None, rows, width), lambda b, j: (b, j, 0))

    state_shape = (DN_HEADS, HEAD_DIM, HEAD_DIM)
    return pl.pallas_call(
        functools.partial(_delta_kernel, chunks=chunks),
        grid=(bsz, seq // rows),
        in_specs=[row(DN_WIDTH)] * 4 + [row(LANES), _full_spec(state_shape), _full_spec(gain.shape)],
        out_specs=[row(DN_WIDTH), pl.BlockSpec((None,) + state_shape, lambda b, j: (b, 0, 0, 0))],
        out_shape=[jax.ShapeDtypeStruct((bsz, seq, DN_WIDTH), F32),
                   jax.ShapeDtypeStruct((bsz,) + state_shape, F32)],
        scratch_shapes=[pltpu.VMEM(state_shape, F32)],
        compiler_params=pltpu.CompilerParams(dimension_semantics=("parallel", "arbitrary"),
                                             vmem_limit_bytes=VMEM_LIMIT),
        name="delta",
    )(q, k, v, z, bg, s0, gain)


def _mix_out_kernel(x_ref, o_ref, c_ref, lng_ref, lnb_ref, wo_ref, g1_ref, b1_ref, wrh_ref, wrl_ref, br_ref,
                    h1s_ref, route_ref, route_t_ref, cnt_out_ref, cnt_ref, *, alpha):
    @pl.when(pl.program_id(0) == 0)
    def _():
        cnt_ref[...] = jnp.zeros_like(cnt_ref)

    h = _layer_norm(x_ref[...], lng_ref[...], lnb_ref[...])
    dn = o_ref.shape[1]
    mix = _dot(o_ref[...].astype(BF16), wo_ref[0:dn, :]) + _dot(c_ref[...].astype(BF16), wo_ref[dn:, :])
    h1 = _layer_norm(alpha * h + mix, g1_ref[...], b1_ref[...])
    _store_slabs(h1s_ref, h1)

    hh, hl = _split2(h1)
    logits = _dot(hh, wrh_ref[...]) + _dot(hl, wrh_ref[...]) + _dot(hh, wrl_ref[...]) + br_ref[...]
    lane = lax.broadcasted_iota(jnp.int32, logits.shape, 1).astype(F32)
    big = float(LANES)
    neg = -jnp.inf

    def first_argmax(vals):
        top = jnp.max(vals, axis=-1, keepdims=True)
        return top, jnp.min(jnp.where(vals == top, lane, big), axis=-1, keepdims=True)

    grp = jnp.where(lane < N_GROUPS, logits, neg)
    g_top, g_sel = first_argmax(grp)
    p_group = 1.0 / jnp.sum(jnp.exp(grp - g_top), axis=-1, keepdims=True)
    lo = N_GROUPS + EXPERTS_PER_GROUP * g_sel
    in_grp = jnp.where((lane >= lo) & (lane < lo + EXPERTS_PER_GROUP), logits, neg)
    m1, i1 = first_argmax(in_grp)
    m2, i2 = first_argmax(jnp.where(lane == i1, neg, in_grp))
    s = jnp.exp(m2 - m1)
    w1 = p_group / (1.0 + s)
    w2 = p_group * s / (1.0 + s)
    e1 = i1 - N_GROUPS
    e2 = i2 - N_GROUPS

    tm = logits.shape[0]
    oh1 = (lane == e1).astype(F32)
    oh2 = (lane == e2).astype(F32)
    both = oh1 + oh2
    ti = lax.broadcasted_iota(jnp.int32, (tm, tm), 0)
    tj = lax.broadcasted_iota(jnp.int32, (tm, tm), 1)
    base = _dot((ti > tj).astype(BF16), both.astype(BF16)) + cnt_ref[...]
    r1 = jnp.sum(oh1 * base, axis=-1, keepdims=True)
    r2 = jnp.sum(oh2 * base, axis=-1, keepdims=True)
    cnt_ref[...] = cnt_ref[...] + jnp.sum(both, axis=0, keepdims=True)
    cnt_out_ref[...] = jnp.broadcast_to(cnt_ref[...], cnt_out_ref.shape)

    vals = (e1, e2, w1, w2, r1, r2)
    route = jnp.zeros_like(logits)
    for idx, val in enumerate(vals):
        route = jnp.where(lane == idx, val, route)
    route_ref[...] = route
    route_t_ref[...] = route.T[0:SUBLANES, :]


def _mix_out(x2d, o2d, c2d, p, tm, alpha):
    n, d = x2d.shape
    assert n % tm == 0
    slabs = d // LANES

    def row(width):
        return pl.BlockSpec((tm, width), lambda i: (i, 0))

    consts = [p['ln_emb_g'], p['ln_emb_b'], p['w_out'], p['ln1_g'], p['ln1_b'], p['w_r_hi'], p['w_r_lo'], p['b_r']]
    return pl.pallas_call(
        functools.partial(_mix_out_kernel, alpha=alpha),
        grid=(n // tm,),
        in_specs=[row(d), row(o2d.shape[1]), row(c2d.shape[1])] + [_full_spec(c.shape) for c in consts],
        out_specs=[pl.BlockSpec((tm * slabs, LANES), lambda i: (i, 0)), row(LANES),
                   pl.BlockSpec((SUBLANES, tm), lambda i: (0, i)), _full_spec((SUBLANES, LANES))],
        out_shape=[jax.ShapeDtypeStruct((n * slabs, LANES), F32), jax.ShapeDtypeStruct((n, LANES), F32),
                   jax.ShapeDtypeStruct((SUBLANES, n), F32), jax.ShapeDtypeStruct((SUBLANES, LANES), F32)],
        scratch_shapes=[pltpu.VMEM((1, LANES), F32)],
        compiler_params=pltpu.CompilerParams(dimension_semantics=("arbitrary",), vmem_limit_bytes=VMEM_LIMIT),
        name="mix_out",
    )(x2d, o2d, c2d, *consts)


def _dispatch_kernel(dest_ref, pad_lo_ref, pad_hi_ref, h1s_hbm, xs_hbm, ring, zslab, fsem, ssem, zsem, *,
                     td, n_slabs, n_tokens):
    i = pl.program_id(0)
    nb = pl.num_programs(0)
    slot = i % RING
    tile_rows = td * n_slabs

    def fetch(step):
        start = pl.multiple_of(step * tile_rows, tile_rows)
        return pltpu.make_async_copy(h1s_hbm.at[pl.ds(start, tile_rows), :], ring.at[step % RING],
                                     fsem.at[step % RING])

    def wait_scatter(step):
        for _ in range(TOP_K):
            pltpu.make_async_copy(ring.at[step % RING], xs_hbm.at[pl.ds(0, tile_rows), :],
                                  ssem.at[step % RING]).wait()

    def pad_copy(row):
        return pltpu.make_async_copy(zslab, xs_hbm.at[pl.ds(pl.multiple_of(row * n_slabs, n_slabs), n_slabs), :],
                                     zsem.at[0])

    def for_each_pad_row(fn):
        def per_expert(e, carry):
            def per_row(row, c2):
                fn(row)
                return c2
            return lax.fori_loop(pad_lo_ref[e], pad_hi_ref[e], per_row, carry)
        lax.fori_loop(0, pad_lo_ref.shape[0], per_expert, 0)

    @pl.when(i == 0)
    def _():
        fetch(0).start()
        zslab[...] = jnp.zeros_like(zslab)
        for_each_pad_row(lambda row: pad_copy(row).start())

    @pl.when((i == 0) & (nb > 1))
    def _():
        fetch(1).start()

    fetch(i).wait()

    def issue_body(r, carry):
        src = ring.at[slot, pl.ds(pl.multiple_of(r * n_slabs, n_slabs), n_slabs), :]
        for k in range(TOP_K):
            dst_row = pl.multiple_of(dest_ref[k * n_tokens + i * td + r], n_slabs)
            pltpu.make_async_copy(src, xs_hbm.at[pl.ds(dst_row, n_slabs), :], ssem.at[slot]).start()
        return carry
    lax.fori_loop(0, td, issue_body, 0, unroll=DMA_UNROLL)

    @pl.when(i > 0)
    def _():
        wait_scatter(i - 1)

    @pl.when(i + 2 < nb)
    def _():
        fetch(i + 2).start()

    @pl.when(i == nb - 1)
    def _():
        wait_scatter(i)
        for_each_pad_row(lambda row: pad_copy(row).wait())


def _dispatch(dest_rows, pad_lo, pad_hi, h1s, cap_rows, td, n_slabs):
    n = h1s.shape[0] // n_slabs
    assert n % td == 0
    grid_spec = pltpu.PrefetchScalarGridSpec(
        num_scalar_prefetch=3,
        grid=(n // td,),
        in_specs=[pl.BlockSpec(memory_space=pl.ANY)],
        out_specs=pl.BlockSpec(memory_space=pl.ANY),
        scratch_shapes=[pltpu.VMEM((RING, td * n_slabs, LANES), F32), pltpu.VMEM((n_slabs, LANES), F32),
                        pltpu.SemaphoreType.DMA((RING,)), pltpu.SemaphoreType.DMA((RING,)),
                        pltpu.SemaphoreType.DMA((1,))],
    )
    return pl.pallas_call(
        functools.partial(_dispatch_kernel, td=td, n_slabs=n_slabs, n_tokens=n),
        grid_spec=grid_spec,
        out_shape=jax.ShapeDtypeStruct((cap_rows * n_slabs, LANES), F32),
        compiler_params=pltpu.CompilerParams(dimension_semantics=("arbitrary",), disable_bounds_checks=True),
        name="dispatch",
    )(dest_rows, pad_lo, pad_hi, h1s)


def _expert_kernel(be_ref, nu_ref, xs_ref, wg_ref, wu_ref, wd_ref, y_ref, *, bm, n_slabs):
    del be_ref
    used = pl.program_id(0) < nu_ref[0]

    @pl.when(used)
    def _():
        xb = _load_slabs(xs_ref, bm, n_slabs).astype(BF16)
        hid = _silu(_dot(xb, wg_ref[...])) * _dot(xb, wu_ref[...])
        _store_slabs(y_ref, _dot(hid.astype(BF16), wd_ref[...]))

    @pl.when(jnp.logical_not(used))
    def _():
        y_ref[...] = jnp.zeros_like(y_ref)


def _experts(block_expert, n_used, xs, w_gate, w_up, w_down, bm):
    n_blocks = block_expert.shape[0]
    d = w_gate.shape[1]
    ff = w_gate.shape[2]
    n_slabs = d // LANES

    def blk(i, be, nu):
        return jnp.minimum(i, nu[0] - 1)

    grid_spec = pltpu.PrefetchScalarGridSpec(
        num_scalar_prefetch=2,
        grid=(n_blocks,),
        in_specs=[pl.BlockSpec((bm * n_slabs, LANES), lambda i, be, nu: (blk(i, be, nu), 0)),
                  pl.BlockSpec((None, d, ff), lambda i, be, nu: (be[blk(i, be, nu)], 0, 0)),
                  pl.BlockSpec((None, d, ff), lambda i, be, nu: (be[blk(i, be, nu)], 0, 0)),
                  pl.BlockSpec((None, ff, d), lambda i, be, nu: (be[blk(i, be, nu)], 0, 0))],
        out_specs=pl.BlockSpec((bm * n_slabs, LANES), lambda i, be, nu: (i, 0)),
    )
    return pl.pallas_call(
        functools.partial(_expert_kernel, bm=bm, n_slabs=n_slabs),
        grid_spec=grid_spec,
        out_shape=jax.ShapeDtypeStruct(xs.shape, F32),
        compiler_params=pltpu.CompilerParams(dimension_semantics=("arbitrary",), vmem_limit_bytes=VMEM_LIMIT),
        name="experts",
    )(block_expert, n_used, xs, w_gate, w_up, w_down)


def _combine_kernel(dest_ref, y_hbm, h1s_ref, route_ref, g2_ref, b2_ref, out_ref, ybuf, sem, *, tm, n_slabs, alpha):
    i = pl.program_id(0)
    nb = pl.num_programs(0)
    part = tm * n_slabs

    def issue(blk, slot):
        def body(r, carry):
            for k in range(TOP_K):
                src_row = pl.multiple_of(dest_ref[k * (nb * tm) + blk * tm + r], n_slabs)
                dst_row = pl.multiple_of((slot * TOP_K + k) * part + r * n_slabs, n_slabs)
                pltpu.make_async_copy(y_hbm.at[pl.ds(src_row, n_slabs), :], ybuf.at[pl.ds(dst_row, n_slabs), :],
                                      sem.at[slot]).start()
            return carry
        lax.fori_loop(0, tm, body, 0, unroll=DMA_UNROLL)

    @pl.when(i == 0)
    def _():
        issue(0, 0)

    @pl.when(i + 1 < nb)
    def _():
        issue(i + 1, (i + 1) % 2)

    slot = i % 2
    base = pl.multiple_of(slot * (TOP_K * part), TOP_K * part)
    pltpu.make_async_copy(y_hbm.at[pl.ds(0, TOP_K * part), :], ybuf.at[pl.ds(base, TOP_K * part), :],
                          sem.at[slot]).wait()

    route = route_ref[...]
    ffn = (_load_slabs(ybuf, tm, n_slabs, base) * route[:, 2:3]
           + _load_slabs(ybuf, tm, n_slabs, base + part) * route[:, 3:4])
    h1 = _load_slabs(h1s_ref, tm, n_slabs)
    out_ref[...] = _layer_norm(alpha * h1 + ffn, g2_ref[...], b2_ref[...])


def _combine(dest_rows, y_sorted, h1s, route, ln2_g, ln2_b, tm, alpha):
    d = ln2_g.shape[1]
    n_slabs = d // LANES
    n = h1s.shape[0] // n_slabs
    assert n % tm == 0
    grid_spec = pltpu.PrefetchScalarGridSpec(
        num_scalar_prefetch=1,
        grid=(n // tm,),
        in_specs=[pl.BlockSpec(memory_space=pl.ANY),
                  pl.BlockSpec((tm * n_slabs, LANES), lambda i, dest: (i, 0)),
                  pl.BlockSpec((tm, LANES), lambda i, dest: (i, 0)),
                  pl.BlockSpec((1, d), lambda i, dest: (0, 0)),
                  pl.BlockSpec((1, d), lambda i, dest: (0, 0))],
        out_specs=pl.BlockSpec((tm, d), lambda i, dest: (i, 0)),
        scratch_shapes=[pltpu.VMEM((2 * TOP_K * tm * n_slabs, LANES), F32), pltpu.SemaphoreType.DMA((2,))],
    )
    return pl.pallas_call(
        functools.partial(_combine_kernel, tm=tm, n_slabs=n_slabs, alpha=alpha),
        grid_spec=grid_spec,
        out_shape=jax.ShapeDtypeStruct((n, d), F32),
        compiler_params=pltpu.CompilerParams(dimension_semantics=("arbitrary",), vmem_limit_bytes=VMEM_LIMIT,
                                             disable_bounds_checks=True),
        name="combine",
    )(dest_rows, y_sorted, h1s, route, ln2_g, ln2_b)


def _dispatch_plan(route_t, counts, bm, n_slabs):
    n = route_t.shape[1]
    expert_id = route_t[0:TOP_K].astype(jnp.int32)
    rank = route_t[4:4 + TOP_K].astype(jnp.int32)
    padded = (counts + bm - 1) // bm * bm
    pad_end = jnp.cumsum(padded)
    pad_start = pad_end - padded
    dest = jnp.take(pad_start, expert_id) + rank
    n_blocks = (n * TOP_K + bm - 1) // bm + N_EXPERTS
    block_start = jnp.arange(n_blocks, dtype=jnp.int32) * bm
    block_expert = jnp.minimum(jnp.sum((block_start[:, None] >= pad_end[None, :]).astype(jnp.int32), axis=1),
                               N_EXPERTS - 1).astype(jnp.int32)
    n_used = (pad_end[-1:] // bm).astype(jnp.int32)
    dest_rows = (dest * n_slabs).reshape(-1).astype(jnp.int32)
    pad_lo = jnp.concatenate([pad_start + counts, pad_end[-1:]]).astype(jnp.int32)
    pad_hi = jnp.concatenate([pad_end, jnp.full((1,), n_blocks * bm, pad_end.dtype)]).astype(jnp.int32)
    return dest_rows, pad_lo, pad_hi, block_expert, n_used, n_blocks


def _pad_lanes(w, width=LANES):
    return jnp.pad(w, [(0, 0)] * (w.ndim - 1) + [(0, width - w.shape[-1])])


def kernel(x, meta_tokens, ln_emb_g, ln_emb_b, w_in, conv_qkv_w, a_log, dt_bias, dn_norm_g, conv_dw_w, conv_dw_b, cv_norm_g, cv_norm_b, w_out, ln1_g, ln1_b, w_group, b_group, w_router, b_router, w_exp_gate, w_exp_up, w_exp_down, ln2_g, ln2_b):
    depth = w_in.shape[0]
    assert depth == 1, "single-layer block"
    bsz, seq, d = x.shape
    alpha = (2.0 * depth) ** 0.25
    qkv_w = 3 * DN_WIDTH
    w_in0 = w_in[0]
    glu_off = 4 * DN_WIDTH + 2 * DN_HEADS
    row = lambda a: a.reshape(1, -1).astype(F32)
    p = {
        'ln_emb_g': row(ln_emb_g), 'ln_emb_b': row(ln_emb_b),
        'w_qkv': w_in0[:, :qkv_w].astype(BF16),
        'w_z': w_in0[:, qkv_w:4 * DN_WIDTH].astype(BF16),
        'w_ba': _pad_lanes(w_in0[:, 4 * DN_WIDTH:glu_off]).astype(BF16),
        'w_glu': w_in0[:, glu_off:].astype(BF16),
        'conv_w': conv_qkv_w[0].astype(F32),
        'neg_a': _pad_lanes(jnp.concatenate([jnp.zeros((DN_HEADS,), F32), -jnp.exp(a_log[0].astype(F32))])[None]),
        'dt_b': _pad_lanes(jnp.concatenate([jnp.zeros((DN_HEADS,), F32), dt_bias[0].astype(F32)])[None]),
        'dw_w': conv_dw_w[0].astype(F32), 'dw_b': row(conv_dw_b[0]),
        'cv_g': row(cv_norm_g[0]), 'cv_b': row(cv_norm_b[0]),
        'w_out': w_out[0].astype(BF16), 'ln1_g': row(ln1_g[0]), 'ln1_b': row(ln1_b[0]),
    }
    w_r = _pad_lanes(jnp.concatenate([w_group[0], w_router[0]], axis=1).astype(F32))
    p['w_r_hi'] = w_r.astype(BF16)
    p['w_r_lo'] = (w_r - p['w_r_hi'].astype(F32)).astype(BF16)
    p['b_r'] = _pad_lanes(jnp.concatenate([b_group[0], b_router[0]])[None].astype(F32))
    gain = row(dn_norm_g[0])

    conf_w = p['dw_w'].shape[1]
    zero_hq = jnp.zeros((QKV_HALO, qkv_w), F32)
    zero_hc = jnp.zeros((CONF_HALO, conf_w), F32)
    mq, mk, mv, mz, _, mbg, halo_q, halo_c = _mix_in(meta_tokens[None].astype(F32), p, zero_hq, zero_hc, N_META)
    front = lambda a: jnp.pad(a, [(0, 0), (CHUNK - N_META, 0), (0, 0)])
    s_zero = jnp.zeros((DN_HEADS, HEAD_DIM, HEAD_DIM), F32)
    _, s_meta = _delta(front(mq), front(mk), front(mv), front(mz), front(mbg), s_zero, gain, 1)

    q, k, v, z, c, bg, _, _ = _mix_in(x, p, halo_q[0], halo_c[0], TM_IN)
    o, _ = _delta(q, k, v, z, bg, s_meta[0], gain, DELTA_CHUNKS)

    n = bsz * seq
    h1s, route, route_t, cnt = _mix_out(x.reshape(n, d), o.reshape(n, DN_WIDTH), c.reshape(n, conf_w), p,
                                        TM_OUT, alpha)

    n_slabs = d // LANES
    counts = cnt[0, :N_EXPERTS].astype(jnp.int32)
    dest_rows, pad_lo, pad_hi, block_expert, n_used, n_blocks = _dispatch_plan(route_t, counts, BM_EXPERT, n_slabs)
    xs = _dispatch(dest_rows, pad_lo, pad_hi, h1s, n_blocks * BM_EXPERT, TD_DISPATCH, n_slabs)
    y_sorted = _experts(block_expert, n_used, xs, w_exp_gate[0].astype(BF16), w_exp_up[0].astype(BF16),
                        w_exp_down[0].astype(BF16), BM_EXPERT)
    out = _combine(dest_rows, y_sorted, h1s, route, row(ln2_g[0]), row(ln2_b[0]), TM_COMBINE, alpha)
    return out.reshape(bsz, seq, d)
```

```python
import functools

import jax
import jax.numpy as jnp
from jax import lax
from jax.experimental import pallas as pl
from jax.experimental.pallas import tpu as pltpu

F32 = jnp.float32
BF16 = jnp.bfloat16

NORM_EPS = 1e-5
N_META = 16
DN_HEADS = 4
HEAD_DIM = 128
DN_WIDTH = DN_HEADS * HEAD_DIM
CHUNK = 64
SHORT_CONV = 4
CONF_KERNEL = 31
N_GROUPS = 4
EXPERTS_PER_GROUP = 8
N_EXPERTS = N_GROUPS * EXPERTS_PER_GROUP
TOP_K = 2
LANES = 128
SUBLANES = 8
QKV_HALO = 8
CONF_HALO = 32
VMEM_LIMIT = 56 * 1024 * 1024

TM_IN = 256
DELTA_CHUNKS = 8
TM_OUT = 256
BM_EXPERT = 256
TM_COMBINE = 256
TD_DISPATCH = 256
RING = 3
DMA_UNROLL = 8


def _dot(a, b):
    return jnp.dot(a, b, preferred_element_type=F32)


def _split2(x):
    hi = x.astype(BF16)
    lo = (x - hi.astype(F32)).astype(BF16)
    return hi, lo


def _dot_hilo(a, b):
    ah, al = _split2(a)
    bh, bl = _split2(b)
    return _dot(ah, bh) + _dot(al, bh) + _dot(ah, bl)


def _dot_exact01(m01, x):
    x1 = x.astype(BF16)
    r1 = x - x1.astype(F32)
    x2 = r1.astype(BF16)
    x3 = (r1 - x2.astype(F32)).astype(BF16)
    return _dot(m01, x1) + _dot(m01, x2) + _dot(m01, x3)


def _sigmoid(x):
    return 1.0 / (1.0 + jnp.exp(-x))


def _silu(x):
    return x * _sigmoid(x)


def _layer_norm(x, g, b):
    mu = jnp.mean(x, axis=-1, keepdims=True)
    xc = x - mu
    var = jnp.mean(xc * xc, axis=-1, keepdims=True)
    return xc * lax.rsqrt(var + NORM_EPS) * g + b


def _store_slabs(ref, val):
    rows, d = val.shape
    n_slabs = d // LANES
    for s in range(n_slabs):
        ref[pl.ds(s, rows, stride=n_slabs), :] = val[:, s * LANES:(s + 1) * LANES]


def _load_slabs(ref, rows, n_slabs, base=0):
    return jnp.concatenate([ref[pl.ds(base + s, rows, stride=n_slabs), :] for s in range(n_slabs)], axis=1)


def _full_spec(shape):
    nd = len(shape)
    return pl.BlockSpec(shape, lambda *_: (0,) * nd)


def _causal_depthwise(ext_ref, w_ref, n_taps, halo, tm):
    ext = ext_ref[...]
    rows = ext.shape[0]
    first = halo - (n_taps - 1)
    acc = None
    for phase in range(SUBLANES):
        taps = [k for k in range(n_taps) if (first + k) % SUBLANES == phase]
        if not taps:
            continue
        src = ext if phase == 0 else pltpu.roll(ext, rows - phase, axis=0)
        for k in taps:
            base = first + k - phase
            term = src[base:base + tm, :] * w_ref[k:k + 1, :]
            acc = term if acc is None else acc + term
    return acc
def _mix_in_kernel(x_ref, lng_ref, lnb_ref, wqkv_ref, wz_ref, wglu_ref, wba_ref, cw_ref, nega_ref,
                   dtb_ref, dww_ref, dwb_ref, cvg_ref, cvb_ref, hq_in_ref, hc_in_ref,
                   q_ref, k_ref, v_ref, z_ref, c_ref, bg_ref, hq_out_ref, hc_out_ref,
                   qkv_ext, c_ext):
    tm = x_ref.shape[0]

    @pl.when(pl.program_id(1) == 0)
    def _():
        qkv_ext[0:QKV_HALO, :] = hq_in_ref[...]
        c_ext[0:CONF_HALO, :] = hc_in_ref[...]

    h = _layer_norm(x_ref[...], lng_ref[...], lnb_ref[...])
    hb = h.astype(BF16)

    qkv_ext[QKV_HALO:QKV_HALO + tm, :] = _dot(hb, wqkv_ref[...])
    qkv = _silu(_causal_depthwise(qkv_ext, cw_ref, SHORT_CONV, QKV_HALO, tm))
    for hd in range(DN_HEADS):
        lo = hd * HEAD_DIM
        qh = qkv[:, lo:lo + HEAD_DIM]
        kh = qkv[:, DN_WIDTH + lo:DN_WIDTH + lo + HEAD_DIM]
        q_ref[:, lo:lo + HEAD_DIM] = qh * (lax.rsqrt(jnp.sum(qh * qh, axis=-1, keepdims=True) + 1e-6)
                                           * (HEAD_DIM ** -0.5))
        k_ref[:, lo:lo + HEAD_DIM] = kh * lax.rsqrt(jnp.sum(kh * kh, axis=-1, keepdims=True) + 1e-6)
    v_ref[...] = qkv[:, 2 * DN_WIDTH:]
    z_ref[...] = _dot(hb, wz_ref[...])

    ba = _dot(hb, wba_ref[...])
    lane = lax.broadcasted_iota(jnp.int32, ba.shape, 1)
    sp_in = ba + dtb_ref[...]
    softplus = jnp.maximum(sp_in, 0.0) + jnp.log(1.0 + jnp.exp(-jnp.abs(sp_in)))
    bg_ref[...] = jnp.where(lane < DN_HEADS, _sigmoid(ba), nega_ref[...] * softplus)

    glu = _dot(hb, wglu_ref[...])
    cw = glu.shape[1] // 2
    c_ext[CONF_HALO:CONF_HALO + tm, :] = glu[:, :cw] * _sigmoid(glu[:, cw:])
    conv = _causal_depthwise(c_ext, dww_ref, CONF_KERNEL, CONF_HALO, tm) + dwb_ref[...]
    c_ref[...] = _silu(_layer_norm(conv, cvg_ref[...], cvb_ref[...]))

    q_tail = qkv_ext[tm:tm + QKV_HALO, :]
    c_tail = c_ext[tm:tm + CONF_HALO, :]
    qkv_ext[0:QKV_HALO, :] = q_tail
    c_ext[0:CONF_HALO, :] = c_tail
    hq_out_ref[...] = q_tail
    hc_out_ref[...] = c_tail


def _mix_in(x, p, halo_q, halo_c, tm):
    bsz, seq, d = x.shape
    assert seq % tm == 0
    qkv_w = 3 * DN_WIDTH
    conf_w = p['dw_w'].shape[1]

    def row(width):
        return pl.BlockSpec((None, tm, width), lambda b, t: (b, t, 0))

    def per_batch(rows, width):
        return pl.BlockSpec((None, rows, width), lambda b, t: (b, 0, 0))

    consts = [p['ln_emb_g'], p['ln_emb_b'], p['w_qkv'], p['w_z'], p['w_glu'], p['w_ba'], p['conv_w'],
              p['neg_a'], p['dt_b'], p['dw_w'], p['dw_b'], p['cv_g'], p['cv_b'], halo_q, halo_c]
    sds = jax.ShapeDtypeStruct
    out_shape = ([sds((bsz, seq, DN_WIDTH), F32)] * 4 + [sds((bsz, seq, conf_w), F32),
                 sds((bsz, seq, LANES), F32), sds((bsz, QKV_HALO, qkv_w), F32),
                 sds((bsz, CONF_HALO, conf_w), F32)])
    out_specs = ([row(DN_WIDTH)] * 4 + [row(conf_w), row(LANES), per_batch(QKV_HALO, qkv_w),
                 per_batch(CONF_HALO, conf_w)])
    return pl.pallas_call(
        _mix_in_kernel,
        grid=(bsz, seq // tm),
        in_specs=[row(d)] + [_full_spec(c.shape) for c in consts],
        out_specs=out_specs,
        out_shape=out_shape,
        scratch_shapes=[pltpu.VMEM((QKV_HALO + tm, qkv_w), F32), pltpu.VMEM((CONF_HALO + tm, conf_w), F32)],
        compiler_params=pltpu.CompilerParams(dimension_semantics=("parallel", "arbitrary"),
                                             vmem_limit_bytes=VMEM_LIMIT),
        name="mix_in",
    )(x, *consts)


def _bmm(a, b):
    return jnp.einsum('nij,njk->nik', a, b, preferred_element_type=F32)


def _delta_kernel(q_ref, k_ref, v_ref, z_ref, bg_ref, s0_ref, gain_ref, o_ref, sfin_ref, s_ref, *, chunks):
    j = pl.program_id(1)

    @pl.when(j == 0)
    def _():
        s_ref[...] = s0_ref[...]

    ii = lax.broadcasted_iota(jnp.int32, (CHUNK, CHUNK), 0)
    jj = lax.broadcasted_iota(jnp.int32, (CHUNK, CHUNK), 1)
    causal = ii >= jj
    strict = ii > jj
    eye = (ii == jj).astype(F32)
    gain = gain_ref[...]

    bg3 = bg_ref[...].reshape(chunks, CHUNK, LANES)
    tril_b = jnp.broadcast_to(causal.astype(BF16), (chunks, CHUNK, CHUNK))
    p1 = bg3.astype(BF16)
    r1 = bg3 - p1.astype(F32)
    p2 = r1.astype(BF16)
    p3 = (r1 - p2.astype(F32)).astype(BF16)
    gc3 = _bmm(tril_b, p1) + _bmm(tril_b, p2) + _bmm(tril_b, p3)

    def heads(ref):
        return jnp.concatenate([ref[:, hd * HEAD_DIM:(hd + 1) * HEAD_DIM].reshape(chunks, CHUNK, HEAD_DIM)
                                for hd in range(DN_HEADS)], axis=0)
    q = heads(q_ref)
    k = heads(k_ref)
    v = heads(v_ref)
    bet = jnp.concatenate([bg3[:, :, hd:hd + 1] for hd in range(DN_HEADS)], axis=0)
    gc = jnp.concatenate([gc3[:, :, DN_HEADS + hd:DN_HEADS + hd + 1] for hd in range(DN_HEADS)], axis=0)
    gc_t = [gc3[c].T for c in range(chunks)]
    decay = jnp.stack([
        jnp.exp(jnp.where(causal, gc3[c][:, DN_HEADS + hd:DN_HEADS + hd + 1]
                          - gc_t[c][DN_HEADS + hd:DN_HEADS + hd + 1, :], -jnp.inf))
        for hd in range(DN_HEADS) for c in range(chunks)], axis=0)

    kb = k * bet
    g_all = jnp.einsum('nid,njd->nij', jnp.concatenate([kb, q], axis=1).astype(BF16), k.astype(BF16),
                       preferred_element_type=F32)
    a_low = jnp.where(strict, g_all[:, :CHUNK] * decay, 0.0)
    attn = (g_all[:, CHUNK:] * decay).astype(BF16)

    l_mat = eye + a_low
    l_bf = l_mat.astype(BF16)
    t_mat = eye - a_low
    for _ in range(4):
        res = eye - _bmm(l_bf, t_mat.astype(BF16))
        t_mat = t_mat + _bmm(t_mat.astype(BF16), res.astype(BF16))
    l_lo = (l_mat - l_bf.astype(F32)).astype(BF16)
    t_hi = t_mat.astype(BF16)
    t_lo = (t_mat - t_hi.astype(F32)).astype(BF16)
    res = eye - (_bmm(l_bf, t_hi) + _bmm(l_lo, t_hi) + _bmm(l_bf, t_lo))
    t_mat = t_mat + _bmm(t_hi, res.astype(BF16))

    eg = jnp.exp(gc)
    uw = _bmm(t_mat.astype(BF16), jnp.concatenate([v * bet, kb * eg], axis=2).astype(BF16))
    u = uw[:, :, :HEAD_DIM]
    wq_lhs = jnp.concatenate([uw[:, :, HEAD_DIM:], q * eg], axis=1).astype(BF16)
    g_last = gc[:, CHUNK - 1:CHUNK, :]
    k_dec = (k * jnp.exp(g_last - gc)).astype(BF16)
    eg_last = jnp.exp(g_last)

    state = [s_ref[hd] for hd in range(DN_HEADS)]
    for c in range(chunks):
        idx = [hd * chunks + c for hd in range(DN_HEADS)]
        wq = [_dot(wq_lhs[n], state[hd].astype(BF16)) for hd, n in enumerate(idx)]
        v_new = [(u[n] - wq[hd][:CHUNK]).astype(BF16) for hd, n in enumerate(idx)]
        o = [wq[hd][CHUNK:] + _dot(attn[n], v_new[hd]) for hd, n in enumerate(idx)]
        state = [state[hd] * eg_last[n] + lax.dot_general(k_dec[n], v_new[hd], (((0,), (0,)), ((), ())),
                                                          preferred_element_type=F32)
                 for hd, n in enumerate(idx)]
        for hd in range(DN_HEADS):
            cols = slice(hd * HEAD_DIM, (hd + 1) * HEAD_DIM)
            rows = slice(c * CHUNK, (c + 1) * CHUNK)
            r = o[hd] * lax.rsqrt(jnp.mean(o[hd] * o[hd], axis=-1, keepdims=True) + 1e-6)
            o_ref[rows, cols] = r * gain * _silu(z_ref[rows, cols])
    for hd in range(DN_HEADS):
        s_ref[hd] = state[hd]

    @pl.when(j == pl.num_programs(1) - 1)
    def _():
        sfin_ref[...] = s_ref[...]


def _delta(q, k, v, z, bg, s0, gain, chunks):
    bsz, seq, _ = q.shape
    rows = chunks * CHUNK
    assert seq % rows == 0

    def row(width):
        return pl.BlockSpec((None, rows, width), lambda b, j: (b, j, 0))

    state_shape = (DN_HEADS, HEAD_DIM, HEAD_DIM)
    return pl.pallas_call(
        functools.partial(_delta_kernel, chunks=chunks),
        grid=(bsz, seq // rows),
        in_specs=[row(DN_WIDTH)] * 4 + [row(LANES), _full_spec(state_shape), _full_spec(gain.shape)],
        out_specs=[row(DN_WIDTH), pl.BlockSpec((None,) + state_shape, lambda b, j: (b, 0, 0, 0))],
        out_shape=[jax.ShapeDtypeStruct((bsz, seq, DN_WIDTH), F32),
                   jax.ShapeDtypeStruct((bsz,) + state_shape, F32)],
        scratch_shapes=[pltpu.VMEM(state_shape, F32)],
        compiler_params=pltpu.CompilerParams(dimension_semantics=("parallel", "arbitrary"),
                                             vmem_limit_bytes=VMEM_LIMIT),
        name="delta",
    )(q, k, v, z, bg, s0, gain)


def _mix_out_kernel(x_ref, o_ref, c_ref, lng_ref, lnb_ref, wo_ref, g1_ref, b1_ref, wrh_ref, wrl_ref, br_ref,
                    h1s_ref, route_ref, route_t_ref, cnt_out_ref, cnt_ref, *, alpha):
    @pl.when(pl.program_id(0) == 0)
    def _():
        cnt_ref[...] = jnp.zeros_like(cnt_ref)

    h = _layer_norm(x_ref[...], lng_ref[...], lnb_ref[...])
    dn = o_ref.shape[1]
    mix = _dot(o_ref[...].astype(BF16), wo_ref[0:dn, :]) + _dot(c_ref[...].astype(BF16), wo_ref[dn:, :])
    h1 = _layer_norm(alpha * h + mix, g1_ref[...], b1_ref[...])
    _store_slabs(h1s_ref, h1)

    hh, hl = _split2(h1)
    logits = _dot(hh, wrh_ref[...]) + _dot(hl, wrh_ref[...]) + _dot(hh, wrl_ref[...]) + br_ref[...]
    lane = lax.broadcasted_iota(jnp.int32, logits.shape, 1).astype(F32)
    big = float(LANES)
    neg = -jnp.inf

    def first_argmax(vals):
        top = jnp.max(vals, axis=-1, keepdims=True)
        return top, jnp.min(jnp.where(vals == top, lane, big), axis=-1, keepdims=True)

    grp = jnp.where(lane < N_GROUPS, logits, neg)
    g_top, g_sel = first_argmax(grp)
    p_group = 1.0 / jnp.sum(jnp.exp(grp - g_top), axis=-1, keepdims=True)
    lo = N_GROUPS + EXPERTS_PER_GROUP * g_sel
    in_grp = jnp.where((lane >= lo) & (lane < lo + EXPERTS_PER_GROUP), logits, neg)
    m1, i1 = first_argmax(in_grp)
    m2, i2 = first_argmax(jnp.where(lane == i1, neg, in_grp))
    s = jnp.exp(m2 - m1)
    w1 = p_group / (1.0 + s)
    w2 = p_group * s / (1.0 + s)
    e1 = i1 - N_GROUPS
    e2 = i2 - N_GROUPS

    tm = logits.shape[0]
    oh1 = (lane == e1).astype(F32)
    oh2 = (lane == e2).astype(F32)
    both = oh1 + oh2
    ti = lax.broadcasted_iota(jnp.int32, (tm, tm), 0)
    tj = lax.broadcasted_iota(jnp.int32, (tm, tm), 1)
    base = _dot((ti > tj).astype(BF16), both.astype(BF16)) + cnt_ref[...]
    r1 = jnp.sum(oh1 * base, axis=-1, keepdims=True)
    r2 = jnp.sum(oh2 * base, axis=-1, keepdims=True)
    cnt_ref[...] = cnt_ref[...] + jnp.sum(both, axis=0, keepdims=True)
    cnt_out_ref[...] = jnp.broadcast_to(cnt_ref[...], cnt_out_ref.shape)

    vals = (e1, e2, w1, w2, r1, r2)
    route = jnp.zeros_like(logits)
    for idx, val in enumerate(vals):
        route = jnp.where(lane == idx, val, route)
    route_ref[...] = route
    route_t_ref[...] = route.T[0:SUBLANES, :]


def _mix_out(x2d, o2d, c2d, p, tm, alpha):
    n, d = x2d.shape
    assert n % tm == 0
    slabs = d // LANES

    def row(width):
        return pl.BlockSpec((tm, width), lambda i: (i, 0))

    consts = [p['ln_emb_g'], p['ln_emb_b'], p['w_out'], p['ln1_g'], p['ln1_b'], p['w_r_hi'], p['w_r_lo'], p['b_r']]
    return pl.pallas_call(
        functools.partial(_mix_out_kernel, alpha=alpha),
        grid=(n // tm,),
        in_specs=[row(d), row(o2d.shape[1]), row(c2d.shape[1])] + [_full_spec(c.shape) for c in consts],
        out_specs=[pl.BlockSpec((tm * slabs, LANES), lambda i: (i, 0)), row(LANES),
                   pl.BlockSpec((SUBLANES, tm), lambda i: (0, i)), _full_spec((SUBLANES, LANES))],
        out_shape=[jax.ShapeDtypeStruct((n * slabs, LANES), F32), jax.ShapeDtypeStruct((n, LANES), F32),
                   jax.ShapeDtypeStruct((SUBLANES, n), F32), jax.ShapeDtypeStruct((SUBLANES, LANES), F32)],
        scratch_shapes=[pltpu.VMEM((1, LANES), F32)],
        compiler_params=pltpu.CompilerParams(dimension_semantics=("arbitrary",), vmem_limit_bytes=VMEM_LIMIT),
        name="mix_out",
    )(x2d, o2d, c2d, *consts)


def _dispatch_kernel(dest_ref, pad_lo_ref, pad_hi_ref, h1s_hbm, xs_hbm, ring, zslab, fsem, ssem, zsem, *,
                     td, n_slabs, n_tokens):
    i = pl.program_id(0)
    nb = pl.num_programs(0)
    slot = i % RING
    tile_rows = td * n_slabs

    def fetch(step):
        start = pl.multiple_of(step * tile_rows, tile_rows)
        return pltpu.make_async_copy(h1s_hbm.at[pl.ds(start, tile_rows), :], ring.at[step % RING],
                                     fsem.at[step % RING])

    def wait_scatter(step):
        for _ in range(TOP_K):
            pltpu.make_async_copy(ring.at[step % RING], xs_hbm.at[pl.ds(0, tile_rows), :],
                                  ssem.at[step % RING]).wait()

    def pad_copy(row):
        return pltpu.make_async_copy(zslab, xs_hbm.at[pl.ds(pl.multiple_of(row * n_slabs, n_slabs), n_slabs), :],
                                     zsem.at[0])

    def for_each_pad_row(fn):
        def per_expert(e, carry):
            def per_row(row, c2):
                fn(row)
                return c2
            return lax.fori_loop(pad_lo_ref[e], pad_hi_ref[e], per_row, carry)
        lax.fori_loop(0, pad_lo_ref.shape[0], per_expert, 0)

    @pl.when(i == 0)
    def _():
        fetch(0).start()
        zslab[...] = jnp.zeros_like(zslab)
        for_each_pad_row(lambda row: pad_copy(row).start())

    @pl.when((i == 0) & (nb > 1))
    def _():
        fetch(1).start()

    fetch(i).wait()

    def issue_body(r, carry):
        src = ring.at[slot, pl.ds(pl.multiple_of(r * n_slabs, n_slabs), n_slabs), :]
        for k in range(TOP_K):
            dst_row = pl.multiple_of(dest_ref[k * n_tokens + i * td + r], n_slabs)
            pltpu.make_async_copy(src, xs_hbm.at[pl.ds(dst_row, n_slabs), :], ssem.at[slot]).start()
        return carry
    lax.fori_loop(0, td, issue_body, 0, unroll=DMA_UNROLL)

    @pl.when(i > 0)
    def _():
        wait_scatter(i - 1)

    @pl.when(i + 2 < nb)
    def _():
        fetch(i + 2).start()

    @pl.when(i == nb - 1)
    def _():
        wait_scatter(i)
        for_each_pad_row(lambda row: pad_copy(row).wait())


def _dispatch(dest_rows, pad_lo, pad_hi, h1s, cap_rows, td, n_slabs):
    n = h1s.shape[0] // n_slabs
    assert n % td == 0
    grid_spec = pltpu.PrefetchScalarGridSpec(
        num_scalar_prefetch=3,
        grid=(n // td,),
        in_specs=[pl.BlockSpec(memory_space=pl.ANY)],
        out_specs=pl.BlockSpec(memory_space=pl.ANY),
        scratch_shapes=[pltpu.VMEM((RING, td * n_slabs, LANES), F32), pltpu.VMEM((n_slabs, LANES), F32),
                        pltpu.SemaphoreType.DMA((RING,)), pltpu.SemaphoreType.DMA((RING,)),
                        pltpu.SemaphoreType.DMA((1,))],
    )
    return pl.pallas_call(
        functools.partial(_dispatch_kernel, td=td, n_slabs=n_slabs, n_tokens=n),
        grid_spec=grid_spec,
        out_shape=jax.ShapeDtypeStruct((cap_rows * n_slabs, LANES), F32),
        compiler_params=pltpu.CompilerParams(dimension_semantics=("arbitrary",), disable_bounds_checks=True),
        name="dispatch",
    )(dest_rows, pad_lo, pad_hi, h1s)


def _expert_kernel(be_ref, nu_ref, xs_ref, wg_ref, wu_ref, wd_ref, y_ref, *, bm, n_slabs):
    del be_ref
    used = pl.program_id(0) < nu_ref[0]

    @pl.when(used)
    def _():
        xb = _load_slabs(xs_ref, bm, n_slabs).astype(BF16)
        hid = _silu(_dot(xb, wg_ref[...])) * _dot(xb, wu_ref[...])
        _store_slabs(y_ref, _dot(hid.astype(BF16), wd_ref[...]))

    @pl.when(jnp.logical_not(used))
    def _():
        y_ref[...] = jnp.zeros_like(y_ref)


def _experts(block_expert, n_used, xs, w_gate, w_up, w_down, bm):
    n_blocks = block_expert.shape[0]
    d = w_gate.shape[1]
    ff = w_gate.shape[2]
    n_slabs = d // LANES

    def blk(i, be, nu):
        return jnp.minimum(i, nu[0] - 1)

    grid_spec = pltpu.PrefetchScalarGridSpec(
        num_scalar_prefetch=2,
        grid=(n_blocks,),
        in_specs=[pl.BlockSpec((bm * n_slabs, LANES), lambda i, be, nu: (blk(i, be, nu), 0)),
                  pl.BlockSpec((None, d, ff), lambda i, be, nu: (be[blk(i, be, nu)], 0, 0)),
                  pl.BlockSpec((None, d, ff), lambda i, be, nu: (be[blk(i, be, nu)], 0, 0)),
                  pl.BlockSpec((None, ff, d), lambda i, be, nu: (be[blk(i, be, nu)], 0, 0))],
        out_specs=pl.BlockSpec((bm * n_slabs, LANES), lambda i, be, nu: (i, 0)),
    )
    return pl.pallas_call(
        functools.partial(_expert_kernel, bm=bm, n_slabs=n_slabs),
        grid_spec=grid_spec,
        out_shape=jax.ShapeDtypeStruct(xs.shape, F32),
        compiler_params=pltpu.CompilerParams(dimension_semantics=("arbitrary",), vmem_limit_bytes=VMEM_LIMIT),
        name="experts",
    )(block_expert, n_used, xs, w_gate, w_up, w_down)


def _combine_kernel(dest_ref, y_hbm, h1s_ref, route_ref, g2_ref, b2_ref, out_ref, ybuf, sem, *, tm, n_slabs, alpha):
    i = pl.program_id(0)
    nb = pl.num_programs(0)
    part = tm * n_slabs

    def issue(blk, slot):
        def body(r, carry):
            for k in range(TOP_K):
                src_row = pl.multiple_of(dest_ref[k * (nb * tm) + blk * tm + r], n_slabs)
                dst_row = pl.multiple_of((slot * TOP_K + k) * part + r * n_slabs, n_slabs)
                pltpu.make_async_copy(y_hbm.at[pl.ds(src_row, n_slabs), :], ybuf.at[pl.ds(dst_row, n_slabs), :],
                                      sem.at[slot]).start()
            return carry
        lax.fori_loop(0, tm, body, 0, unroll=DMA_UNROLL)

    @pl.when(i == 0)
    def _():
        issue(0, 0)

    @pl.when(i + 1 < nb)
    def _():
        issue(i + 1, (i + 1) % 2)

    slot = i % 2
    base = pl.multiple_of(slot * (TOP_K * part), TOP_K * part)
    pltpu.make_async_copy(y_hbm.at[pl.ds(0, TOP_K * part), :], ybuf.at[pl.ds(base, TOP_K * part), :],
                          sem.at[slot]).wait()

    route = route_ref[...]
    ffn = (_load_slabs(ybuf, tm, n_slabs, base) * route[:, 2:3]
           + _load_slabs(ybuf, tm, n_slabs, base + part) * route[:, 3:4])
    h1 = _load_slabs(h1s_ref, tm, n_slabs)
    out_ref[...] = _layer_norm(alpha * h1 + ffn, g2_ref[...], b2_ref[...])


def _combine(dest_rows, y_sorted, h1s, route, ln2_g, ln2_b, tm, alpha):
    d = ln2_g.shape[1]
    n_slabs = d // LANES
    n = h1s.shape[0] // n_slabs
    assert n % tm == 0
    grid_spec = pltpu.PrefetchScalarGridSpec(
        num_scalar_prefetch=1,
        grid=(n // tm,),
        in_specs=[pl.BlockSpec(memory_space=pl.ANY),
                  pl.BlockSpec((tm * n_slabs, LANES), lambda i, dest: (i, 0)),
                  pl.BlockSpec((tm, LANES), lambda i, dest: (i, 0)),
                  pl.BlockSpec((1, d), lambda i, dest: (0, 0)),
                  pl.BlockSpec((1, d), lambda i, dest: (0, 0))],
        out_specs=pl.BlockSpec((tm, d), lambda i, dest: (i, 0)),
        scratch_shapes=[pltpu.VMEM((2 * TOP_K * tm * n_slabs, LANES), F32), pltpu.SemaphoreType.DMA((2,))],
    )
    return pl.pallas_call(
        functools.partial(_combine_kernel, tm=tm, n_slabs=n_slabs, alpha=alpha),
        grid_spec=grid_spec,
        out_shape=jax.ShapeDtypeStruct((n, d), F32),
        compiler_params=pltpu.CompilerParams(dimension_semantics=("arbitrary",), vmem_limit_bytes=VMEM_LIMIT,
                                             disable_bounds_checks=True),
        name="combine",
    )(dest_rows, y_sorted, h1s, route, ln2_g, ln2_b)


def _dispatch_plan(route_t, counts, bm, n_slabs):
    n = route_t.shape[1]
    expert_id = route_t[0:TOP_K].astype(jnp.int32)
    rank = route_t[4:4 + TOP_K].astype(jnp.int32)
    padded = (counts + bm - 1) // bm * bm
    pad_end = jnp.cumsum(padded)
    pad_start = pad_end - padded
    experts = jnp.arange(N_EXPERTS, dtype=jnp.int32)[:, None, None]
    dest = jnp.sum(jnp.where(expert_id[None] == experts, pad_start[:, None, None], 0), axis=0) + rank
    n_blocks = (n * TOP_K + bm - 1) // bm + N_EXPERTS
    block_start = jnp.arange(n_blocks, dtype=jnp.int32) * bm
    block_expert = jnp.minimum(jnp.sum((block_start[:, None] >= pad_end[None, :]).astype(jnp.int32), axis=1),
                               N_EXPERTS - 1).astype(jnp.int32)
    n_used = (pad_end[-1:] // bm).astype(jnp.int32)
    dest_rows = (dest * n_slabs).reshape(-1).astype(jnp.int32)
    pad_lo = jnp.concatenate([pad_start + counts, pad_end[-1:]]).astype(jnp.int32)
    pad_hi = jnp.concatenate([pad_end, jnp.full((1,), n_blocks * bm, pad_end.dtype)]).astype(jnp.int32)
    return dest_rows, pad_lo, pad_hi, block_expert, n_used, n_blocks


def _pad_lanes(w, width=LANES):
    return jnp.pad(w, [(0, 0)] * (w.ndim - 1) + [(0, width - w.shape[-1])])


def kernel(x, meta_tokens, ln_emb_g, ln_emb_b, w_in, conv_qkv_w, a_log, dt_bias, dn_norm_g, conv_dw_w, conv_dw_b, cv_norm_g, cv_norm_b, w_out, ln1_g, ln1_b, w_group, b_group, w_router, b_router, w_exp_gate, w_exp_up, w_exp_down, ln2_g, ln2_b):
    depth = w_in.shape[0]
    assert depth == 1, "single-layer block"
    bsz, seq, d = x.shape
    alpha = (2.0 * depth) ** 0.25
    qkv_w = 3 * DN_WIDTH
    w_in0 = w_in[0]
    glu_off = 4 * DN_WIDTH + 2 * DN_HEADS
    row = lambda a: a.reshape(1, -1).astype(F32)
    p = {
        'ln_emb_g': row(ln_emb_g), 'ln_emb_b': row(ln_emb_b),
        'w_qkv': w_in0[:, :qkv_w].astype(BF16),
        'w_z': w_in0[:, qkv_w:4 * DN_WIDTH].astype(BF16),
        'w_ba': _pad_lanes(w_in0[:, 4 * DN_WIDTH:glu_off]).astype(BF16),
        'w_glu': w_in0[:, glu_off:].astype(BF16),
        'conv_w': conv_qkv_w[0].astype(F32),
        'neg_a': _pad_lanes(jnp.concatenate([jnp.zeros((DN_HEADS,), F32), -jnp.exp(a_log[0].astype(F32))])[None]),
        'dt_b': _pad_lanes(jnp.concatenate([jnp.zeros((DN_HEADS,), F32), dt_bias[0].astype(F32)])[None]),
        'dw_w': conv_dw_w[0].astype(F32), 'dw_b': row(conv_dw_b[0]),
        'cv_g': row(cv_norm_g[0]), 'cv_b': row(cv_norm_b[0]),
        'w_out': w_out[0].astype(BF16), 'ln1_g': row(ln1_g[0]), 'ln1_b': row(ln1_b[0]),
    }
    w_r = _pad_lanes(jnp.concatenate([w_group[0], w_router[0]], axis=1).astype(F32))
    p['w_r_hi'] = w_r.astype(BF16)
    p['w_r_lo'] = (w_r - p['w_r_hi'].astype(F32)).astype(BF16)
    p['b_r'] = _pad_lanes(jnp.concatenate([b_group[0], b_router[0]])[None].astype(F32))
    gain = row(dn_norm_g[0])

    conf_w = p['dw_w'].shape[1]
    zero_hq = jnp.zeros((QKV_HALO, qkv_w), F32)
    zero_hc = jnp.zeros((CONF_HALO, conf_w), F32)
    mq, mk, mv, mz, _, mbg, halo_q, halo_c = _mix_in(meta_tokens[None].astype(F32), p, zero_hq, zero_hc, N_META)
    front = lambda a: jnp.pad(a, [(0, 0), (CHUNK - N_META, 0), (0, 0)])
    s_zero = jnp.zeros((DN_HEADS, HEAD_DIM, HEAD_DIM), F32)
    _, s_meta = _delta(front(mq), front(mk), front(mv), front(mz), front(mbg), s_zero, gain, 1)

    q, k, v, z, c, bg, _, _ = _mix_in(x, p, halo_q[0], halo_c[0], TM_IN)
    o, _ = _delta(q, k, v, z, bg, s_meta[0], gain, DELTA_CHUNKS)

    n = bsz * seq
    h1s, route, route_t, cnt = _mix_out(x.reshape(n, d), o.reshape(n, DN_WIDTH), c.reshape(n, conf_w), p,
                                        TM_OUT, alpha)

    n_slabs = d // LANES
    counts = cnt[0, :N_EXPERTS].astype(jnp.int32)
    dest_rows, pad_lo, pad_hi, block_expert, n_used, n_blocks = _dispatch_plan(route_t, counts, BM_EXPERT, n_slabs)
    xs = _dispatch(dest_rows, pad_lo, pad_hi, h1s, n_blocks * BM_EXPERT, TD_DISPATCH, n_slabs)
    y_sorted = _experts(block_expert, n_used, xs, w_exp_gate[0].astype(BF16), w_exp_up[0].astype(BF16),
                        w_exp_down[0].astype(BF16), BM_EXPERT)
    out = _combine(dest_rows, y_sorted, h1s, route, row(ln2_g[0]), row(ln2_b[0]), TM_COMBINE, alpha)
    return out.reshape(bsz, seq, d)
```

```python
import functools

import jax
import jax.numpy as jnp
from jax import lax
from jax.experimental import pallas as pl
from jax.experimental.pallas import tpu as pltpu

F32 = jnp.float32
BF16 = jnp.bfloat16

NORM_EPS = 1e-5
N_META = 16
DN_HEADS = 4
HEAD_DIM = 128
DN_WIDTH = DN_HEADS * HEAD_DIM
CHUNK = 64
SHORT_CONV = 4
CONF_KERNEL = 31
N_GROUPS = 4
EXPERTS_PER_GROUP = 8
N_EXPERTS = N_GROUPS * EXPERTS_PER_GROUP
TOP_K = 2
LANES = 128
SUBLANES = 8
QKV_HALO = 8
CONF_HALO = 32
VMEM_LIMIT = 56 * 1024 * 1024

TM_IN = 512
DELTA_CHUNKS = 8
TM_OUT = 512
BM_EXPERT = 256
TM_COMBINE = 256
TD_DISPATCH = 256
RING = 3
DMA_UNROLL = 8


def _dot(a, b):
    return jnp.dot(a, b, preferred_element_type=F32)


def _split2(x):
    hi = x.astype(BF16)
    lo = (x - hi.astype(F32)).astype(BF16)
    return hi, lo


def _dot_hilo(a, b):
    ah, al = _split2(a)
    bh, bl = _split2(b)
    return _dot(ah, bh) + _dot(al, bh) + _dot(ah, bl)


def _dot_exact01(m01, x):
    x1 = x.astype(BF16)
    r1 = x - x1.astype(F32)
    x2 = r1.astype(BF16)
    x3 = (r1 - x2.astype(F32)).astype(BF16)
    return _dot(m01, x1) + _dot(m01, x2) + _dot(m01, x3)


def _sigmoid(x):
    return 1.0 / (1.0 + jnp.exp(-x))


def _silu(x):
    return x * _sigmoid(x)


def _layer_norm(x, g, b):
    mu = jnp.mean(x, axis=-1, keepdims=True)
    xc = x - mu
    var = jnp.mean(xc * xc, axis=-1, keepdims=True)
    return xc * lax.rsqrt(var + NORM_EPS) * g + b


def _store_slabs(ref, val):
    rows, d = val.shape
    n_slabs = d // LANES
    for s in range(n_slabs):
        ref[pl.ds(s, rows, stride=n_slabs), :] = val[:, s * LANES:(s + 1) * LANES]


def _load_slabs(ref, rows, n_slabs, base=0):
    return jnp.concatenate([ref[pl.ds(base + s, rows, stride=n_slabs), :] for s in range(n_slabs)], axis=1)


def _full_spec(shape):
    nd = len(shape)
    return pl.BlockSpec(shape, lambda *_: (0,) * nd)


def _causal_depthwise(ext_ref, w_ref, n_taps, halo, tm):
    ext = ext_ref[...]
    rows = ext.shape[0]
    first = halo - (n_taps - 1)
    acc = None
    for phase in range(SUBLANES):
        taps = [k for k in range(n_taps) if (first + k) % SUBLANES == phase]
        if not taps:
            continue
        src = ext if phase == 0 else pltpu.roll(ext, rows - phase, axis=0)
        for k in taps:
            base = first + k - phase
            term = src[base:base + tm, :] * w_ref[k:k + 1, :]
            acc = term if acc is None else acc + term
    return acc
def _mix_in_kernel(x_ref, lng_ref, lnb_ref, wqkv_ref, wz_ref, wglu_ref, wba_ref, cw_ref, nega_ref,
                   dtb_ref, dww_ref, dwb_ref, cvg_ref, cvb_ref, hq_in_ref, hc_in_ref,
                   q_ref, k_ref, v_ref, z_ref, c_ref, bg_ref, hq_out_ref, hc_out_ref,
                   qkv_ext, c_ext):
    tm = x_ref.shape[0]

    @pl.when(pl.program_id(1) == 0)
    def _():
        qkv_ext[0:QKV_HALO, :] = hq_in_ref[...]
        c_ext[0:CONF_HALO, :] = hc_in_ref[...]

    h = _layer_norm(x_ref[...], lng_ref[...], lnb_ref[...])
    hb = h.astype(BF16)

    qkv_ext[QKV_HALO:QKV_HALO + tm, :] = _dot(hb, wqkv_ref[...])
    qkv = _silu(_causal_depthwise(qkv_ext, cw_ref, SHORT_CONV, QKV_HALO, tm))
    for hd in range(DN_HEADS):
        lo = hd * HEAD_DIM
        qh = qkv[:, lo:lo + HEAD_DIM]
        kh = qkv[:, DN_WIDTH + lo:DN_WIDTH + lo + HEAD_DIM]
        q_ref[:, lo:lo + HEAD_DIM] = qh * (lax.rsqrt(jnp.sum(qh * qh, axis=-1, keepdims=True) + 1e-6)
                                           * (HEAD_DIM ** -0.5))
        k_ref[:, lo:lo + HEAD_DIM] = kh * lax.rsqrt(jnp.sum(kh * kh, axis=-1, keepdims=True) + 1e-6)
    v_ref[...] = qkv[:, 2 * DN_WIDTH:]
    z_ref[...] = _dot(hb, wz_ref[...])

    ba = _dot(hb, wba_ref[...])
    lane = lax.broadcasted_iota(jnp.int32, ba.shape, 1)
    sp_in = ba + dtb_ref[...]
    softplus = jnp.maximum(sp_in, 0.0) + jnp.log(1.0 + jnp.exp(-jnp.abs(sp_in)))
    bg_ref[...] = jnp.where(lane < DN_HEADS, _sigmoid(ba), nega_ref[...] * softplus)

    glu = _dot(hb, wglu_ref[...])
    cw = glu.shape[1] // 2
    c_ext[CONF_HALO:CONF_HALO + tm, :] = glu[:, :cw] * _sigmoid(glu[:, cw:])
    conv = _causal_depthwise(c_ext, dww_ref, CONF_KERNEL, CONF_HALO, tm) + dwb_ref[...]
    c_ref[...] = _silu(_layer_norm(conv, cvg_ref[...], cvb_ref[...]))

    q_tail = qkv_ext[tm:tm + QKV_HALO, :]
    c_tail = c_ext[tm:tm + CONF_HALO, :]
    qkv_ext[0:QKV_HALO, :] = q_tail
    c_ext[0:CONF_HALO, :] = c_tail
    hq_out_ref[...] = q_tail
    hc_out_ref[...] = c_tail


def _mix_in(x, p, halo_q, halo_c, tm):
    bsz, seq, d = x.shape
    assert seq % tm == 0
    qkv_w = 3 * DN_WIDTH
    conf_w = p['dw_w'].shape[1]

    def row(width):
        return pl.BlockSpec((None, tm, width), lambda b, t: (b, t, 0))

    def per_batch(rows, width):
        return pl.BlockSpec((None, rows, width), lambda b, t: (b, 0, 0))

    consts = [p['ln_emb_g'], p['ln_emb_b'], p['w_qkv'], p['w_z'], p['w_glu'], p['w_ba'], p['conv_w'],
              p['neg_a'], p['dt_b'], p['dw_w'], p['dw_b'], p['cv_g'], p['cv_b'], halo_q, halo_c]
    sds = jax.ShapeDtypeStruct
    out_shape = ([sds((bsz, seq, DN_WIDTH), F32)] * 4 + [sds((bsz, seq, conf_w), F32),
                 sds((bsz, seq, LANES), F32), sds((bsz, QKV_HALO, qkv_w), F32),
                 sds((bsz, CONF_HALO, conf_w), F32)])
    out_specs = ([row(DN_WIDTH)] * 4 + [row(conf_w), row(LANES), per_batch(QKV_HALO, qkv_w),
                 per_batch(CONF_HALO, conf_w)])
    return pl.pallas_call(
        _mix_in_kernel,
        grid=(bsz, seq // tm),
        in_specs=[row(d)] + [_full_spec(c.shape) for c in consts],
        out_specs=out_specs,
        out_shape=out_shape,
        scratch_shapes=[pltpu.VMEM((QKV_HALO + tm, qkv_w), F32), pltpu.VMEM((CONF_HALO + tm, conf_w), F32)],
        compiler_params=pltpu.CompilerParams(dimension_semantics=("parallel", "arbitrary"),
                                             vmem_limit_bytes=VMEM_LIMIT),
        name="mix_in",
    )(x, *consts)


def _bmm(a, b):
    return jnp.einsum('nij,njk->nik', a, b, preferred_element_type=F32)


def _delta_kernel(q_ref, k_ref, v_ref, z_ref, bg_ref, s0_ref, gain_ref, o_ref, sfin_ref, s_ref, *, chunks):
    j = pl.program_id(1)

    @pl.when(j == 0)
    def _():
        s_ref[...] = s0_ref[...]

    ii = lax.broadcasted_iota(jnp.int32, (CHUNK, CHUNK), 0)
    jj = lax.broadcasted_iota(jnp.int32, (CHUNK, CHUNK), 1)
    causal = ii >= jj
    strict = ii > jj
    eye = (ii == jj).astype(F32)
    gain = gain_ref[...]

    bg3 = bg_ref[...].reshape(chunks, CHUNK, LANES)
    tril_b = jnp.broadcast_to(causal.astype(BF16), (chunks, CHUNK, CHUNK))
    p1 = bg3.astype(BF16)
    r1 = bg3 - p1.astype(F32)
    p2 = r1.astype(BF16)
    p3 = (r1 - p2.astype(F32)).astype(BF16)
    gc3 = _bmm(tril_b, p1) + _bmm(tril_b, p2) + _bmm(tril_b, p3)

    def heads(ref):
        return jnp.concatenate([ref[:, hd * HEAD_DIM:(hd + 1) * HEAD_DIM].reshape(chunks, CHUNK, HEAD_DIM)
                                for hd in range(DN_HEADS)], axis=0)
    q = heads(q_ref)
    k = heads(k_ref)
    v = heads(v_ref)
    bet = jnp.concatenate([bg3[:, :, hd:hd + 1] for hd in range(DN_HEADS)], axis=0)
    gc = jnp.concatenate([gc3[:, :, DN_HEADS + hd:DN_HEADS + hd + 1] for hd in range(DN_HEADS)], axis=0)
    gc_t = [gc3[c].T for c in range(chunks)]
    decay = jnp.stack([
        jnp.exp(jnp.where(causal, gc3[c][:, DN_HEADS + hd:DN_HEADS + hd + 1]
                          - gc_t[c][DN_HEADS + hd:DN_HEADS + hd + 1, :], -jnp.inf))
        for hd in range(DN_HEADS) for c in range(chunks)], axis=0)

    kb = k * bet
    g_all = jnp.einsum('nid,njd->nij', jnp.concatenate([kb, q], axis=1).astype(BF16), k.astype(BF16),
                       preferred_element_type=F32)
    a_low = jnp.where(strict, g_all[:, :CHUNK] * decay, 0.0)
    attn = (g_all[:, CHUNK:] * decay).astype(BF16)

    l_mat = eye + a_low
    l_bf = l_mat.astype(BF16)
    t_mat = eye - a_low
    for _ in range(4):
        res = eye - _bmm(l_bf, t_mat.astype(BF16))
        t_mat = t_mat + _bmm(t_mat.astype(BF16), res.astype(BF16))
    l_lo = (l_mat - l_bf.astype(F32)).astype(BF16)
    t_hi = t_mat.astype(BF16)
    t_lo = (t_mat - t_hi.astype(F32)).astype(BF16)
    res = eye - (_bmm(l_bf, t_hi) + _bmm(l_lo, t_hi) + _bmm(l_bf, t_lo))
    t_mat = t_mat + _bmm(t_hi, res.astype(BF16))

    eg = jnp.exp(gc)
    uw = _bmm(t_mat.astype(BF16), jnp.concatenate([v * bet, kb * eg], axis=2).astype(BF16))
    u = uw[:, :, :HEAD_DIM]
    wq_lhs = jnp.concatenate([uw[:, :, HEAD_DIM:], q * eg], axis=1).astype(BF16)
    g_last = gc[:, CHUNK - 1:CHUNK, :]
    k_dec = (k * jnp.exp(g_last - gc)).astype(BF16)
    eg_last = jnp.exp(g_last)

    state = [s_ref[hd] for hd in range(DN_HEADS)]
    for c in range(chunks):
        idx = [hd * chunks + c for hd in range(DN_HEADS)]
        wq = [_dot(wq_lhs[n], state[hd].astype(BF16)) for hd, n in enumerate(idx)]
        v_new = [(u[n] - wq[hd][:CHUNK]).astype(BF16) for hd, n in enumerate(idx)]
        o = [wq[hd][CHUNK:] + _dot(attn[n], v_new[hd]) for hd, n in enumerate(idx)]
        state = [state[hd] * eg_last[n] + lax.dot_general(k_dec[n], v_new[hd], (((0,), (0,)), ((), ())),
                                                          preferred_element_type=F32)
                 for hd, n in enumerate(idx)]
        for hd in range(DN_HEADS):
            cols = slice(hd * HEAD_DIM, (hd + 1) * HEAD_DIM)
            rows = slice(c * CHUNK, (c + 1) * CHUNK)
            r = o[hd] * lax.rsqrt(jnp.mean(o[hd] * o[hd], axis=-1, keepdims=True) + 1e-6)
            o_ref[rows, cols] = r * gain * _silu(z_ref[rows, cols])
    for hd in range(DN_HEADS):
        s_ref[hd] = state[hd]

    @pl.when(j == pl.num_programs(1) - 1)
    def _():
        sfin_ref[...] = s_ref[...]


def _delta(q, k, v, z, bg, s0, gain, chunks):
    bsz, seq, _ = q.shape
    rows = chunks * CHUNK
    assert seq % rows == 0

    def row(width):
        return pl.BlockSpec((None, rows, width), lambda b, j: (b, j, 0))

    state_shape = (DN_HEADS, HEAD_DIM, HEAD_DIM)
    return pl.pallas_call(
        functools.partial(_delta_kernel, chunks=chunks),
        grid=(bsz, seq // rows),
        in_specs=[row(DN_WIDTH)] * 4 + [row(LANES), _full_spec(state_shape), _full_spec(gain.shape)],
        out_specs=[row(DN_WIDTH), pl.BlockSpec((None,) + state_shape, lambda b, j: (b, 0, 0, 0))],
        out_shape=[jax.ShapeDtypeStruct((bsz, seq, DN_WIDTH), F32),
                   jax.ShapeDtypeStruct((bsz,) + state_shape, F32)],
        scratch_shapes=[pltpu.VMEM(state_shape, F32)],
        compiler_params=pltpu.CompilerParams(dimension_semantics=("parallel", "arbitrary"),
                                             vmem_limit_bytes=VMEM_LIMIT),
        name="delta",
    )(q, k, v, z, bg, s0, gain)


def _mix_out_kernel(x_ref, o_ref, c_ref, lng_ref, lnb_ref, wo_ref, g1_ref, b1_ref, wrh_ref, wrl_ref, br_ref,
                    h1s_ref, route_ref, route_t_ref, cnt_out_ref, cnt_ref, *, alpha):
    @pl.when(pl.program_id(0) == 0)
    def _():
        cnt_ref[...] = jnp.zeros_like(cnt_ref)

    h = _layer_norm(x_ref[...], lng_ref[...], lnb_ref[...])
    dn = o_ref.shape[1]
    mix = _dot(o_ref[...].astype(BF16), wo_ref[0:dn, :]) + _dot(c_ref[...].astype(BF16), wo_ref[dn:, :])
    h1 = _layer_norm(alpha * h + mix, g1_ref[...], b1_ref[...])
    _store_slabs(h1s_ref, h1)

    hh, hl = _split2(h1)
    logits = _dot(hh, wrh_ref[...]) + _dot(hl, wrh_ref[...]) + _dot(hh, wrl_ref[...]) + br_ref[...]
    lane = lax.broadcasted_iota(jnp.int32, logits.shape, 1).astype(F32)
    big = float(LANES)
    neg = -jnp.inf

    def first_argmax(vals):
        top = jnp.max(vals, axis=-1, keepdims=True)
        return top, jnp.min(jnp.where(vals == top, lane, big), axis=-1, keepdims=True)

    grp = jnp.where(lane < N_GROUPS, logits, neg)
    g_top, g_sel = first_argmax(grp)
    p_group = 1.0 / jnp.sum(jnp.exp(grp - g_top), axis=-1, keepdims=True)
    lo = N_GROUPS + EXPERTS_PER_GROUP * g_sel
    in_grp = jnp.where((lane >= lo) & (lane < lo + EXPERTS_PER_GROUP), logits, neg)
    m1, i1 = first_argmax(in_grp)
    m2, i2 = first_argmax(jnp.where(lane == i1, neg, in_grp))
    s = jnp.exp(m2 - m1)
    w1 = p_group / (1.0 + s)
    w2 = p_group * s / (1.0 + s)
    e1 = i1 - N_GROUPS
    e2 = i2 - N_GROUPS

    tm = logits.shape[0]
    oh1 = (lane == e1).astype(F32)
    oh2 = (lane == e2).astype(F32)
    both = oh1 + oh2
    ti = lax.broadcasted_iota(jnp.int32, (tm, tm), 0)
    tj = lax.broadcasted_iota(jnp.int32, (tm, tm), 1)
    base = _dot((ti > tj).astype(BF16), both.astype(BF16)) + cnt_ref[...]
    r1 = jnp.sum(oh1 * base, axis=-1, keepdims=True)
    r2 = jnp.sum(oh2 * base, axis=-1, keepdims=True)
    cnt_ref[...] = cnt_ref[...] + jnp.sum(both, axis=0, keepdims=True)
    cnt_out_ref[...] = jnp.broadcast_to(cnt_ref[...], cnt_out_ref.shape)

    vals = (e1, e2, w1, w2, r1, r2)
    route = jnp.zeros_like(logits)
    for idx, val in enumerate(vals):
        route = jnp.where(lane == idx, val, route)
    route_ref[...] = route
    route_t_ref[...] = route.T[0:SUBLANES, :]


def _mix_out(x2d, o2d, c2d, p, tm, alpha):
    n, d = x2d.shape
    assert n % tm == 0
    slabs = d // LANES

    def row(width):
        return pl.BlockSpec((tm, width), lambda i: (i, 0))

    consts = [p['ln_emb_g'], p['ln_emb_b'], p['w_out'], p['ln1_g'], p['ln1_b'], p['w_r_hi'], p['w_r_lo'], p['b_r']]
    return pl.pallas_call(
        functools.partial(_mix_out_kernel, alpha=alpha),
        grid=(n // tm,),
        in_specs=[row(d), row(o2d.shape[1]), row(c2d.shape[1])] + [_full_spec(c.shape) for c in consts],
        out_specs=[pl.BlockSpec((tm * slabs, LANES), lambda i: (i, 0)), row(LANES),
                   pl.BlockSpec((SUBLANES, tm), lambda i: (0, i)), _full_spec((SUBLANES, LANES))],
        out_shape=[jax.ShapeDtypeStruct((n * slabs, LANES), F32), jax.ShapeDtypeStruct((n, LANES), F32),
                   jax.ShapeDtypeStruct((SUBLANES, n), F32), jax.ShapeDtypeStruct((SUBLANES, LANES), F32)],
        scratch_shapes=[pltpu.VMEM((1, LANES), F32)],
        compiler_params=pltpu.CompilerParams(dimension_semantics=("arbitrary",), vmem_limit_bytes=VMEM_LIMIT),
        name="mix_out",
    )(x2d, o2d, c2d, *consts)


def _dispatch_kernel(dest_ref, pad_lo_ref, pad_hi_ref, h1s_hbm, xs_hbm, ring, zslab, fsem, ssem, zsem, *,
                     td, n_slabs, n_tokens):
    i = pl.program_id(0)
    nb = pl.num_programs(0)
    slot = i % RING
    tile_rows = td * n_slabs

    def fetch(step):
        start = pl.multiple_of(step * tile_rows, tile_rows)
        return pltpu.make_async_copy(h1s_hbm.at[pl.ds(start, tile_rows), :], ring.at[step % RING],
                                     fsem.at[step % RING])

    def wait_scatter(step):
        for _ in range(TOP_K):
            pltpu.make_async_copy(ring.at[step % RING], xs_hbm.at[pl.ds(0, tile_rows), :],
                                  ssem.at[step % RING]).wait()

    def pad_copy(row):
        return pltpu.make_async_copy(zslab, xs_hbm.at[pl.ds(pl.multiple_of(row * n_slabs, n_slabs), n_slabs), :],
                                     zsem.at[0])

    def for_each_pad_row(fn):
        def per_expert(e, carry):
            def per_row(row, c2):
                fn(row)
                return c2
            return lax.fori_loop(pad_lo_ref[e], pad_hi_ref[e], per_row, carry)
        lax.fori_loop(0, pad_lo_ref.shape[0], per_expert, 0)

    @pl.when(i == 0)
    def _():
        fetch(0).start()
        zslab[...] = jnp.zeros_like(zslab)
        for_each_pad_row(lambda row: pad_copy(row).start())

    @pl.when((i == 0) & (nb > 1))
    def _():
        fetch(1).start()

    fetch(i).wait()

    def issue_body(r, carry):
        src = ring.at[slot, pl.ds(pl.multiple_of(r * n_slabs, n_slabs), n_slabs), :]
        for k in range(TOP_K):
            dst_row = pl.multiple_of(dest_ref[k * n_tokens + i * td + r], n_slabs)
            pltpu.make_async_copy(src, xs_hbm.at[pl.ds(dst_row, n_slabs), :], ssem.at[slot]).start(priority=k)
        return carry
    lax.fori_loop(0, td, issue_body, 0, unroll=DMA_UNROLL)

    @pl.when(i > 0)
    def _():
        wait_scatter(i - 1)

    @pl.when(i + 2 < nb)
    def _():
        fetch(i + 2).start()

    @pl.when(i == nb - 1)
    def _():
        wait_scatter(i)
        for_each_pad_row(lambda row: pad_copy(row).wait())


def _dispatch(dest_rows, pad_lo, pad_hi, h1s, cap_rows, td, n_slabs):
    n = h1s.shape[0] // n_slabs
    assert n % td == 0
    grid_spec = pltpu.PrefetchScalarGridSpec(
        num_scalar_prefetch=3,
        grid=(n // td,),
        in_specs=[pl.BlockSpec(memory_space=pl.ANY)],
        out_specs=pl.BlockSpec(memory_space=pl.ANY),
        scratch_shapes=[pltpu.VMEM((RING, td * n_slabs, LANES), F32), pltpu.VMEM((n_slabs, LANES), F32),
                        pltpu.SemaphoreType.DMA((RING,)), pltpu.SemaphoreType.DMA((RING,)),
                        pltpu.SemaphoreType.DMA((1,))],
    )
    return pl.pallas_call(
        functools.partial(_dispatch_kernel, td=td, n_slabs=n_slabs, n_tokens=n),
        grid_spec=grid_spec,
        out_shape=jax.ShapeDtypeStruct((cap_rows * n_slabs, LANES), F32),
        compiler_params=pltpu.CompilerParams(dimension_semantics=("arbitrary",), disable_bounds_checks=True),
        name="dispatch",
    )(dest_rows, pad_lo, pad_hi, h1s)


def _expert_kernel(be_ref, nu_ref, xs_ref, wg_ref, wu_ref, wd_ref, y_ref, wg_bf, wu_bf, wd_bf, *, bm, n_slabs):
    i = pl.program_id(0)
    used = i < nu_ref[0]

    @pl.when(used & ((i == 0) | (be_ref[i] != be_ref[jnp.maximum(i - 1, 0)])))
    def _():
        wg_bf[...] = wg_ref[...].astype(BF16)
        wu_bf[...] = wu_ref[...].astype(BF16)
        wd_bf[...] = wd_ref[...].astype(BF16)

    @pl.when(used)
    def _():
        xb = _load_slabs(xs_ref, bm, n_slabs).astype(BF16)
        hid = _silu(_dot(xb, wg_bf[...])) * _dot(xb, wu_bf[...])
        _store_slabs(y_ref, _dot(hid.astype(BF16), wd_bf[...]))

    @pl.when(jnp.logical_not(used))
    def _():
        y_ref[...] = jnp.zeros_like(y_ref)


def _experts(block_expert, n_used, xs, w_gate, w_up, w_down, bm):
    n_blocks = block_expert.shape[0]
    d = w_gate.shape[1]
    ff = w_gate.shape[2]
    n_slabs = d // LANES

    def blk(i, be, nu):
        return jnp.minimum(i, nu[0] - 1)

    grid_spec = pltpu.PrefetchScalarGridSpec(
        num_scalar_prefetch=2,
        grid=(n_blocks,),
        in_specs=[pl.BlockSpec((bm * n_slabs, LANES), lambda i, be, nu: (blk(i, be, nu), 0)),
                  pl.BlockSpec((None, d, ff), lambda i, be, nu: (be[blk(i, be, nu)], 0, 0)),
                  pl.BlockSpec((None, d, ff), lambda i, be, nu: (be[blk(i, be, nu)], 0, 0)),
                  pl.BlockSpec((None, ff, d), lambda i, be, nu: (be[blk(i, be, nu)], 0, 0))],
        out_specs=pl.BlockSpec((bm * n_slabs, LANES), lambda i, be, nu: (i, 0)),
        scratch_shapes=[pltpu.VMEM((d, ff), BF16), pltpu.VMEM((d, ff), BF16), pltpu.VMEM((ff, d), BF16)],
    )
    return pl.pallas_call(
        functools.partial(_expert_kernel, bm=bm, n_slabs=n_slabs),
        grid_spec=grid_spec,
        out_shape=jax.ShapeDtypeStruct(xs.shape, F32),
        compiler_params=pltpu.CompilerParams(dimension_semantics=("arbitrary",), vmem_limit_bytes=VMEM_LIMIT),
        name="experts",
    )(block_expert, n_used, xs, w_gate, w_up, w_down)


def _combine_kernel(dest_ref, y_hbm, h1s_ref, route_ref, g2_ref, b2_ref, out_ref, ybuf, sem, *, tm, n_slabs, alpha):
    i = pl.program_id(0)
    nb = pl.num_programs(0)
    part = tm * n_slabs

    def issue(blk, slot):
        def body(r, carry):
            for k in range(TOP_K):
                src_row = pl.multiple_of(dest_ref[k * (nb * tm) + blk * tm + r], n_slabs)
                dst_row = pl.multiple_of((slot * TOP_K + k) * part + r * n_slabs, n_slabs)
                pltpu.make_async_copy(y_hbm.at[pl.ds(src_row, n_slabs), :], ybuf.at[pl.ds(dst_row, n_slabs), :],
                                      sem.at[slot]).start(priority=k)
            return carry
        lax.fori_loop(0, tm, body, 0, unroll=DMA_UNROLL)

    @pl.when(i == 0)
    def _():
        issue(0, 0)

    @pl.when(i + 1 < nb)
    def _():
        issue(i + 1, (i + 1) % 2)

    slot = i % 2
    base = pl.multiple_of(slot * (TOP_K * part), TOP_K * part)
    pltpu.make_async_copy(y_hbm.at[pl.ds(0, TOP_K * part), :], ybuf.at[pl.ds(base, TOP_K * part), :],
                          sem.at[slot]).wait()

    route = route_ref[...]
    ffn = (_load_slabs(ybuf, tm, n_slabs, base) * route[:, 2:3]
           + _load_slabs(ybuf, tm, n_slabs, base + part) * route[:, 3:4])
    h1 = _load_slabs(h1s_ref, tm, n_slabs)
    out_ref[...] = _layer_norm(alpha * h1 + ffn, g2_ref[...], b2_ref[...])


def _combine(dest_rows, y_sorted, h1s, route, ln2_g, ln2_b, tm, alpha):
    d = ln2_g.shape[1]
    n_slabs = d // LANES
    n = h1s.shape[0] // n_slabs
    assert n % tm == 0
    grid_spec = pltpu.PrefetchScalarGridSpec(
        num_scalar_prefetch=1,
        grid=(n // tm,),
        in_specs=[pl.BlockSpec(memory_space=pl.ANY),
                  pl.BlockSpec((tm * n_slabs, LANES), lambda i, dest: (i, 0)),
                  pl.BlockSpec((tm, LANES), lambda i, dest: (i, 0)),
                  pl.BlockSpec((1, d), lambda i, dest: (0, 0)),
                  pl.BlockSpec((1, d), lambda i, dest: (0, 0))],
        out_specs=pl.BlockSpec((tm, d), lambda i, dest: (i, 0)),
        scratch_shapes=[pltpu.VMEM((2 * TOP_K * tm * n_slabs, LANES), F32), pltpu.SemaphoreType.DMA((2,))],
    )
    return pl.pallas_call(
        functools.partial(_combine_kernel, tm=tm, n_slabs=n_slabs, alpha=alpha),
        grid_spec=grid_spec,
        out_shape=jax.ShapeDtypeStruct((n, d), F32),
        compiler_params=pltpu.CompilerParams(dimension_semantics=("arbitrary",), vmem_limit_bytes=VMEM_LIMIT,
                                             disable_bounds_checks=True),
        name="combine",
    )(dest_rows, y_sorted, h1s, route, ln2_g, ln2_b)


def _dispatch_plan(route_t, counts, bm, n_slabs):
    n = route_t.shape[1]
    expert_id = route_t[0:TOP_K].astype(jnp.int32)
    rank = route_t[4:4 + TOP_K].astype(jnp.int32)
    padded = (counts + bm - 1) // bm * bm
    pad_end = jnp.cumsum(padded)
    pad_start = pad_end - padded
    experts = jnp.arange(N_EXPERTS, dtype=jnp.int32)[:, None, None]
    dest = jnp.sum(jnp.where(expert_id[None] == experts, pad_start[:, None, None], 0), axis=0) + rank
    n_blocks = (n * TOP_K + bm - 1) // bm + N_EXPERTS
    block_start = jnp.arange(n_blocks, dtype=jnp.int32) * bm
    block_expert = jnp.minimum(jnp.sum((block_start[:, None] >= pad_end[None, :]).astype(jnp.int32), axis=1),
                               N_EXPERTS - 1).astype(jnp.int32)
    n_used = (pad_end[-1:] // bm).astype(jnp.int32)
    dest_rows = (dest * n_slabs).reshape(-1).astype(jnp.int32)
    pad_lo = jnp.concatenate([pad_start + counts, pad_end[-1:]]).astype(jnp.int32)
    pad_hi = jnp.concatenate([pad_end, jnp.full((1,), n_blocks * bm, pad_end.dtype)]).astype(jnp.int32)
    return dest_rows, pad_lo, pad_hi, block_expert, n_used, n_blocks


def _pad_lanes(w, width=LANES):
    return jnp.pad(w, [(0, 0)] * (w.ndim - 1) + [(0, width - w.shape[-1])])


def kernel(x, meta_tokens, ln_emb_g, ln_emb_b, w_in, conv_qkv_w, a_log, dt_bias, dn_norm_g, conv_dw_w, conv_dw_b, cv_norm_g, cv_norm_b, w_out, ln1_g, ln1_b, w_group, b_group, w_router, b_router, w_exp_gate, w_exp_up, w_exp_down, ln2_g, ln2_b):
    depth = w_in.shape[0]
    assert depth == 1, "single-layer block"
    bsz, seq, d = x.shape
    alpha = (2.0 * depth) ** 0.25
    qkv_w = 3 * DN_WIDTH
    w_in0 = w_in[0]
    glu_off = 4 * DN_WIDTH + 2 * DN_HEADS
    row = lambda a: a.reshape(1, -1).astype(F32)
    p = {
        'ln_emb_g': row(ln_emb_g), 'ln_emb_b': row(ln_emb_b),
        'w_qkv': w_in0[:, :qkv_w].astype(BF16),
        'w_z': w_in0[:, qkv_w:4 * DN_WIDTH].astype(BF16),
        'w_ba': _pad_lanes(w_in0[:, 4 * DN_WIDTH:glu_off]).astype(BF16),
        'w_glu': w_in0[:, glu_off:].astype(BF16),
        'conv_w': conv_qkv_w[0].astype(F32),
        'neg_a': _pad_lanes(jnp.concatenate([jnp.zeros((DN_HEADS,), F32), -jnp.exp(a_log[0].astype(F32))])[None]),
        'dt_b': _pad_lanes(jnp.concatenate([jnp.zeros((DN_HEADS,), F32), dt_bias[0].astype(F32)])[None]),
        'dw_w': conv_dw_w[0].astype(F32), 'dw_b': row(conv_dw_b[0]),
        'cv_g': row(cv_norm_g[0]), 'cv_b': row(cv_norm_b[0]),
        'w_out': w_out[0].astype(BF16), 'ln1_g': row(ln1_g[0]), 'ln1_b': row(ln1_b[0]),
    }
    w_r = _pad_lanes(jnp.concatenate([w_group[0], w_router[0]], axis=1).astype(F32))
    p['w_r_hi'] = w_r.astype(BF16)
    p['w_r_lo'] = (w_r - p['w_r_hi'].astype(F32)).astype(BF16)
    p['b_r'] = _pad_lanes(jnp.concatenate([b_group[0], b_router[0]])[None].astype(F32))
    gain = row(dn_norm_g[0])

    conf_w = p['dw_w'].shape[1]
    zero_hq = jnp.zeros((QKV_HALO, qkv_w), F32)
    zero_hc = jnp.zeros((CONF_HALO, conf_w), F32)
    mq, mk, mv, mz, _, mbg, halo_q, halo_c = _mix_in(meta_tokens[None].astype(F32), p, zero_hq, zero_hc, N_META)
    front = lambda a: jnp.pad(a, [(0, 0), (CHUNK - N_META, 0), (0, 0)])
    s_zero = jnp.zeros((DN_HEADS, HEAD_DIM, HEAD_DIM), F32)
    _, s_meta = _delta(front(mq), front(mk), front(mv), front(mz), front(mbg), s_zero, gain, 1)

    q, k, v, z, c, bg, _, _ = _mix_in(x, p, halo_q[0], halo_c[0], TM_IN)
    o, _ = _delta(q, k, v, z, bg, s_meta[0], gain, DELTA_CHUNKS)

    n = bsz * seq
    h1s, route, route_t, cnt = _mix_out(x.reshape(n, d), o.reshape(n, DN_WIDTH), c.reshape(n, conf_w), p,
                                        TM_OUT, alpha)

    n_slabs = d // LANES
    counts = cnt[0, :N_EXPERTS].astype(jnp.int32)
    dest_rows, pad_lo, pad_hi, block_expert, n_used, n_blocks = _dispatch_plan(route_t, counts, BM_EXPERT, n_slabs)
    xs = _dispatch(dest_rows, pad_lo, pad_hi, h1s, n_blocks * BM_EXPERT, TD_DISPATCH, n_slabs)
    first_layer = lambda w: w.reshape(w.shape[1:])
    y_sorted = _experts(block_expert, n_used, xs, first_layer(w_exp_gate), first_layer(w_exp_up),
                        first_layer(w_exp_down), BM_EXPERT)
    out = _combine(dest_rows, y_sorted, h1s, route, row(ln2_g[0]), row(ln2_b[0]), TM_COMBINE, alpha)
    return out.reshape(bsz, seq, d)
```

```python
import functools

import jax
import jax.numpy as jnp
from jax import lax
from jax.experimental import pallas as pl
from jax.experimental.pallas import tpu as pltpu

F32 = jnp.float32
BF16 = jnp.bfloat16

NORM_EPS = 1e-5
N_META = 16
DN_HEADS = 4
HEAD_DIM = 128
DN_WIDTH = DN_HEADS * HEAD_DIM
CHUNK = 64
SHORT_CONV = 4
CONF_KERNEL = 31
N_GROUPS = 4
EXPERTS_PER_GROUP = 8
N_EXPERTS = N_GROUPS * EXPERTS_PER_GROUP
TOP_K = 2
LANES = 128
SUBLANES = 8
QKV_HALO = 8
CONF_HALO = 32
VMEM_LIMIT = 56 * 1024 * 1024

TM_IN = 512
DELTA_CHUNKS = 8
TM_OUT = 512
BM_EXPERT = 256
TM_COMBINE = 256
TD_DISPATCH = 256
RING = 3
DMA_UNROLL = 8


def _dot(a, b):
    return jnp.dot(a, b, preferred_element_type=F32)


def _split2(x):
    hi = x.astype(BF16)
    lo = (x - hi.astype(F32)).astype(BF16)
    return hi, lo


def _dot_hilo(a, b):
    ah, al = _split2(a)
    bh, bl = _split2(b)
    return _dot(ah, bh) + _dot(al, bh) + _dot(ah, bl)


def _dot_exact01(m01, x):
    x1 = x.astype(BF16)
    r1 = x - x1.astype(F32)
    x2 = r1.astype(BF16)
    x3 = (r1 - x2.astype(F32)).astype(BF16)
    return _dot(m01, x1) + _dot(m01, x2) + _dot(m01, x3)


def _sigmoid(x):
    return 1.0 / (1.0 + jnp.exp(-x))


def _silu(x):
    return x * _sigmoid(x)


def _layer_norm(x, g, b):
    mu = jnp.mean(x, axis=-1, keepdims=True)
    xc = x - mu
    var = jnp.mean(xc * xc, axis=-1, keepdims=True)
    return xc * lax.rsqrt(var + NORM_EPS) * g + b


def _store_slabs(ref, val):
    rows, d = val.shape
    n_slabs = d // LANES
    for s in range(n_slabs):
        ref[pl.ds(s, rows, stride=n_slabs), :] = val[:, s * LANES:(s + 1) * LANES]


def _load_slabs(ref, rows, n_slabs, base=0):
    return jnp.concatenate([ref[pl.ds(base + s, rows, stride=n_slabs), :] for s in range(n_slabs)], axis=1)


def _full_spec(shape):
    nd = len(shape)
    return pl.BlockSpec(shape, lambda *_: (0,) * nd)


def _causal_depthwise(ext_ref, w_ref, n_taps, halo, tm):
    ext = ext_ref[...]
    rows = ext.shape[0]
    first = halo - (n_taps - 1)
    acc = None
    for phase in range(SUBLANES):
        taps = [k for k in range(n_taps) if (first + k) % SUBLANES == phase]
        if not taps:
            continue
        src = ext if phase == 0 else pltpu.roll(ext, rows - phase, axis=0)
        for k in taps:
            base = first + k - phase
            term = src[base:base + tm, :] * w_ref[k:k + 1, :]
            acc = term if acc is None else acc + term
    return acc
def _mix_in_kernel(x_ref, lng_ref, lnb_ref, wqkv_ref, wz_ref, wglu_ref, wba_ref, cw_ref, nega_ref,
                   dtb_ref, hq_in_ref, hc_in_ref,
                   q_ref, k_ref, v_ref, z_ref, c_ref, bg_ref, hq_out_ref, hc_out_ref,
                   qkv_ext, c_ext):
    tm = x_ref.shape[0]

    @pl.when(pl.program_id(1) == 0)
    def _():
        qkv_ext[0:QKV_HALO, :] = hq_in_ref[...]
        c_ext[0:CONF_HALO, :] = hc_in_ref[...]

    h = _layer_norm(x_ref[...], lng_ref[...], lnb_ref[...])
    hb = h.astype(BF16)

    qkv_ext[QKV_HALO:QKV_HALO + tm, :] = _dot(hb, wqkv_ref[...])
    qkv = _silu(_causal_depthwise(qkv_ext, cw_ref, SHORT_CONV, QKV_HALO, tm))
    for hd in range(DN_HEADS):
        lo = hd * HEAD_DIM
        qh = qkv[:, lo:lo + HEAD_DIM]
        kh = qkv[:, DN_WIDTH + lo:DN_WIDTH + lo + HEAD_DIM]
        q_ref[:, lo:lo + HEAD_DIM] = qh * (lax.rsqrt(jnp.sum(qh * qh, axis=-1, keepdims=True) + 1e-6)
                                           * (HEAD_DIM ** -0.5))
        k_ref[:, lo:lo + HEAD_DIM] = kh * lax.rsqrt(jnp.sum(kh * kh, axis=-1, keepdims=True) + 1e-6)
    v_ref[...] = qkv[:, 2 * DN_WIDTH:]
    z_ref[...] = _dot(hb, wz_ref[...])

    ba = _dot(hb, wba_ref[...])
    lane = lax.broadcasted_iota(jnp.int32, ba.shape, 1)
    sp_in = ba + dtb_ref[...]
    softplus = jnp.maximum(sp_in, 0.0) + jnp.log(1.0 + jnp.exp(-jnp.abs(sp_in)))
    bg_ref[...] = jnp.where(lane < DN_HEADS, _sigmoid(ba), nega_ref[...] * softplus)

    glu = _dot(hb, wglu_ref[...])
    cw = glu.shape[1] // 2
    c_pre = glu[:, :cw] * _sigmoid(glu[:, cw:])
    c_ref[...] = c_pre
    c_ext[CONF_HALO:CONF_HALO + tm, :] = c_pre

    q_tail = qkv_ext[tm:tm + QKV_HALO, :]
    c_tail = c_ext[tm:tm + CONF_HALO, :]
    qkv_ext[0:QKV_HALO, :] = q_tail
    c_ext[0:CONF_HALO, :] = c_tail
    hq_out_ref[...] = q_tail
    hc_out_ref[...] = c_tail


def _mix_in(x, p, halo_q, halo_c, tm):
    bsz, seq, d = x.shape
    assert seq % tm == 0
    qkv_w = 3 * DN_WIDTH
    conf_w = p['dw_w'].shape[1]

    def row(width):
        return pl.BlockSpec((None, tm, width), lambda b, t: (b, t, 0))

    def per_batch(rows, width):
        return pl.BlockSpec((None, rows, width), lambda b, t: (b, 0, 0))

    consts = [p['ln_emb_g'], p['ln_emb_b'], p['w_qkv'], p['w_z'], p['w_glu'], p['w_ba'], p['conv_w'],
              p['neg_a'], p['dt_b'], halo_q, halo_c]
    sds = jax.ShapeDtypeStruct
    out_shape = ([sds((bsz, seq, DN_WIDTH), F32)] * 4 + [sds((bsz, seq, conf_w), F32),
                 sds((bsz, seq, LANES), F32), sds((bsz, QKV_HALO, qkv_w), F32),
                 sds((bsz, CONF_HALO, conf_w), F32)])
    out_specs = ([row(DN_WIDTH)] * 4 + [row(conf_w), row(LANES), per_batch(QKV_HALO, qkv_w),
                 per_batch(CONF_HALO, conf_w)])
    return pl.pallas_call(
        _mix_in_kernel,
        grid=(bsz, seq // tm),
        in_specs=[row(d)] + [_full_spec(c.shape) for c in consts],
        out_specs=out_specs,
        out_shape=out_shape,
        scratch_shapes=[pltpu.VMEM((QKV_HALO + tm, qkv_w), F32), pltpu.VMEM((CONF_HALO + tm, conf_w), F32)],
        compiler_params=pltpu.CompilerParams(dimension_semantics=("parallel", "arbitrary"),
                                             vmem_limit_bytes=VMEM_LIMIT),
        name="mix_in",
    )(x, *consts)


def _bmm(a, b):
    return jnp.einsum('nij,njk->nik', a, b, preferred_element_type=F32)


def _delta_kernel(q_ref, k_ref, v_ref, z_ref, bg_ref, s0_ref, gain_ref, o_ref, sfin_ref,
                  s_ref, u_s, wq_s, attn_s, kd_s, egl_s, *, chunks):
    j = pl.program_id(1)
    slot = j % 2
    prev = 1 - slot
    gain = gain_ref[...]

    @pl.when(j == 0)
    def _():
        s_ref[...] = s0_ref[...]
        u_s[1] = jnp.zeros(u_s.shape[1:], u_s.dtype)
        wq_s[1] = jnp.zeros(wq_s.shape[1:], wq_s.dtype)
        attn_s[1] = jnp.zeros(attn_s.shape[1:], attn_s.dtype)
        kd_s[1] = jnp.zeros(kd_s.shape[1:], kd_s.dtype)
        egl_s[1] = jnp.zeros(egl_s.shape[1:], egl_s.dtype)

    def recurrence():
        live = j > 0
        state = [s_ref[hd] for hd in range(DN_HEADS)]
        for c in range(chunks):
            idx = [hd * chunks + c for hd in range(DN_HEADS)]
            wq = [_dot(wq_s[prev, n], state[hd].astype(BF16)) for hd, n in enumerate(idx)]
            yield
            v_new = [(u_s[prev, n] - wq[hd][:CHUNK]).astype(BF16) for hd, n in enumerate(idx)]
            o = [wq[hd][CHUNK:] + _dot(attn_s[prev, n], v_new[hd]) for hd, n in enumerate(idx)]
            state = [state[hd] * egl_s[prev, n][0:1, :]
                     + lax.dot_general(kd_s[prev, n], v_new[hd], (((0,), (0,)), ((), ())),
                                       preferred_element_type=F32)
                     for hd, n in enumerate(idx)]
            yield
            for hd in range(DN_HEADS):
                cols = slice(hd * HEAD_DIM, (hd + 1) * HEAD_DIM)
                rows = slice(c * CHUNK, (c + 1) * CHUNK)
                r = o[hd] * lax.rsqrt(jnp.mean(o[hd] * o[hd], axis=-1, keepdims=True) + 1e-6)
                o_ref[rows, cols] = r * gain * _silu(z_ref[rows, cols])
        for hd in range(DN_HEADS):
            kept = jnp.where(live, state[hd], s_ref[hd])
            s_ref[hd] = kept
            sfin_ref[hd] = kept

    def preparation():
        yield from _delta_prepare(q_ref, k_ref, v_ref, bg_ref, u_s, wq_s, attn_s, kd_s, egl_s, slot, chunks)

    halves = [recurrence(), preparation()]
    while halves:
        for gen in list(halves):
            try:
                next(gen)
            except StopIteration:
                halves.remove(gen)


def _delta_prepare(q_ref, k_ref, v_ref, bg_ref, u_s, wq_s, attn_s, kd_s, egl_s, slot, chunks):
    ii = lax.broadcasted_iota(jnp.int32, (CHUNK, CHUNK), 0)
    jj = lax.broadcasted_iota(jnp.int32, (CHUNK, CHUNK), 1)
    causal = ii >= jj
    strict = ii > jj
    eye = (ii == jj).astype(F32)
    bg3 = bg_ref[...].reshape(chunks, CHUNK, LANES)
    tril_b = jnp.broadcast_to(causal.astype(BF16), (chunks, CHUNK, CHUNK))
    p1 = bg3.astype(BF16)
    r1 = bg3 - p1.astype(F32)
    p2 = r1.astype(BF16)
    p3 = (r1 - p2.astype(F32)).astype(BF16)
    gc3 = _bmm(tril_b, p1) + _bmm(tril_b, p2) + _bmm(tril_b, p3)
    yield

    def heads(ref):
        return jnp.concatenate([ref[:, hd * HEAD_DIM:(hd + 1) * HEAD_DIM].reshape(chunks, CHUNK, HEAD_DIM)
                                for hd in range(DN_HEADS)], axis=0)
    q = heads(q_ref)
    k = heads(k_ref)
    v = heads(v_ref)
    bet = jnp.concatenate([bg3[:, :, hd:hd + 1] for hd in range(DN_HEADS)], axis=0)
    gc = jnp.concatenate([gc3[:, :, DN_HEADS + hd:DN_HEADS + hd + 1] for hd in range(DN_HEADS)], axis=0)
    gc_t = [gc3[c].T for c in range(chunks)]
    decay = jnp.stack([
        jnp.exp(jnp.where(causal, gc3[c][:, DN_HEADS + hd:DN_HEADS + hd + 1]
                          - gc_t[c][DN_HEADS + hd:DN_HEADS + hd + 1, :], -jnp.inf))
        for hd in range(DN_HEADS) for c in range(chunks)], axis=0)

    kb = k * bet
    g_all = jnp.einsum('nid,njd->nij', jnp.concatenate([kb, q], axis=1).astype(BF16), k.astype(BF16),
                       preferred_element_type=F32)
    yield
    a_low = jnp.where(strict, g_all[:, :CHUNK] * decay, 0.0)
    attn = (g_all[:, CHUNK:] * decay).astype(BF16)

    l_mat = eye + a_low
    l_bf = l_mat.astype(BF16)
    t_mat = eye - a_low
    for _ in range(4):
        res = eye - _bmm(l_bf, t_mat.astype(BF16))
        yield
        t_mat = t_mat + _bmm(t_mat.astype(BF16), res.astype(BF16))
        yield
    l_lo = (l_mat - l_bf.astype(F32)).astype(BF16)
    t_hi = t_mat.astype(BF16)
    t_lo = (t_mat - t_hi.astype(F32)).astype(BF16)
    res = eye - (_bmm(l_bf, t_hi) + _bmm(l_lo, t_hi) + _bmm(l_bf, t_lo))
    yield
    t_mat = t_mat + _bmm(t_hi, res.astype(BF16))
    yield

    eg = jnp.exp(gc)
    uw = _bmm(t_mat.astype(BF16), jnp.concatenate([v * bet, kb * eg], axis=2).astype(BF16))
    yield
    u = uw[:, :, :HEAD_DIM]
    wq_lhs = jnp.concatenate([uw[:, :, HEAD_DIM:], q * eg], axis=1).astype(BF16)
    g_last = gc[:, CHUNK - 1:CHUNK, :]
    k_dec = (k * jnp.exp(g_last - gc)).astype(BF16)
    eg_last = jnp.exp(g_last)

    u_s[slot] = u
    wq_s[slot] = wq_lhs
    attn_s[slot] = attn
    kd_s[slot] = k_dec
    egl_s[slot] = jnp.broadcast_to(eg_last, egl_s.shape[1:])


def _delta(q, k, v, z, bg, s0, gain, chunks):
    bsz, seq, _ = q.shape
    rows = chunks * CHUNK
    assert seq % rows == 0

    nj = seq // rows
    nb = DN_HEADS * chunks

    def prep(width):
        return pl.BlockSpec((None, rows, width), lambda b, j: (b, jnp.minimum(j, nj - 1), 0))

    def scan(width):
        return pl.BlockSpec((None, rows, width), lambda b, j: (b, jnp.maximum(j - 1, 0), 0))

    state_shape = (DN_HEADS, HEAD_DIM, HEAD_DIM)
    return pl.pallas_call(
        functools.partial(_delta_kernel, chunks=chunks),
        grid=(bsz, nj + 1),
        in_specs=[prep(DN_WIDTH)] * 3 + [scan(DN_WIDTH), prep(LANES), _full_spec(state_shape),
                                         _full_spec(gain.shape)],
        out_specs=[scan(DN_WIDTH), pl.BlockSpec((None,) + state_shape, lambda b, j: (b, 0, 0, 0))],
        out_shape=[jax.ShapeDtypeStruct((bsz, seq, DN_WIDTH), F32),
                   jax.ShapeDtypeStruct((bsz,) + state_shape, F32)],
        scratch_shapes=[pltpu.VMEM(state_shape, F32),
                        pltpu.VMEM((2, nb, CHUNK, HEAD_DIM), F32),
                        pltpu.VMEM((2, nb, 2 * CHUNK, HEAD_DIM), BF16),
                        pltpu.VMEM((2, nb, CHUNK, CHUNK), BF16),
                        pltpu.VMEM((2, nb, CHUNK, HEAD_DIM), BF16),
                        pltpu.VMEM((2, nb, SUBLANES, HEAD_DIM), F32)],
        compiler_params=pltpu.CompilerParams(dimension_semantics=("parallel", "arbitrary"),
                                             vmem_limit_bytes=VMEM_LIMIT),
        name="delta",
    )(q, k, v, z, bg, s0, gain)


def _mix_out_kernel(x_ref, o_ref, c_ref, lng_ref, lnb_ref, wo_ref, g1_ref, b1_ref, wrh_ref, wrl_ref, br_ref,
                    dww_ref, dwb_ref, cvg_ref, cvb_ref, hc_in_ref,
                    h1s_ref, route_ref, route_t_ref, cnt_out_ref, cnt_ref, c_ext, *, alpha, tiles_per_seq):
    tm = x_ref.shape[0]

    @pl.when(pl.program_id(0) == 0)
    def _():
        cnt_ref[...] = jnp.zeros_like(cnt_ref)

    @pl.when(pl.program_id(0) % tiles_per_seq == 0)
    def _():
        c_ext[0:CONF_HALO, :] = hc_in_ref[...]

    c_ext[CONF_HALO:CONF_HALO + tm, :] = c_ref[...]
    conv = _causal_depthwise(c_ext, dww_ref, CONF_KERNEL, CONF_HALO, tm) + dwb_ref[...]
    conf = _silu(_layer_norm(conv, cvg_ref[...], cvb_ref[...]))
    c_ext[0:CONF_HALO, :] = c_ext[tm:tm + CONF_HALO, :]

    h = _layer_norm(x_ref[...], lng_ref[...], lnb_ref[...])
    dn = o_ref.shape[1]
    mix = _dot(o_ref[...].astype(BF16), wo_ref[0:dn, :]) + _dot(conf.astype(BF16), wo_ref[dn:, :])
    h1 = _layer_norm(alpha * h + mix, g1_ref[...], b1_ref[...])
    _store_slabs(h1s_ref, h1)

    hh, hl = _split2(h1)
    logits = _dot(hh, wrh_ref[...]) + _dot(hl, wrh_ref[...]) + _dot(hh, wrl_ref[...]) + br_ref[...]
    lane = lax.broadcasted_iota(jnp.int32, logits.shape, 1).astype(F32)
    big = float(LANES)
    neg = -jnp.inf

    def first_argmax(vals):
        top = jnp.max(vals, axis=-1, keepdims=True)
        return top, jnp.min(jnp.where(vals == top, lane, big), axis=-1, keepdims=True)

    grp = jnp.where(lane < N_GROUPS, logits, neg)
    g_top, g_sel = first_argmax(grp)
    p_group = 1.0 / jnp.sum(jnp.exp(grp - g_top), axis=-1, keepdims=True)
    lo = N_GROUPS + EXPERTS_PER_GROUP * g_sel
    in_grp = jnp.where((lane >= lo) & (lane < lo + EXPERTS_PER_GROUP), logits, neg)
    m1, i1 = first_argmax(in_grp)
    m2, i2 = first_argmax(jnp.where(lane == i1, neg, in_grp))
    s = jnp.exp(m2 - m1)
    w1 = p_group / (1.0 + s)
    w2 = p_group * s / (1.0 + s)
    e1 = i1 - N_GROUPS
    e2 = i2 - N_GROUPS

    tm = logits.shape[0]
    oh1 = (lane == e1).astype(F32)
    oh2 = (lane == e2).astype(F32)
    both = oh1 + oh2
    ti = lax.broadcasted_iota(jnp.int32, (tm, tm), 0)
    tj = lax.broadcasted_iota(jnp.int32, (tm, tm), 1)
    base = _dot((ti > tj).astype(BF16), both.astype(BF16)) + cnt_ref[...]
    r1 = jnp.sum(oh1 * base, axis=-1, keepdims=True)
    r2 = jnp.sum(oh2 * base, axis=-1, keepdims=True)
    cnt_ref[...] = cnt_ref[...] + jnp.sum(both, axis=0, keepdims=True)
    cnt_out_ref[...] = jnp.broadcast_to(cnt_ref[...], cnt_out_ref.shape)

    vals = (e1, e2, w1, w2, r1, r2)
    route = jnp.zeros_like(logits)
    for idx, val in enumerate(vals):
        route = jnp.where(lane == idx, val, route)
    route_ref[...] = route
    route_t_ref[...] = route.T[0:SUBLANES, :]


def _mix_out(x2d, o2d, c2d, p, halo_c, seq, tm, alpha):
    n, d = x2d.shape
    assert seq % tm == 0
    slabs = d // LANES

    def row(width):
        return pl.BlockSpec((tm, width), lambda i: (i, 0))

    consts = [p['ln_emb_g'], p['ln_emb_b'], p['w_out'], p['ln1_g'], p['ln1_b'], p['w_r_hi'], p['w_r_lo'], p['b_r'],
              p['dw_w'], p['dw_b'], p['cv_g'], p['cv_b'], halo_c]
    return pl.pallas_call(
        functools.partial(_mix_out_kernel, alpha=alpha, tiles_per_seq=seq // tm),
        grid=(n // tm,),
        in_specs=[row(d), row(o2d.shape[1]), row(c2d.shape[1])] + [_full_spec(c.shape) for c in consts],
        out_specs=[pl.BlockSpec((tm * slabs, LANES), lambda i: (i, 0)), row(LANES),
                   pl.BlockSpec((SUBLANES, tm), lambda i: (0, i)), _full_spec((SUBLANES, LANES))],
        out_shape=[jax.ShapeDtypeStruct((n * slabs, LANES), F32), jax.ShapeDtypeStruct((n, LANES), F32),
                   jax.ShapeDtypeStruct((SUBLANES, n), F32), jax.ShapeDtypeStruct((SUBLANES, LANES), F32)],
        scratch_shapes=[pltpu.VMEM((1, LANES), F32), pltpu.VMEM((CONF_HALO + tm, c2d.shape[1]), F32)],
        compiler_params=pltpu.CompilerParams(dimension_semantics=("arbitrary",), vmem_limit_bytes=VMEM_LIMIT),
        name="mix_out",
    )(x2d, o2d, c2d, *consts)


def _dispatch_kernel(dest_ref, pad_lo_ref, pad_hi_ref, h1s_hbm, xs_hbm, ring, zslab, fsem, ssem, zsem, *,
                     td, n_slabs, n_tokens):
    i = pl.program_id(0)
    nb = pl.num_programs(0)
    slot = i % RING
    tile_rows = td * n_slabs

    def fetch(step):
        start = pl.multiple_of(step * tile_rows, tile_rows)
        return pltpu.make_async_copy(h1s_hbm.at[pl.ds(start, tile_rows), :], ring.at[step % RING],
                                     fsem.at[step % RING])

    def wait_scatter(step):
        for _ in range(TOP_K):
            pltpu.make_async_copy(ring.at[step % RING], xs_hbm.at[pl.ds(0, tile_rows), :],
                                  ssem.at[step % RING]).wait()

    def pad_copy(row):
        return pltpu.make_async_copy(zslab, xs_hbm.at[pl.ds(pl.multiple_of(row * n_slabs, n_slabs), n_slabs), :],
                                     zsem.at[0])

    def for_each_pad_row(fn):
        def per_expert(e, carry):
            def per_row(row, c2):
                fn(row)
                return c2
            return lax.fori_loop(pad_lo_ref[e], pad_hi_ref[e], per_row, carry)
        lax.fori_loop(0, pad_lo_ref.shape[0], per_expert, 0)

    @pl.when(i == 0)
    def _():
        fetch(0).start()
        zslab[...] = jnp.zeros_like(zslab)
        for_each_pad_row(lambda row: pad_copy(row).start())

    @pl.when((i == 0) & (nb > 1))
    def _():
        fetch(1).start()

    fetch(i).wait()

    def issue_body(r, carry):
        src = ring.at[slot, pl.ds(pl.multiple_of(r * n_slabs, n_slabs), n_slabs), :]
        for k in range(TOP_K):
            dst_row = pl.multiple_of(dest_ref[k * n_tokens + i * td + r], n_slabs)
            pltpu.make_async_copy(src, xs_hbm.at[pl.ds(dst_row, n_slabs), :], ssem.at[slot]).start(priority=k)
        return carry
    lax.fori_loop(0, td, issue_body, 0, unroll=DMA_UNROLL)

    @pl.when(i > 0)
    def _():
        wait_scatter(i - 1)

    @pl.when(i + 2 < nb)
    def _():
        fetch(i + 2).start()

    @pl.when(i == nb - 1)
    def _():
        wait_scatter(i)
        for_each_pad_row(lambda row: pad_copy(row).wait())


def _dispatch(dest_rows, pad_lo, pad_hi, h1s, cap_rows, td, n_slabs):
    n = h1s.shape[0] // n_slabs
    assert n % td == 0
    grid_spec = pltpu.PrefetchScalarGridSpec(
        num_scalar_prefetch=3,
        grid=(n // td,),
        in_specs=[pl.BlockSpec(memory_space=pl.ANY)],
        out_specs=pl.BlockSpec(memory_space=pl.ANY),
        scratch_shapes=[pltpu.VMEM((RING, td * n_slabs, LANES), F32), pltpu.VMEM((n_slabs, LANES), F32),
                        pltpu.SemaphoreType.DMA((RING,)), pltpu.SemaphoreType.DMA((RING,)),
                        pltpu.SemaphoreType.DMA((1,))],
    )
    return pl.pallas_call(
        functools.partial(_dispatch_kernel, td=td, n_slabs=n_slabs, n_tokens=n),
        grid_spec=grid_spec,
        out_shape=jax.ShapeDtypeStruct((cap_rows * n_slabs, LANES), F32),
        compiler_params=pltpu.CompilerParams(dimension_semantics=("arbitrary",), disable_bounds_checks=True),
        name="dispatch",
    )(dest_rows, pad_lo, pad_hi, h1s)


def _expert_kernel(be_ref, nu_ref, xs_ref, wg_ref, wu_ref, wd_ref, y_ref, wg_bf, wu_bf, wd_bf, *, bm, n_slabs):
    i = pl.program_id(0)
    used = i < nu_ref[0]

    @pl.when(used & ((i == 0) | (be_ref[i] != be_ref[jnp.maximum(i - 1, 0)])))
    def _():
        wg_bf[...] = wg_ref[...].astype(BF16)
        wu_bf[...] = wu_ref[...].astype(BF16)
        wd_bf[...] = wd_ref[...].astype(BF16)

    @pl.when(used)
    def _():
        xb = _load_slabs(xs_ref, bm, n_slabs).astype(BF16)
        hid = _silu(_dot(xb, wg_bf[...])) * _dot(xb, wu_bf[...])
        _store_slabs(y_ref, _dot(hid.astype(BF16), wd_bf[...]))

    @pl.when(jnp.logical_not(used))
    def _():
        y_ref[...] = jnp.zeros_like(y_ref)


def _experts(block_expert, n_used, xs, w_gate, w_up, w_down, bm):
    n_blocks = block_expert.shape[0]
    d = w_gate.shape[1]
    ff = w_gate.shape[2]
    n_slabs = d // LANES

    def blk(i, be, nu):
        return jnp.minimum(i, nu[0] - 1)

    grid_spec = pltpu.PrefetchScalarGridSpec(
        num_scalar_prefetch=2,
        grid=(n_blocks,),
        in_specs=[pl.BlockSpec((bm * n_slabs, LANES), lambda i, be, nu: (blk(i, be, nu), 0)),
                  pl.BlockSpec((None, d, ff), lambda i, be, nu: (be[blk(i, be, nu)], 0, 0)),
                  pl.BlockSpec((None, d, ff), lambda i, be, nu: (be[blk(i, be, nu)], 0, 0)),
                  pl.BlockSpec((None, ff, d), lambda i, be, nu: (be[blk(i, be, nu)], 0, 0))],
        out_specs=pl.BlockSpec((bm * n_slabs, LANES), lambda i, be, nu: (i, 0)),
        scratch_shapes=[pltpu.VMEM((d, ff), BF16), pltpu.VMEM((d, ff), BF16), pltpu.VMEM((ff, d), BF16)],
    )
    return pl.pallas_call(
        functools.partial(_expert_kernel, bm=bm, n_slabs=n_slabs),
        grid_spec=grid_spec,
        out_shape=jax.ShapeDtypeStruct(xs.shape, F32),
        compiler_params=pltpu.CompilerParams(dimension_semantics=("arbitrary",), vmem_limit_bytes=VMEM_LIMIT),
        name="experts",
    )(block_expert, n_used, xs, w_gate, w_up, w_down)


def _combine_kernel(dest_ref, y_hbm, h1s_ref, route_ref, g2_ref, b2_ref, out_ref, ybuf, sem, *, tm, n_slabs, alpha):
    i = pl.program_id(0)
    nb = pl.num_programs(0)
    part = tm * n_slabs

    def issue(blk, slot):
        def body(r, carry):
            for k in range(TOP_K):
                src_row = pl.multiple_of(dest_ref[k * (nb * tm) + blk * tm + r], n_slabs)
                dst_row = pl.multiple_of((slot * TOP_K + k) * part + r * n_slabs, n_slabs)
                pltpu.make_async_copy(y_hbm.at[pl.ds(src_row, n_slabs), :], ybuf.at[pl.ds(dst_row, n_slabs), :],
                                      sem.at[slot]).start(priority=k)
            return carry
        lax.fori_loop(0, tm, body, 0, unroll=DMA_UNROLL)

    @pl.when(i == 0)
    def _():
        issue(0, 0)

    @pl.when(i + 1 < nb)
    def _():
        issue(i + 1, (i + 1) % 2)

    slot = i % 2
    base = pl.multiple_of(slot * (TOP_K * part), TOP_K * part)
    pltpu.make_async_copy(y_hbm.at[pl.ds(0, TOP_K * part), :], ybuf.at[pl.ds(base, TOP_K * part), :],
                          sem.at[slot]).wait()

    route = route_ref[...]
    ffn = (_load_slabs(ybuf, tm, n_slabs, base) * route[:, 2:3]
           + _load_slabs(ybuf, tm, n_slabs, base + part) * route[:, 3:4])
    h1 = _load_slabs(h1s_ref, tm, n_slabs)
    out_ref[...] = _layer_norm(alpha * h1 + ffn, g2_ref[...], b2_ref[...])


def _combine(dest_rows, y_sorted, h1s, route, ln2_g, ln2_b, tm, alpha):
    d = ln2_g.shape[1]
    n_slabs = d // LANES
    n = h1s.shape[0] // n_slabs
    assert n % tm == 0
    grid_spec = pltpu.PrefetchScalarGridSpec(
        num_scalar_prefetch=1,
        grid=(n // tm,),
        in_specs=[pl.BlockSpec(memory_space=pl.ANY),
                  pl.BlockSpec((tm * n_slabs, LANES), lambda i, dest: (i, 0)),
                  pl.BlockSpec((tm, LANES), lambda i, dest: (i, 0)),
                  pl.BlockSpec((1, d), lambda i, dest: (0, 0)),
                  pl.BlockSpec((1, d), lambda i, dest: (0, 0))],
        out_specs=pl.BlockSpec((tm, d), lambda i, dest: (i, 0)),
        scratch_shapes=[pltpu.VMEM((2 * TOP_K * tm * n_slabs, LANES), F32), pltpu.SemaphoreType.DMA((2,))],
    )
    return pl.pallas_call(
        functools.partial(_combine_kernel, tm=tm, n_slabs=n_slabs, alpha=alpha),
        grid_spec=grid_spec,
        out_shape=jax.ShapeDtypeStruct((n, d), F32),
        compiler_params=pltpu.CompilerParams(dimension_semantics=("arbitrary",), vmem_limit_bytes=VMEM_LIMIT,
                                             disable_bounds_checks=True),
        name="combine",
    )(dest_rows, y_sorted, h1s, route, ln2_g, ln2_b)


def _dispatch_plan(route_t, counts, bm, n_slabs):
    n = route_t.shape[1]
    expert_id = route_t[0:TOP_K].astype(jnp.int32)
    rank = route_t[4:4 + TOP_K].astype(jnp.int32)
    padded = (counts + bm - 1) // bm * bm
    pad_end = jnp.cumsum(padded)
    pad_start = pad_end - padded
    experts = jnp.arange(N_EXPERTS, dtype=jnp.int32)[:, None, None]
    dest = jnp.sum(jnp.where(expert_id[None] == experts, pad_start[:, None, None], 0), axis=0) + rank
    n_blocks = (n * TOP_K + bm - 1) // bm + N_EXPERTS
    block_start = jnp.arange(n_blocks, dtype=jnp.int32) * bm
    block_expert = jnp.minimum(jnp.sum((block_start[:, None] >= pad_end[None, :]).astype(jnp.int32), axis=1),
                               N_EXPERTS - 1).astype(jnp.int32)
    n_used = (pad_end[-1:] // bm).astype(jnp.int32)
    dest_rows = (dest * n_slabs).reshape(-1).astype(jnp.int32)
    pad_lo = jnp.concatenate([pad_start + counts, pad_end[-1:]]).astype(jnp.int32)
    pad_hi = jnp.concatenate([pad_end, jnp.full((1,), n_blocks * bm, pad_end.dtype)]).astype(jnp.int32)
    return dest_rows, pad_lo, pad_hi, block_expert, n_used, n_blocks


def _pad_lanes(w, width=LANES):
    return jnp.pad(w, [(0, 0)] * (w.ndim - 1) + [(0, width - w.shape[-1])])


def kernel(x, meta_tokens, ln_emb_g, ln_emb_b, w_in, conv_qkv_w, a_log, dt_bias, dn_norm_g, conv_dw_w, conv_dw_b, cv_norm_g, cv_norm_b, w_out, ln1_g, ln1_b, w_group, b_group, w_router, b_router, w_exp_gate, w_exp_up, w_exp_down, ln2_g, ln2_b):
    depth = w_in.shape[0]
    assert depth == 1, "single-layer block"
    bsz, seq, d = x.shape
    alpha = (2.0 * depth) ** 0.25
    qkv_w = 3 * DN_WIDTH
    w_in0 = w_in[0]
    glu_off = 4 * DN_WIDTH + 2 * DN_HEADS
    row = lambda a: a.reshape(1, -1).astype(F32)
    p = {
        'ln_emb_g': row(ln_emb_g), 'ln_emb_b': row(ln_emb_b),
        'w_qkv': w_in0[:, :qkv_w].astype(BF16),
        'w_z': w_in0[:, qkv_w:4 * DN_WIDTH].astype(BF16),
        'w_ba': _pad_lanes(w_in0[:, 4 * DN_WIDTH:glu_off]).astype(BF16),
        'w_glu': w_in0[:, glu_off:].astype(BF16),
        'conv_w': conv_qkv_w[0].astype(F32),
        'neg_a': _pad_lanes(jnp.concatenate([jnp.zeros((DN_HEADS,), F32), -jnp.exp(a_log[0].astype(F32))])[None]),
        'dt_b': _pad_lanes(jnp.concatenate([jnp.zeros((DN_HEADS,), F32), dt_bias[0].astype(F32)])[None]),
        'dw_w': conv_dw_w[0].astype(F32), 'dw_b': row(conv_dw_b[0]),
        'cv_g': row(cv_norm_g[0]), 'cv_b': row(cv_norm_b[0]),
        'w_out': w_out[0].astype(BF16), 'ln1_g': row(ln1_g[0]), 'ln1_b': row(ln1_b[0]),
    }
    w_r = _pad_lanes(jnp.concatenate([w_group[0], w_router[0]], axis=1).astype(F32))
    p['w_r_hi'] = w_r.astype(BF16)
    p['w_r_lo'] = (w_r - p['w_r_hi'].astype(F32)).astype(BF16)
    p['b_r'] = _pad_lanes(jnp.concatenate([b_group[0], b_router[0]])[None].astype(F32))
    gain = row(dn_norm_g[0])

    conf_w = p['dw_w'].shape[1]
    zero_hq = jnp.zeros((QKV_HALO, qkv_w), F32)
    zero_hc = jnp.zeros((CONF_HALO, conf_w), F32)
    mq, mk, mv, mz, _, mbg, halo_q, halo_c = _mix_in(meta_tokens[None].astype(F32), p, zero_hq, zero_hc, N_META)
    front = lambda a: jnp.pad(a, [(0, 0), (CHUNK - N_META, 0), (0, 0)])
    s_zero = jnp.zeros((DN_HEADS, HEAD_DIM, HEAD_DIM), F32)
    _, s_meta = _delta(front(mq), front(mk), front(mv), front(mz), front(mbg), s_zero, gain, 1)

    q, k, v, z, c, bg, _, _ = _mix_in(x, p, halo_q[0], halo_c[0], TM_IN)
    o, _ = _delta(q, k, v, z, bg, s_meta[0], gain, DELTA_CHUNKS)

    n = bsz * seq
    h1s, route, route_t, cnt = _mix_out(x.reshape(n, d), o.reshape(n, DN_WIDTH), c.reshape(n, conf_w), p,
                                        halo_c[0], seq, TM_OUT, alpha)

    n_slabs = d // LANES
    counts = cnt[0, :N_EXPERTS].astype(jnp.int32)
    dest_rows, pad_lo, pad_hi, block_expert, n_used, n_blocks = _dispatch_plan(route_t, counts, BM_EXPERT, n_slabs)
    xs = _dispatch(dest_rows, pad_lo, pad_hi, h1s, n_blocks * BM_EXPERT, TD_DISPATCH, n_slabs)
    first_layer = lambda w: w.reshape(w.shape[1:])
    y_sorted = _experts(block_expert, n_used, xs, first_layer(w_exp_gate), first_layer(w_exp_up),
                        first_layer(w_exp_down), BM_EXPERT)
    out = _combine(dest_rows, y_sorted, h1s, route, row(ln2_g[0]), row(ln2_b[0]), TM_COMBINE, alpha)
    return out.reshape(bsz, seq, d)
```

```python
import functools

import jax
import jax.numpy as jnp
from jax import lax
from jax.experimental import pallas as pl
from jax.experimental.pallas import tpu as pltpu

F32 = jnp.float32
BF16 = jnp.bfloat16

NORM_EPS = 1e-5
N_META = 16
DN_HEADS = 4
HEAD_DIM = 128
DN_WIDTH = DN_HEADS * HEAD_DIM
CHUNK = 64
SHORT_CONV = 4
CONF_KERNEL = 31
N_GROUPS = 4
EXPERTS_PER_GROUP = 8
N_EXPERTS = N_GROUPS * EXPERTS_PER_GROUP
TOP_K = 2
LANES = 128
SUBLANES = 8
QKV_HALO = 8
CONF_HALO = 32
VMEM_LIMIT = 56 * 1024 * 1024

TM_IN = 512
DELTA_CHUNKS = 8
TM_OUT = 512
BM_EXPERT = 512
TM_COMBINE = 512
TD_DISPATCH = 512
RING = 3
SUB_ROWS = 256
SUB_ROWS_OUT = 512
CONV_BLOCK_ROWS = 64
CONV_BLOCK_COLS = 512
DMA_UNROLL = 8


def _dot(a, b):
    return jnp.dot(a, b, preferred_element_type=F32)


def _split2(x):
    hi = x.astype(BF16)
    lo = (x - hi.astype(F32)).astype(BF16)
    return hi, lo


def _dot_hilo(a, b):
    ah, al = _split2(a)
    bh, bl = _split2(b)
    return _dot(ah, bh) + _dot(al, bh) + _dot(ah, bl)


def _dot_exact01(m01, x):
    x1 = x.astype(BF16)
    r1 = x - x1.astype(F32)
    x2 = r1.astype(BF16)
    x3 = (r1 - x2.astype(F32)).astype(BF16)
    return _dot(m01, x1) + _dot(m01, x2) + _dot(m01, x3)


def _sigmoid(x):
    return 1.0 / (1.0 + jnp.exp(-x))


def _silu(x):
    return x * _sigmoid(x)


def _layer_norm(x, g, b):
    mu = jnp.mean(x, axis=-1, keepdims=True)
    xc = x - mu
    var = jnp.mean(xc * xc, axis=-1, keepdims=True)
    return xc * lax.rsqrt(var + NORM_EPS) * g + b


def _pack_bf16_pairs(x):
    half = x.shape[1] // 2
    lo = lax.bitcast_convert_type(x[:, :half].astype(BF16).astype(F32), jnp.uint32)
    hi = lax.bitcast_convert_type(x[:, half:].astype(BF16).astype(F32), jnp.uint32)
    return (lo >> 16) | (hi & jnp.uint32(0xFFFF0000))


def _unpack_bf16_pairs(p):
    lo = lax.bitcast_convert_type(p << 16, F32)
    hi = lax.bitcast_convert_type(p & jnp.uint32(0xFFFF0000), F32)
    return jnp.concatenate([lo, hi], axis=1)


def _store_slabs(ref, val):
    rows, d = val.shape
    n_slabs = d // LANES
    for s in range(n_slabs):
        ref[pl.ds(s, rows, stride=n_slabs), :] = val[:, s * LANES:(s + 1) * LANES]


def _load_slabs(ref, rows, n_slabs, base=0):
    return jnp.concatenate([ref[pl.ds(base + s, rows, stride=n_slabs), :] for s in range(n_slabs)], axis=1)


def _full_spec(shape):
    nd = len(shape)
    return pl.BlockSpec(shape, lambda *_: (0,) * nd)


def _causal_depthwise(ext_ref, w_ref, n_taps, halo, tm):
    ext = ext_ref[...]
    rows, cols = ext.shape
    first = halo - (n_taps - 1)
    groups = ext.reshape(rows // SUBLANES, SUBLANES, cols)
    sub = lax.broadcasted_iota(jnp.int32, (1, SUBLANES, 1), 1)

    def shifted(phase):
        rolled = pltpu.roll(groups, SUBLANES - phase, axis=1)
        nxt = jnp.concatenate([rolled[1:], rolled[:1]], axis=0)
        return jnp.where(sub < SUBLANES - phase, rolled, nxt).reshape(rows, cols)

    phases = {}
    for k in range(n_taps):
        phase = (first + k) % SUBLANES
        if phase not in phases:
            phases[phase] = ext if phase == 0 else shifted(phase)

    rb = min(tm, CONV_BLOCK_ROWS)
    cb = min(cols, CONV_BLOCK_COLS)
    out_rows = []
    for r0 in range(0, tm, rb):
        out_cols = []
        for c0 in range(0, cols, cb):
            acc = None
            for k in range(n_taps):
                phase = (first + k) % SUBLANES
                base = first + k - phase + r0
                term = phases[phase][base:base + rb, c0:c0 + cb] * w_ref[k:k + 1, c0:c0 + cb]
                acc = term if acc is None else acc + term
            out_cols.append(acc)
        out_rows.append(out_cols[0] if len(out_cols) == 1 else jnp.concatenate(out_cols, axis=1))
    return out_rows[0] if len(out_rows) == 1 else jnp.concatenate(out_rows, axis=0)


def _row_views(refs, r0, rows, lead=0):
    return [ref.at[pl.ds(r0, lead + rows), :] for ref in refs]


def _mix_in_kernel(x_ref, lng_ref, lnb_ref, wqkv_ref, wz_ref, wglu_ref, wba_ref, cw_ref, nega_ref,
                   dtb_ref, hq_in_ref, hc_in_ref,
                   q_ref, k_ref, v_ref, z_ref, c_ref, bg_ref, hq_out_ref, hc_out_ref, h_ref,
                   qkv_ext, c_ext):
    tm = x_ref.shape[0]

    @pl.when(pl.program_id(1) == 0)
    def _():
        qkv_ext[0:QKV_HALO, :] = hq_in_ref[...]
        c_ext[0:CONF_HALO, :] = hc_in_ref[...]

    th = min(tm, SUB_ROWS)
    for r0 in range(0, tm, th):
        x_v, q_v, k_v, v_v, z_v, c_v, bg_v, h_v = _row_views(
            (x_ref, q_ref, k_ref, v_ref, z_ref, c_ref, bg_ref, h_ref), r0, th)
        qkv_v, = _row_views((qkv_ext,), r0, th, QKV_HALO)
        cext_v, = _row_views((c_ext,), r0, th, CONF_HALO)
        _mix_in_rows(x_v, lng_ref, lnb_ref, wqkv_ref, wz_ref, wglu_ref, wba_ref, cw_ref, nega_ref, dtb_ref,
                     q_v, k_v, v_v, z_v, c_v, bg_v, h_v, qkv_v, cext_v)

    q_tail = qkv_ext[tm:tm + QKV_HALO, :]
    c_tail = c_ext[tm:tm + CONF_HALO, :]
    qkv_ext[0:QKV_HALO, :] = q_tail
    c_ext[0:CONF_HALO, :] = c_tail
    hq_out_ref[...] = q_tail
    hc_out_ref[...] = c_tail


def _mix_in_rows(x_ref, lng_ref, lnb_ref, wqkv_ref, wz_ref, wglu_ref, wba_ref, cw_ref, nega_ref, dtb_ref,
                 q_ref, k_ref, v_ref, z_ref, c_ref, bg_ref, h_ref, qkv_ext, c_ext):
    tm = x_ref.shape[0]
    h = _layer_norm(x_ref[...], lng_ref[...], lnb_ref[...])
    h_ref[...] = h
    hb = h.astype(BF16)

    qkv_ext[QKV_HALO:QKV_HALO + tm, :] = _dot(hb, wqkv_ref[...])
    qkv = _silu(_causal_depthwise(qkv_ext, cw_ref, SHORT_CONV, QKV_HALO, tm))
    for hd in range(DN_HEADS):
        lo = hd * HEAD_DIM
        qh = qkv[:, lo:lo + HEAD_DIM]
        kh = qkv[:, DN_WIDTH + lo:DN_WIDTH + lo + HEAD_DIM]
        q_ref[:, lo:lo + HEAD_DIM] = qh * (lax.rsqrt(jnp.sum(qh * qh, axis=-1, keepdims=True) + 1e-6)
                                           * (HEAD_DIM ** -0.5))
        k_ref[:, lo:lo + HEAD_DIM] = kh * lax.rsqrt(jnp.sum(kh * kh, axis=-1, keepdims=True) + 1e-6)
    v_ref[...] = qkv[:, 2 * DN_WIDTH:]
    z_ref[...] = _dot(hb, wz_ref[...])

    ba = _dot(hb, wba_ref[...])
    lane = lax.broadcasted_iota(jnp.int32, ba.shape, 1)
    sp_in = ba + dtb_ref[...]
    softplus = jnp.maximum(sp_in, 0.0) + jnp.log(1.0 + jnp.exp(-jnp.abs(sp_in)))
    bg_ref[...] = jnp.where(lane < DN_HEADS, _sigmoid(ba), nega_ref[...] * softplus)

    glu = _dot(hb, wglu_ref[...])
    cw = glu.shape[1] // 2
    c_pre = glu[:, :cw] * _sigmoid(glu[:, cw:])
    c_ref[...] = c_pre
    c_ext[CONF_HALO:CONF_HALO + tm, :] = c_pre


def _mix_in(x, p, halo_q, halo_c, tm):
    bsz, seq, d = x.shape
    assert seq % tm == 0
    qkv_w = 3 * DN_WIDTH
    conf_w = p['dw_w'].shape[1]

    def row(width):
        return pl.BlockSpec((None, tm, width), lambda b, t: (b, t, 0))

    def per_batch(rows, width):
        return pl.BlockSpec((None, rows, width), lambda b, t: (b, 0, 0))

    consts = [p['ln_emb_g'], p['ln_emb_b'], p['w_qkv'], p['w_z'], p['w_glu'], p['w_ba'], p['conv_w'],
              p['neg_a'], p['dt_b'], halo_q, halo_c]
    sds = jax.ShapeDtypeStruct
    out_shape = ([sds((bsz, seq, DN_WIDTH), F32)] * 4 + [sds((bsz, seq, conf_w), F32),
                 sds((bsz, seq, LANES), F32), sds((bsz, QKV_HALO, qkv_w), F32),
                 sds((bsz, CONF_HALO, conf_w), F32), sds((bsz, seq, d), F32)])
    out_specs = ([row(DN_WIDTH)] * 4 + [row(conf_w), row(LANES), per_batch(QKV_HALO, qkv_w),
                 per_batch(CONF_HALO, conf_w), row(d)])
    return pl.pallas_call(
        _mix_in_kernel,
        grid=(bsz, seq // tm),
        in_specs=[row(d)] + [_full_spec(c.shape) for c in consts],
        out_specs=out_specs,
        out_shape=out_shape,
        scratch_shapes=[pltpu.VMEM((QKV_HALO + tm, qkv_w), F32), pltpu.VMEM((CONF_HALO + tm, conf_w), F32)],
        compiler_params=pltpu.CompilerParams(dimension_semantics=("parallel", "arbitrary"),
                                             vmem_limit_bytes=VMEM_LIMIT),
        name="mix_in",
    )(x, *consts)


def _bmm(a, b):
    return jnp.einsum('nij,njk->nik', a, b, preferred_element_type=F32)


def _delta_kernel(q_ref, k_ref, v_ref, z_ref, bg_ref, s0_ref, gain_ref, o_ref, sfin_ref,
                  s_ref, u_s, wq_s, attn_s, kd_s, egl_s, *, chunks):
    j = pl.program_id(1)
    slot = j % 2
    prev = 1 - slot
    gain = gain_ref[...]

    @pl.when(j == 0)
    def _():
        s_ref[...] = s0_ref[...]
        u_s[1] = jnp.zeros(u_s.shape[1:], u_s.dtype)
        wq_s[1] = jnp.zeros(wq_s.shape[1:], wq_s.dtype)
        attn_s[1] = jnp.zeros(attn_s.shape[1:], attn_s.dtype)
        kd_s[1] = jnp.zeros(kd_s.shape[1:], kd_s.dtype)
        egl_s[1] = jnp.zeros(egl_s.shape[1:], egl_s.dtype)

    def recurrence():
        live = j > 0
        state = [s_ref[hd] for hd in range(DN_HEADS)]
        for c in range(chunks):
            idx = [hd * chunks + c for hd in range(DN_HEADS)]
            wq = [_dot(wq_s[prev, n], state[hd].astype(BF16)) for hd, n in enumerate(idx)]
            yield
            v_new = [(u_s[prev, n] - wq[hd][:CHUNK]).astype(BF16) for hd, n in enumerate(idx)]
            o = [wq[hd][CHUNK:] + _dot(attn_s[prev, n], v_new[hd]) for hd, n in enumerate(idx)]
            state = [state[hd] * egl_s[prev, n][0:1, :]
                     + lax.dot_general(kd_s[prev, n], v_new[hd], (((0,), (0,)), ((), ())),
                                       preferred_element_type=F32)
                     for hd, n in enumerate(idx)]
            yield
            for hd in range(DN_HEADS):
                cols = slice(hd * HEAD_DIM, (hd + 1) * HEAD_DIM)
                rows = slice(c * CHUNK, (c + 1) * CHUNK)
                r = o[hd] * lax.rsqrt(jnp.mean(o[hd] * o[hd], axis=-1, keepdims=True) + 1e-6)
                o_ref[rows, cols] = r * gain * _silu(z_ref[rows, cols])
        for hd in range(DN_HEADS):
            kept = jnp.where(live, state[hd], s_ref[hd])
            s_ref[hd] = kept
            sfin_ref[hd] = kept

    def preparation():
        yield from _delta_prepare(q_ref, k_ref, v_ref, bg_ref, u_s, wq_s, attn_s, kd_s, egl_s, slot, chunks)

    halves = [recurrence(), preparation()]
    while halves:
        for gen in list(halves):
            try:
                next(gen)
            except StopIteration:
                halves.remove(gen)


def _delta_prepare(q_ref, k_ref, v_ref, bg_ref, u_s, wq_s, attn_s, kd_s, egl_s, slot, chunks):
    ii = lax.broadcasted_iota(jnp.int32, (CHUNK, CHUNK), 0)
    jj = lax.broadcasted_iota(jnp.int32, (CHUNK, CHUNK), 1)
    causal = ii >= jj
    strict = ii > jj
    eye = (ii == jj).astype(F32)
    bg3 = bg_ref[...].reshape(chunks, CHUNK, LANES)
    tril_b = jnp.broadcast_to(causal.astype(BF16), (chunks, CHUNK, CHUNK))
    p1 = bg3.astype(BF16)
    r1 = bg3 - p1.astype(F32)
    p2 = r1.astype(BF16)
    p3 = (r1 - p2.astype(F32)).astype(BF16)
    gc3 = _bmm(tril_b, p1) + _bmm(tril_b, p2) + _bmm(tril_b, p3)
    yield

    def heads(ref):
        return jnp.concatenate([ref[:, hd * HEAD_DIM:(hd + 1) * HEAD_DIM].reshape(chunks, CHUNK, HEAD_DIM)
                                for hd in range(DN_HEADS)], axis=0)
    q = heads(q_ref)
    k = heads(k_ref)
    v = heads(v_ref)
    bet = jnp.concatenate([bg3[:, :, hd:hd + 1] for hd in range(DN_HEADS)], axis=0)
    gc = jnp.concatenate([gc3[:, :, DN_HEADS + hd:DN_HEADS + hd + 1] for hd in range(DN_HEADS)], axis=0)
    gc_t = [gc3[c].T for c in range(chunks)]
    decay = jnp.stack([
        jnp.exp(jnp.where(causal, gc3[c][:, DN_HEADS + hd:DN_HEADS + hd + 1]
                          - gc_t[c][DN_HEADS + hd:DN_HEADS + hd + 1, :], -jnp.inf))
        for hd in range(DN_HEADS) for c in range(chunks)], axis=0)

    kb = k * bet
    g_all = jnp.einsum('nid,njd->nij', jnp.concatenate([kb, q], axis=1).astype(BF16), k.astype(BF16),
                       preferred_element_type=F32)
    yield
    a_low = jnp.where(strict, g_all[:, :CHUNK] * decay, 0.0)
    attn = (g_all[:, CHUNK:] * decay).astype(BF16)

    l_mat = eye + a_low
    l_bf = l_mat.astype(BF16)
    t_mat = eye - a_low
    for _ in range(4):
        res = eye - _bmm(l_bf, t_mat.astype(BF16))
        yield
        t_mat = t_mat + _bmm(t_mat.astype(BF16), res.astype(BF16))
        yield
    l_lo = (l_mat - l_bf.astype(F32)).astype(BF16)
    t_hi = t_mat.astype(BF16)
    t_lo = (t_mat - t_hi.astype(F32)).astype(BF16)
    res = eye - (_bmm(l_bf, t_hi) + _bmm(l_lo, t_hi) + _bmm(l_bf, t_lo))
    yield
    t_mat = t_mat + _bmm(t_hi, res.astype(BF16))
    yield

    eg = jnp.exp(gc)
    uw = _bmm(t_mat.astype(BF16), jnp.concatenate([v * bet, kb * eg], axis=2).astype(BF16))
    yield
    u = uw[:, :, :HEAD_DIM]
    wq_lhs = jnp.concatenate([uw[:, :, HEAD_DIM:], q * eg], axis=1).astype(BF16)
    g_last = gc[:, CHUNK - 1:CHUNK, :]
    k_dec = (k * jnp.exp(g_last - gc)).astype(BF16)
    eg_last = jnp.exp(g_last)

    u_s[slot] = u
    wq_s[slot] = wq_lhs
    attn_s[slot] = attn
    kd_s[slot] = k_dec
    egl_s[slot] = jnp.broadcast_to(eg_last, egl_s.shape[1:])


def _delta(q, k, v, z, bg, s0, gain, chunks):
    bsz, seq, _ = q.shape
    rows = chunks * CHUNK
    assert seq % rows == 0

    nj = seq // rows
    nb = DN_HEADS * chunks

    def prep(width):
        return pl.BlockSpec((None, rows, width), lambda b, j: (b, jnp.minimum(j, nj - 1), 0))

    def scan(width):
        return pl.BlockSpec((None, rows, width), lambda b, j: (b, jnp.maximum(j - 1, 0), 0))

    state_shape = (DN_HEADS, HEAD_DIM, HEAD_DIM)
    return pl.pallas_call(
        functools.partial(_delta_kernel, chunks=chunks),
        grid=(bsz, nj + 1),
        in_specs=[prep(DN_WIDTH)] * 3 + [scan(DN_WIDTH), prep(LANES), _full_spec(state_shape),
                                         _full_spec(gain.shape)],
        out_specs=[scan(DN_WIDTH), pl.BlockSpec((None,) + state_shape, lambda b, j: (b, 0, 0, 0))],
        out_shape=[jax.ShapeDtypeStruct((bsz, seq, DN_WIDTH), F32),
                   jax.ShapeDtypeStruct((bsz,) + state_shape, F32)],
        scratch_shapes=[pltpu.VMEM(state_shape, F32),
                        pltpu.VMEM((2, nb, CHUNK, HEAD_DIM), F32),
                        pltpu.VMEM((2, nb, 2 * CHUNK, HEAD_DIM), BF16),
                        pltpu.VMEM((2, nb, CHUNK, CHUNK), BF16),
                        pltpu.VMEM((2, nb, CHUNK, HEAD_DIM), BF16),
                        pltpu.VMEM((2, nb, SUBLANES, HEAD_DIM), F32)],
        compiler_params=pltpu.CompilerParams(dimension_semantics=("parallel", "arbitrary"),
                                             vmem_limit_bytes=VMEM_LIMIT),
        name="delta",
    )(q, k, v, z, bg, s0, gain)


def _mix_out_kernel(h_ref, o_ref, c_ref, wo_ref, g1_ref, b1_ref, wrh_ref, wrl_ref, br_ref,
                    dww_ref, dwb_ref, cvg_ref, cvb_ref, hc_in_ref,
                    h1_ref, h1p_ref, route_ref, route_t_ref, cnt_out_ref, cnt_ref, c_ext, *, alpha, tiles_per_seq):
    tm = h_ref.shape[0]

    @pl.when(pl.program_id(0) == 0)
    def _():
        cnt_ref[...] = jnp.zeros_like(cnt_ref)

    @pl.when(pl.program_id(0) % tiles_per_seq == 0)
    def _():
        c_ext[0:CONF_HALO, :] = hc_in_ref[...]

    th = min(tm, SUB_ROWS_OUT)
    n_slabs = h1p_ref.shape[0] // tm
    for r0 in range(0, tm, th):
        h_v, o_v, c_v, h1_v, route_v = _row_views((h_ref, o_ref, c_ref, h1_ref, route_ref), r0, th)
        cext_v, = _row_views((c_ext,), r0, th, CONF_HALO)
        h1p_v, = _row_views((h1p_ref,), r0 * n_slabs, th * n_slabs)
        _mix_out_rows(h_v, o_v, c_v, wo_ref, g1_ref, b1_ref, wrh_ref, wrl_ref, br_ref, dww_ref, dwb_ref, cvg_ref,
                      cvb_ref, h1_v, h1p_v, route_v, route_t_ref.at[:, pl.ds(r0, th)], cnt_ref, cext_v, alpha)
    c_ext[0:CONF_HALO, :] = c_ext[tm:tm + CONF_HALO, :]
    cnt_out_ref[...] = jnp.broadcast_to(cnt_ref[...], cnt_out_ref.shape)


def _mix_out_rows(h_ref, o_ref, c_ref, wo_ref, g1_ref, b1_ref, wrh_ref, wrl_ref, br_ref, dww_ref, dwb_ref, cvg_ref,
                  cvb_ref, h1_ref, h1p_ref, route_ref, route_t_ref, cnt_ref, c_ext, alpha):
    tm = h_ref.shape[0]

    c_ext[CONF_HALO:CONF_HALO + tm, :] = c_ref[...]
    conv = _causal_depthwise(c_ext, dww_ref, CONF_KERNEL, CONF_HALO, tm) + dwb_ref[...]
    conf = _silu(_layer_norm(conv, cvg_ref[...], cvb_ref[...]))

    h = h_ref[...]
    dn = o_ref.shape[1]
    mix = _dot(o_ref[...].astype(BF16), wo_ref[0:dn, :]) + _dot(conf.astype(BF16), wo_ref[dn:, :])
    h1 = _layer_norm(alpha * h + mix, g1_ref[...], b1_ref[...])
    h1_ref[...] = h1
    _store_slabs(h1p_ref, _pack_bf16_pairs(h1))

    hh, hl = _split2(h1)
    logits = _dot(hh, wrh_ref[...]) + _dot(hl, wrh_ref[...]) + _dot(hh, wrl_ref[...]) + br_ref[...]
    lane = lax.broadcasted_iota(jnp.int32, logits.shape, 1).astype(F32)
    big = float(LANES)
    neg = -jnp.inf

    def first_argmax(vals):
        top = jnp.max(vals, axis=-1, keepdims=True)
        return top, jnp.min(jnp.where(vals == top, lane, big), axis=-1, keepdims=True)

    grp = jnp.where(lane < N_GROUPS, logits, neg)
    g_top, g_sel = first_argmax(grp)
    p_group = 1.0 / jnp.sum(jnp.exp(grp - g_top), axis=-1, keepdims=True)
    lo = N_GROUPS + EXPERTS_PER_GROUP * g_sel
    in_grp = jnp.where((lane >= lo) & (lane < lo + EXPERTS_PER_GROUP), logits, neg)
    m1, i1 = first_argmax(in_grp)
    m2, i2 = first_argmax(jnp.where(lane == i1, neg, in_grp))
    s = jnp.exp(m2 - m1)
    w1 = p_group / (1.0 + s)
    w2 = p_group * s / (1.0 + s)
    e1 = i1 - N_GROUPS
    e2 = i2 - N_GROUPS

    tm = logits.shape[0]
    oh1 = (lane == e1).astype(F32)
    oh2 = (lane == e2).astype(F32)
    both = oh1 + oh2
    ti = lax.broadcasted_iota(jnp.int32, (tm, tm), 0)
    tj = lax.broadcasted_iota(jnp.int32, (tm, tm), 1)
    base = _dot((ti > tj).astype(BF16), both.astype(BF16)) + cnt_ref[...]
    r1 = jnp.sum(oh1 * base, axis=-1, keepdims=True)
    r2 = jnp.sum(oh2 * base, axis=-1, keepdims=True)
    cnt_ref[...] = cnt_ref[...] + jnp.sum(both, axis=0, keepdims=True)

    vals = (e1, e2, w1, w2, r1, r2)
    route = jnp.zeros_like(logits)
    for idx, val in enumerate(vals):
        route = jnp.where(lane == idx, val, route)
    route_ref[...] = route
    route_t_ref[...] = route.T[0:SUBLANES, :]


def _mix_out(h2d, o2d, c2d, p, halo_c, seq, tm, alpha):
    n, d = h2d.shape
    assert seq % tm == 0
    slabs = d // (2 * LANES)

    def row(width):
        return pl.BlockSpec((tm, width), lambda i: (i, 0))

    consts = [p['w_out'], p['ln1_g'], p['ln1_b'], p['w_r_hi'], p['w_r_lo'], p['b_r'],
              p['dw_w'], p['dw_b'], p['cv_g'], p['cv_b'], halo_c]
    return pl.pallas_call(
        functools.partial(_mix_out_kernel, alpha=alpha, tiles_per_seq=seq // tm),
        grid=(n // tm,),
        in_specs=[row(d), row(o2d.shape[1]), row(c2d.shape[1])] + [_full_spec(c.shape) for c in consts],
        out_specs=[row(d), pl.BlockSpec((tm * slabs, LANES), lambda i: (i, 0)), row(LANES),
                   pl.BlockSpec((SUBLANES, tm), lambda i: (0, i)), _full_spec((SUBLANES, LANES))],
        out_shape=[jax.ShapeDtypeStruct((n, d), F32), jax.ShapeDtypeStruct((n * slabs, LANES), jnp.uint32),
                   jax.ShapeDtypeStruct((n, LANES), F32),
                   jax.ShapeDtypeStruct((SUBLANES, n), F32), jax.ShapeDtypeStruct((SUBLANES, LANES), F32)],
        scratch_shapes=[pltpu.VMEM((1, LANES), F32), pltpu.VMEM((CONF_HALO + tm, c2d.shape[1]), F32)],
        compiler_params=pltpu.CompilerParams(dimension_semantics=("arbitrary",), vmem_limit_bytes=VMEM_LIMIT),
        name="mix_out",
    )(h2d, o2d, c2d, *consts)


def _dispatch_kernel(dest_ref, pad_lo_ref, pad_hi_ref, h1s_hbm, xs_hbm, ring, zslab, fsem, ssem, zsem, *,
                     td, n_slabs, n_tokens):
    i = pl.program_id(0)
    nb = pl.num_programs(0)
    slot = i % RING
    tile_rows = td * n_slabs

    def fetch(step):
        start = pl.multiple_of(step * tile_rows, tile_rows)
        return pltpu.make_async_copy(h1s_hbm.at[pl.ds(start, tile_rows), :], ring.at[step % RING],
                                     fsem.at[step % RING])

    def wait_scatter(step):
        for _ in range(TOP_K):
            pltpu.make_async_copy(ring.at[step % RING], xs_hbm.at[pl.ds(0, tile_rows), :],
                                  ssem.at[step % RING]).wait()

    zrows = zslab.shape[0] // n_slabs

    def pad_copy(row, size):
        return pltpu.make_async_copy(
            zslab.at[pl.ds(0, size * n_slabs), :],
            xs_hbm.at[pl.ds(pl.multiple_of(row * n_slabs, n_slabs), size * n_slabs), :], zsem.at[0])

    def for_each_pad_copy(fn):
        def per_expert(e, carry):
            lo = pad_lo_ref[e]
            length = pad_hi_ref[e] - lo
            for b in range(zrows.bit_length()):
                size = 1 << b

                @pl.when(((length >> b) & 1) == 1)
                def _():
                    fn(pad_copy(lo + (length & (size - 1)), size))
            return carry
        lax.fori_loop(0, N_EXPERTS, per_expert, 0)
        tail_lo = pad_lo_ref[N_EXPERTS]

        def per_piece(piece, carry):
            fn(pad_copy(tail_lo + piece * zrows, zrows))
            return carry
        lax.fori_loop(0, (pad_hi_ref[N_EXPERTS] - tail_lo) // zrows, per_piece, 0)

    @pl.when(i == 0)
    def _():
        fetch(0).start()
        zslab[...] = jnp.zeros_like(zslab)
        for_each_pad_copy(lambda cp: cp.start())

    @pl.when((i == 0) & (nb > 1))
    def _():
        fetch(1).start()

    fetch(i).wait()

    def issue_body(r, carry):
        src = ring.at[slot, pl.ds(pl.multiple_of(r * n_slabs, n_slabs), n_slabs), :]
        for k in range(TOP_K):
            dst_row = pl.multiple_of(dest_ref[k * n_tokens + i * td + r], n_slabs)
            pltpu.make_async_copy(src, xs_hbm.at[pl.ds(dst_row, n_slabs), :], ssem.at[slot]).start(priority=k)
        return carry
    lax.fori_loop(0, td, issue_body, 0, unroll=DMA_UNROLL)

    @pl.when(i > 0)
    def _():
        wait_scatter(i - 1)

    @pl.when(i + 2 < nb)
    def _():
        fetch(i + 2).start()

    @pl.when(i == nb - 1)
    def _():
        wait_scatter(i)
        for_each_pad_copy(lambda cp: cp.wait())


def _dispatch(dest_rows, pad_lo, pad_hi, h1s, cap_rows, td, n_slabs, bm):
    n = h1s.shape[0] // n_slabs
    assert n % td == 0
    grid_spec = pltpu.PrefetchScalarGridSpec(
        num_scalar_prefetch=3,
        grid=(n // td,),
        in_specs=[pl.BlockSpec(memory_space=pl.ANY)],
        out_specs=pl.BlockSpec(memory_space=pl.ANY),
        scratch_shapes=[pltpu.VMEM((RING, td * n_slabs, LANES), h1s.dtype),
                        pltpu.VMEM((bm // 2 * n_slabs, LANES), h1s.dtype),
                        pltpu.SemaphoreType.DMA((RING,)), pltpu.SemaphoreType.DMA((RING,)),
                        pltpu.SemaphoreType.DMA((1,))],
    )
    return pl.pallas_call(
        functools.partial(_dispatch_kernel, td=td, n_slabs=n_slabs, n_tokens=n),
        grid_spec=grid_spec,
        out_shape=jax.ShapeDtypeStruct((cap_rows * n_slabs, LANES), h1s.dtype),
        compiler_params=pltpu.CompilerParams(dimension_semantics=("arbitrary",), disable_bounds_checks=True),
        name="dispatch",
    )(dest_rows, pad_lo, pad_hi, h1s)


def _expert_kernel(be_ref, nu_ref, xs_ref, wg_ref, wu_ref, wd_ref, y_ref, wg_bf, wu_bf, wd_bf, *, bm, n_slabs):
    i = pl.program_id(0)
    used = i < nu_ref[0]

    @pl.when(used & ((i == 0) | (be_ref[i] != be_ref[jnp.maximum(i - 1, 0)])))
    def _():
        wg_bf[...] = wg_ref[...].astype(BF16)
        wu_bf[...] = wu_ref[...].astype(BF16)
        wd_bf[...] = wd_ref[...].astype(BF16)

    @pl.when(used)
    def _():
        xb = _unpack_bf16_pairs(_load_slabs(xs_ref, bm, n_slabs)).astype(BF16)
        hid = _silu(_dot(xb, wg_bf[...])) * _dot(xb, wu_bf[...])
        _store_slabs(y_ref, _pack_bf16_pairs(_dot(hid.astype(BF16), wd_bf[...])))

    @pl.when(jnp.logical_not(used))
    def _():
        y_ref[...] = jnp.zeros_like(y_ref)


def _experts(block_expert, n_used, xs, w_gate, w_up, w_down, bm):
    n_blocks = block_expert.shape[0]
    d = w_gate.shape[1]
    ff = w_gate.shape[2]
    n_slabs = d // (2 * LANES)

    def blk(i, be, nu):
        return jnp.minimum(i, nu[0] - 1)

    grid_spec = pltpu.PrefetchScalarGridSpec(
        num_scalar_prefetch=2,
        grid=(n_blocks,),
        in_specs=[pl.BlockSpec((bm * n_slabs, LANES), lambda i, be, nu: (blk(i, be, nu), 0)),
                  pl.BlockSpec((None, d, ff), lambda i, be, nu: (be[blk(i, be, nu)], 0, 0)),
                  pl.BlockSpec((None, d, ff), lambda i, be, nu: (be[blk(i, be, nu)], 0, 0)),
                  pl.BlockSpec((None, ff, d), lambda i, be, nu: (be[blk(i, be, nu)], 0, 0))],
        out_specs=pl.BlockSpec((bm * n_slabs, LANES), lambda i, be, nu: (i, 0)),
        scratch_shapes=[pltpu.VMEM((d, ff), BF16), pltpu.VMEM((d, ff), BF16), pltpu.VMEM((ff, d), BF16)],
    )
    return pl.pallas_call(
        functools.partial(_expert_kernel, bm=bm, n_slabs=n_slabs),
        grid_spec=grid_spec,
        out_shape=jax.ShapeDtypeStruct(xs.shape, xs.dtype),
        compiler_params=pltpu.CompilerParams(dimension_semantics=("arbitrary",), vmem_limit_bytes=VMEM_LIMIT),
        name="experts",
    )(block_expert, n_used, xs, w_gate, w_up, w_down)


def _combine_kernel(dest_ref, y_hbm, h1_ref, route_ref, g2_ref, b2_ref, out_ref, ybuf, sem, *, tm, n_slabs, alpha):
    i = pl.program_id(0)
    nb = pl.num_programs(0)
    part = tm * n_slabs

    def issue(blk, slot):
        def body(r, carry):
            for k in range(TOP_K):
                src_row = pl.multiple_of(dest_ref[k * (nb * tm) + blk * tm + r], n_slabs)
                dst_row = pl.multiple_of((slot * TOP_K + k) * part + r * n_slabs, n_slabs)
                pltpu.make_async_copy(y_hbm.at[pl.ds(src_row, n_slabs), :], ybuf.at[pl.ds(dst_row, n_slabs), :],
                                      sem.at[slot]).start(priority=k)
            return carry
        lax.fori_loop(0, tm, body, 0, unroll=DMA_UNROLL)

    @pl.when(i == 0)
    def _():
        issue(0, 0)

    @pl.when(i + 1 < nb)
    def _():
        issue(i + 1, (i + 1) % 2)

    slot = i % 2
    base = pl.multiple_of(slot * (TOP_K * part), TOP_K * part)
    pltpu.make_async_copy(y_hbm.at[pl.ds(0, TOP_K * part), :], ybuf.at[pl.ds(base, TOP_K * part), :],
                          sem.at[slot]).wait()

    route = route_ref[...]
    ffn = (_unpack_bf16_pairs(_load_slabs(ybuf, tm, n_slabs, base)) * route[:, 2:3]
           + _unpack_bf16_pairs(_load_slabs(ybuf, tm, n_slabs, base + part)) * route[:, 3:4])
    out_ref[...] = _layer_norm(alpha * h1_ref[...] + ffn, g2_ref[...], b2_ref[...])


def _combine(dest_rows, y_sorted, h1, route, ln2_g, ln2_b, tm, alpha):
    n, d = h1.shape
    n_slabs = d // (2 * LANES)
    assert n % tm == 0
    grid_spec = pltpu.PrefetchScalarGridSpec(
        num_scalar_prefetch=1,
        grid=(n // tm,),
        in_specs=[pl.BlockSpec(memory_space=pl.ANY),
                  pl.BlockSpec((tm, d), lambda i, dest: (i, 0)),
                  pl.BlockSpec((tm, LANES), lambda i, dest: (i, 0)),
                  pl.BlockSpec((1, d), lambda i, dest: (0, 0)),
                  pl.BlockSpec((1, d), lambda i, dest: (0, 0))],
        out_specs=pl.BlockSpec((tm, d), lambda i, dest: (i, 0)),
        scratch_shapes=[pltpu.VMEM((2 * TOP_K * tm * n_slabs, LANES), y_sorted.dtype),
                        pltpu.SemaphoreType.DMA((2,))],
    )
    return pl.pallas_call(
        functools.partial(_combine_kernel, tm=tm, n_slabs=n_slabs, alpha=alpha),
        grid_spec=grid_spec,
        out_shape=jax.ShapeDtypeStruct((n, d), F32),
        compiler_params=pltpu.CompilerParams(dimension_semantics=("arbitrary",), vmem_limit_bytes=VMEM_LIMIT,
                                             disable_bounds_checks=True),
        name="combine",
    )(dest_rows, y_sorted, h1, route, ln2_g, ln2_b)


def _dispatch_plan(route_t, counts, bm, n_slabs):
    n = route_t.shape[1]
    expert_id = route_t[0:TOP_K].astype(jnp.int32)
    rank = route_t[4:4 + TOP_K].astype(jnp.int32)
    padded = (counts + bm - 1) // bm * bm
    pad_end = jnp.cumsum(padded)
    pad_start = pad_end - padded
    experts = jnp.arange(N_EXPERTS, dtype=jnp.int32)[:, None, None]
    dest = jnp.sum(jnp.where(expert_id[None] == experts, pad_start[:, None, None], 0), axis=0) + rank
    n_blocks = (n * TOP_K + bm - 1) // bm + N_EXPERTS
    block_start = jnp.arange(n_blocks, dtype=jnp.int32) * bm
    block_expert = jnp.minimum(jnp.sum((block_start[:, None] >= pad_end[None, :]).astype(jnp.int32), axis=1),
                               N_EXPERTS - 1).astype(jnp.int32)
    n_used = (pad_end[-1:] // bm).astype(jnp.int32)
    dest_rows = (dest * n_slabs).reshape(-1).astype(jnp.int32)
    pad_lo = jnp.concatenate([pad_start + counts, pad_end[-1:]]).astype(jnp.int32)
    pad_hi = jnp.concatenate([pad_end, jnp.full((1,), n_blocks * bm, pad_end.dtype)]).astype(jnp.int32)
    return dest_rows, pad_lo, pad_hi, block_expert, n_used, n_blocks


def _pad_lanes(w, width=LANES):
    return jnp.pad(w, [(0, 0)] * (w.ndim - 1) + [(0, width - w.shape[-1])])


def kernel(x, meta_tokens, ln_emb_g, ln_emb_b, w_in, conv_qkv_w, a_log, dt_bias, dn_norm_g, conv_dw_w, conv_dw_b, cv_norm_g, cv_norm_b, w_out, ln1_g, ln1_b, w_group, b_group, w_router, b_router, w_exp_gate, w_exp_up, w_exp_down, ln2_g, ln2_b):
    depth = w_in.shape[0]
    assert depth == 1, "single-layer block"
    bsz, seq, d = x.shape
    alpha = (2.0 * depth) ** 0.25
    qkv_w = 3 * DN_WIDTH
    w_in0 = w_in[0]
    glu_off = 4 * DN_WIDTH + 2 * DN_HEADS
    row = lambda a: a.reshape(1, -1).astype(F32)
    p = {
        'ln_emb_g': row(ln_emb_g), 'ln_emb_b': row(ln_emb_b),
        'w_qkv': w_in0[:, :qkv_w].astype(BF16),
        'w_z': w_in0[:, qkv_w:4 * DN_WIDTH].astype(BF16),
        'w_ba': _pad_lanes(w_in0[:, 4 * DN_WIDTH:glu_off]).astype(BF16),
        'w_glu': w_in0[:, glu_off:].astype(BF16),
        'conv_w': conv_qkv_w[0].astype(F32),
        'neg_a': _pad_lanes(jnp.concatenate([jnp.zeros((DN_HEADS,), F32), -jnp.exp(a_log[0].astype(F32))])[None]),
        'dt_b': _pad_lanes(jnp.concatenate([jnp.zeros((DN_HEADS,), F32), dt_bias[0].astype(F32)])[None]),
        'dw_w': conv_dw_w[0].astype(F32), 'dw_b': row(conv_dw_b[0]),
        'cv_g': row(cv_norm_g[0]), 'cv_b': row(cv_norm_b[0]),
        'w_out': w_out[0].astype(BF16), 'ln1_g': row(ln1_g[0]), 'ln1_b': row(ln1_b[0]),
    }
    w_r = _pad_lanes(jnp.concatenate([w_group[0], w_router[0]], axis=1).astype(F32))
    p['w_r_hi'] = w_r.astype(BF16)
    p['w_r_lo'] = (w_r - p['w_r_hi'].astype(F32)).astype(BF16)
    p['b_r'] = _pad_lanes(jnp.concatenate([b_group[0], b_router[0]])[None].astype(F32))
    gain = row(dn_norm_g[0])

    conf_w = p['dw_w'].shape[1]
    zero_hq = jnp.zeros((QKV_HALO, qkv_w), F32)
    zero_hc = jnp.zeros((CONF_HALO, conf_w), F32)
    mq, mk, mv, mz, _, mbg, halo_q, halo_c, _ = _mix_in(meta_tokens[None].astype(F32), p, zero_hq, zero_hc, N_META)
    front = lambda a: jnp.pad(a, [(0, 0), (CHUNK - N_META, 0), (0, 0)])
    s_zero = jnp.zeros((DN_HEADS, HEAD_DIM, HEAD_DIM), F32)
    _, s_meta = _delta(front(mq), front(mk), front(mv), front(mz), front(mbg), s_zero, gain, 1)

    q, k, v, z, c, bg, _, _, h = _mix_in(x, p, halo_q[0], halo_c[0], TM_IN)
    o, _ = _delta(q, k, v, z, bg, s_meta[0], gain, DELTA_CHUNKS)

    n = bsz * seq
    h1, h1p, route, route_t, cnt = _mix_out(h.reshape(n, d), o.reshape(n, DN_WIDTH), c.reshape(n, conf_w), p,
                                            halo_c[0], seq, TM_OUT, alpha)

    n_slabs = d // (2 * LANES)
    counts = cnt[0, :N_EXPERTS].astype(jnp.int32)
    dest_rows, pad_lo, pad_hi, block_expert, n_used, n_blocks = _dispatch_plan(route_t, counts, BM_EXPERT, n_slabs)
    xs = _dispatch(dest_rows, pad_lo, pad_hi, h1p, n_blocks * BM_EXPERT, TD_DISPATCH, n_slabs, BM_EXPERT)
    first_layer = lambda w: w.reshape(w.shape[1:])
    y_sorted = _experts(block_expert, n_used, xs, first_layer(w_exp_gate), first_layer(w_exp_up),
                        first_layer(w_exp_down), BM_EXPERT)
    out = _combine(dest_rows, y_sorted, h1, route, row(ln2_g[0]), row(ln2_b[0]), TM_COMBINE, alpha)
    return out.reshape(bsz, seq, d)
```

```python
import functools

import jax
import jax.numpy as jnp
from jax import lax
from jax.experimental import pallas as pl
from jax.experimental.pallas import tpu as pltpu

F32 = jnp.float32
BF16 = jnp.bfloat16

NORM_EPS = 1e-5
N_META = 16
DN_HEADS = 4
HEAD_DIM = 128
DN_WIDTH = DN_HEADS * HEAD_DIM
CHUNK = 64
SHORT_CONV = 4
CONF_KERNEL = 31
N_GROUPS = 4
EXPERTS_PER_GROUP = 8
N_EXPERTS = N_GROUPS * EXPERTS_PER_GROUP
TOP_K = 2
LANES = 128
SUBLANES = 8
QKV_HALO = 8
CONF_HALO = 32
VMEM_LIMIT = 56 * 1024 * 1024

TM_IN = 512
DELTA_CHUNKS = 8
TM_OUT = 512
BM_EXPERT = 512
TM_COMBINE = 512
TD_DISPATCH = 512
RING = 3
SUB_ROWS = 256
SUB_ROWS_OUT = 512
CONV_BLOCK_ROWS = 64
CONV_BLOCK_COLS = 512
DMA_UNROLL = 8


def _dot(a, b):
    return jnp.dot(a, b, preferred_element_type=F32)


def _split2(x):
    hi = x.astype(BF16)
    lo = (x - hi.astype(F32)).astype(BF16)
    return hi, lo


def _dot_hilo(a, b):
    ah, al = _split2(a)
    bh, bl = _split2(b)
    return _dot(ah, bh) + _dot(al, bh) + _dot(ah, bl)


def _dot_exact01(m01, x):
    x1 = x.astype(BF16)
    r1 = x - x1.astype(F32)
    x2 = r1.astype(BF16)
    x3 = (r1 - x2.astype(F32)).astype(BF16)
    return _dot(m01, x1) + _dot(m01, x2) + _dot(m01, x3)


def _sigmoid(x):
    return 1.0 / (1.0 + jnp.exp(-x))


def _silu(x):
    return x * _sigmoid(x)


def _layer_norm(x, g, b):
    mu = jnp.mean(x, axis=-1, keepdims=True)
    xc = x - mu
    var = jnp.mean(xc * xc, axis=-1, keepdims=True)
    return xc * lax.rsqrt(var + NORM_EPS) * g + b


def _pack_bf16_pairs(x):
    half = x.shape[1] // 2
    lo = lax.bitcast_convert_type(x[:, :half].astype(BF16).astype(F32), jnp.uint32)
    hi = lax.bitcast_convert_type(x[:, half:].astype(BF16).astype(F32), jnp.uint32)
    return (lo >> 16) | (hi & jnp.uint32(0xFFFF0000))


def _unpack_bf16_pairs(p):
    lo = lax.bitcast_convert_type(p << 16, F32)
    hi = lax.bitcast_convert_type(p & jnp.uint32(0xFFFF0000), F32)
    return jnp.concatenate([lo, hi], axis=1)


def _store_slabs(ref, val):
    rows, d = val.shape
    n_slabs = d // LANES
    for s in range(n_slabs):
        ref[pl.ds(s, rows, stride=n_slabs), :] = val[:, s * LANES:(s + 1) * LANES]


def _load_slabs(ref, rows, n_slabs, base=0):
    return jnp.concatenate([ref[pl.ds(base + s, rows, stride=n_slabs), :] for s in range(n_slabs)], axis=1)


def _full_spec(shape):
    nd = len(shape)
    return pl.BlockSpec(shape, lambda *_: (0,) * nd)


def _causal_depthwise(ext_ref, w_ref, n_taps, halo, tm):
    ext = ext_ref[...]
    rows, cols = ext.shape
    first = halo - (n_taps - 1)
    groups = ext.reshape(rows // SUBLANES, SUBLANES, cols)
    sub = lax.broadcasted_iota(jnp.int32, (1, SUBLANES, 1), 1)

    def shifted(phase):
        rolled = pltpu.roll(groups, SUBLANES - phase, axis=1)
        nxt = jnp.concatenate([rolled[1:], rolled[:1]], axis=0)
        return jnp.where(sub < SUBLANES - phase, rolled, nxt).reshape(rows, cols)

    phases = {}
    for k in range(n_taps):
        phase = (first + k) % SUBLANES
        if phase not in phases:
            phases[phase] = ext if phase == 0 else shifted(phase)

    rb = min(tm, CONV_BLOCK_ROWS)
    cb = min(cols, CONV_BLOCK_COLS)
    out_rows = []
    for r0 in range(0, tm, rb):
        out_cols = []
        for c0 in range(0, cols, cb):
            acc = None
            for k in range(n_taps):
                phase = (first + k) % SUBLANES
                base = first + k - phase + r0
                term = phases[phase][base:base + rb, c0:c0 + cb] * w_ref[k:k + 1, c0:c0 + cb]
                acc = term if acc is None else acc + term
            out_cols.append(acc)
        out_rows.append(out_cols[0] if len(out_cols) == 1 else jnp.concatenate(out_cols, axis=1))
    return out_rows[0] if len(out_rows) == 1 else jnp.concatenate(out_rows, axis=0)


def _row_views(refs, r0, rows, lead=0):
    return [ref.at[pl.ds(r0, lead + rows), :] for ref in refs]


def _mix_in_kernel(x_ref, lng_ref, lnb_ref, wqkv_ref, wz_ref, wglu_ref, wba_ref, cw_ref, nega_ref,
                   dtb_ref, hq_in_ref, hc_in_ref,
                   q_ref, k_ref, v_ref, z_ref, c_ref, bg_ref, hq_out_ref, hc_out_ref, h_ref,
                   qkv_ext, c_ext):
    tm = x_ref.shape[0]

    @pl.when(pl.program_id(1) == 0)
    def _():
        qkv_ext[0:QKV_HALO, :] = hq_in_ref[...]
        c_ext[0:CONF_HALO, :] = hc_in_ref[...]

    th = min(tm, SUB_ROWS)
    for r0 in range(0, tm, th):
        x_v, q_v, k_v, v_v, z_v, c_v, bg_v, h_v = _row_views(
            (x_ref, q_ref, k_ref, v_ref, z_ref, c_ref, bg_ref, h_ref), r0, th)
        qkv_v, = _row_views((qkv_ext,), r0, th, QKV_HALO)
        cext_v, = _row_views((c_ext,), r0, th, CONF_HALO)
        _mix_in_rows(x_v, lng_ref, lnb_ref, wqkv_ref, wz_ref, wglu_ref, wba_ref, cw_ref, nega_ref, dtb_ref,
                     q_v, k_v, v_v, z_v, c_v, bg_v, h_v, qkv_v, cext_v)

    q_tail = qkv_ext[tm:tm + QKV_HALO, :]
    c_tail = c_ext[tm:tm + CONF_HALO, :]
    qkv_ext[0:QKV_HALO, :] = q_tail
    c_ext[0:CONF_HALO, :] = c_tail
    hq_out_ref[...] = q_tail
    hc_out_ref[...] = c_tail


def _mix_in_rows(x_ref, lng_ref, lnb_ref, wqkv_ref, wz_ref, wglu_ref, wba_ref, cw_ref, nega_ref, dtb_ref,
                 q_ref, k_ref, v_ref, z_ref, c_ref, bg_ref, h_ref, qkv_ext, c_ext):
    tm = x_ref.shape[0]
    h = _layer_norm(x_ref[...], lng_ref[...], lnb_ref[...])
    h_ref[...] = h
    hb = h.astype(BF16)

    qkv_ext[QKV_HALO:QKV_HALO + tm, :] = _dot(hb, wqkv_ref[...])
    qkv = _silu(_causal_depthwise(qkv_ext, cw_ref, SHORT_CONV, QKV_HALO, tm))
    for hd in range(DN_HEADS):
        lo = hd * HEAD_DIM
        qh = qkv[:, lo:lo + HEAD_DIM]
        kh = qkv[:, DN_WIDTH + lo:DN_WIDTH + lo + HEAD_DIM]
        q_ref[:, lo:lo + HEAD_DIM] = (qh * (lax.rsqrt(jnp.sum(qh * qh, axis=-1, keepdims=True) + 1e-6)
                                            * (HEAD_DIM ** -0.5))).astype(q_ref.dtype)
        k_ref[:, lo:lo + HEAD_DIM] = (kh * lax.rsqrt(jnp.sum(kh * kh, axis=-1, keepdims=True) + 1e-6)
                                      ).astype(k_ref.dtype)
    v_ref[...] = qkv[:, 2 * DN_WIDTH:].astype(v_ref.dtype)
    z_ref[...] = _dot(hb, wz_ref[...]).astype(z_ref.dtype)

    ba = _dot(hb, wba_ref[...])
    lane = lax.broadcasted_iota(jnp.int32, ba.shape, 1)
    sp_in = ba + dtb_ref[...]
    softplus = jnp.maximum(sp_in, 0.0) + jnp.log(1.0 + jnp.exp(-jnp.abs(sp_in)))
    bg_ref[...] = jnp.where(lane < DN_HEADS, _sigmoid(ba), nega_ref[...] * softplus)

    glu = _dot(hb, wglu_ref[...])
    cw = glu.shape[1] // 2
    c_pre = glu[:, :cw] * _sigmoid(glu[:, cw:])
    c_ref[...] = c_pre.astype(c_ref.dtype)
    c_ext[CONF_HALO:CONF_HALO + tm, :] = c_pre


def _mix_in(x, p, halo_q, halo_c, tm):
    bsz, seq, d = x.shape
    assert seq % tm == 0
    qkv_w = 3 * DN_WIDTH
    conf_w = p['dw_w'].shape[1]

    def row(width):
        return pl.BlockSpec((None, tm, width), lambda b, t: (b, t, 0))

    def per_batch(rows, width):
        return pl.BlockSpec((None, rows, width), lambda b, t: (b, 0, 0))

    consts = [p['ln_emb_g'], p['ln_emb_b'], p['w_qkv'], p['w_z'], p['w_glu'], p['w_ba'], p['conv_w'],
              p['neg_a'], p['dt_b'], halo_q, halo_c]
    sds = jax.ShapeDtypeStruct
    out_shape = ([sds((bsz, seq, DN_WIDTH), BF16)] * 4 + [sds((bsz, seq, conf_w), BF16),
                 sds((bsz, seq, LANES), F32), sds((bsz, QKV_HALO, qkv_w), F32),
                 sds((bsz, CONF_HALO, conf_w), F32), sds((bsz, seq, d), F32)])
    out_specs = ([row(DN_WIDTH)] * 4 + [row(conf_w), row(LANES), per_batch(QKV_HALO, qkv_w),
                 per_batch(CONF_HALO, conf_w), row(d)])
    return pl.pallas_call(
        _mix_in_kernel,
        grid=(bsz, seq // tm),
        in_specs=[row(d)] + [_full_spec(c.shape) for c in consts],
        out_specs=out_specs,
        out_shape=out_shape,
        scratch_shapes=[pltpu.VMEM((QKV_HALO + tm, qkv_w), F32), pltpu.VMEM((CONF_HALO + tm, conf_w), F32)],
        compiler_params=pltpu.CompilerParams(dimension_semantics=("parallel", "arbitrary"),
                                             vmem_limit_bytes=VMEM_LIMIT),
        name="mix_in",
    )(x, *consts)


def _bmm(a, b):
    return jnp.einsum('nij,njk->nik', a, b, preferred_element_type=F32)


def _delta_kernel(q_ref, k_ref, v_ref, z_ref, bg_ref, s0_ref, gain_ref, o_ref, sfin_ref,
                  s_ref, u_s, wq_s, attn_s, kd_s, egl_s, *, chunks):
    j = pl.program_id(1)
    slot = j % 2
    prev = 1 - slot
    gain = gain_ref[...]

    @pl.when(j == 0)
    def _():
        s_ref[...] = s0_ref[...]
        u_s[1] = jnp.zeros(u_s.shape[1:], u_s.dtype)
        wq_s[1] = jnp.zeros(wq_s.shape[1:], wq_s.dtype)
        attn_s[1] = jnp.zeros(attn_s.shape[1:], attn_s.dtype)
        kd_s[1] = jnp.zeros(kd_s.shape[1:], kd_s.dtype)
        egl_s[1] = jnp.zeros(egl_s.shape[1:], egl_s.dtype)

    def recurrence():
        live = j > 0
        state = [s_ref[hd] for hd in range(DN_HEADS)]
        for c in range(chunks):
            idx = [hd * chunks + c for hd in range(DN_HEADS)]
            wq = [_dot(wq_s[prev, n], state[hd].astype(BF16)) for hd, n in enumerate(idx)]
            yield
            v_new = [(u_s[prev, n] - wq[hd][:CHUNK]).astype(BF16) for hd, n in enumerate(idx)]
            o = [wq[hd][CHUNK:] + _dot(attn_s[prev, n], v_new[hd]) for hd, n in enumerate(idx)]
            state = [state[hd] * egl_s[prev, n][0:1, :]
                     + lax.dot_general(kd_s[prev, n], v_new[hd], (((0,), (0,)), ((), ())),
                                       preferred_element_type=F32)
                     for hd, n in enumerate(idx)]
            yield
            for hd in range(DN_HEADS):
                cols = slice(hd * HEAD_DIM, (hd + 1) * HEAD_DIM)
                rows = slice(c * CHUNK, (c + 1) * CHUNK)
                r = o[hd] * lax.rsqrt(jnp.mean(o[hd] * o[hd], axis=-1, keepdims=True) + 1e-6)
                o_ref[rows, cols] = (r * gain * _silu(z_ref[rows, cols].astype(F32))).astype(o_ref.dtype)
        for hd in range(DN_HEADS):
            kept = jnp.where(live, state[hd], s_ref[hd])
            s_ref[hd] = kept
            sfin_ref[hd] = kept

    def preparation():
        yield from _delta_prepare(q_ref, k_ref, v_ref, bg_ref, u_s, wq_s, attn_s, kd_s, egl_s, slot, chunks)

    halves = [recurrence(), preparation()]
    while halves:
        for gen in list(halves):
            try:
                next(gen)
            except StopIteration:
                halves.remove(gen)


def _delta_prepare(q_ref, k_ref, v_ref, bg_ref, u_s, wq_s, attn_s, kd_s, egl_s, slot, chunks):
    ii = lax.broadcasted_iota(jnp.int32, (CHUNK, CHUNK), 0)
    jj = lax.broadcasted_iota(jnp.int32, (CHUNK, CHUNK), 1)
    causal = ii >= jj
    strict = ii > jj
    eye = (ii == jj).astype(F32)
    bg3 = bg_ref[...].reshape(chunks, CHUNK, LANES)
    tril_b = jnp.broadcast_to(causal.astype(BF16), (chunks, CHUNK, CHUNK))
    p1 = bg3.astype(BF16)
    r1 = bg3 - p1.astype(F32)
    p2 = r1.astype(BF16)
    p3 = (r1 - p2.astype(F32)).astype(BF16)
    gc3 = _bmm(tril_b, p1) + _bmm(tril_b, p2) + _bmm(tril_b, p3)
    yield

    def heads(ref):
        return jnp.concatenate([ref[:, hd * HEAD_DIM:(hd + 1) * HEAD_DIM].astype(F32).reshape(chunks, CHUNK, HEAD_DIM)
                                for hd in range(DN_HEADS)], axis=0)
    q = heads(q_ref)
    k = heads(k_ref)
    v = heads(v_ref)
    bet = jnp.concatenate([bg3[:, :, hd:hd + 1] for hd in range(DN_HEADS)], axis=0)
    gc = jnp.concatenate([gc3[:, :, DN_HEADS + hd:DN_HEADS + hd + 1] for hd in range(DN_HEADS)], axis=0)
    gc_t = [gc3[c].T for c in range(chunks)]
    decay = jnp.stack([
        jnp.exp(jnp.where(causal, gc3[c][:, DN_HEADS + hd:DN_HEADS + hd + 1]
                          - gc_t[c][DN_HEADS + hd:DN_HEADS + hd + 1, :], -jnp.inf))
        for hd in range(DN_HEADS) for c in range(chunks)], axis=0)

    kb = k * bet
    g_all = jnp.einsum('nid,njd->nij', jnp.concatenate([kb, q], axis=1).astype(BF16), k.astype(BF16),
                       preferred_element_type=F32)
    yield
    a_low = jnp.where(strict, g_all[:, :CHUNK] * decay, 0.0)
    attn = (g_all[:, CHUNK:] * decay).astype(BF16)

    l_mat = eye + a_low
    l_bf = l_mat.astype(BF16)
    t_mat = eye - a_low
    for _ in range(4):
        res = eye - _bmm(l_bf, t_mat.astype(BF16))
        yield
        t_mat = t_mat + _bmm(t_mat.astype(BF16), res.astype(BF16))
        yield
    l_lo = (l_mat - l_bf.astype(F32)).astype(BF16)
    t_hi = t_mat.astype(BF16)
    t_lo = (t_mat - t_hi.astype(F32)).astype(BF16)
    res = eye - (_bmm(l_bf, t_hi) + _bmm(l_lo, t_hi) + _bmm(l_bf, t_lo))
    yield
    t_mat = t_mat + _bmm(t_hi, res.astype(BF16))
    yield

    eg = jnp.exp(gc)
    uw = _bmm(t_mat.astype(BF16), jnp.concatenate([v * bet, kb * eg], axis=2).astype(BF16))
    yield
    u = uw[:, :, :HEAD_DIM]
    wq_lhs = jnp.concatenate([uw[:, :, HEAD_DIM:], q * eg], axis=1).astype(BF16)
    g_last = gc[:, CHUNK - 1:CHUNK, :]
    k_dec = (k * jnp.exp(g_last - gc)).astype(BF16)
    eg_last = jnp.exp(g_last)

    u_s[slot] = u
    wq_s[slot] = wq_lhs
    attn_s[slot] = attn
    kd_s[slot] = k_dec
    egl_s[slot] = jnp.broadcast_to(eg_last, egl_s.shape[1:])


def _delta(q, k, v, z, bg, s0, gain, chunks):
    bsz, seq, _ = q.shape
    rows = chunks * CHUNK
    assert seq % rows == 0

    nj = seq // rows
    nb = DN_HEADS * chunks

    def prep(width):
        return pl.BlockSpec((None, rows, width), lambda b, j: (b, jnp.minimum(j, nj - 1), 0))

    def scan(width):
        return pl.BlockSpec((None, rows, width), lambda b, j: (b, jnp.maximum(j - 1, 0), 0))

    state_shape = (DN_HEADS, HEAD_DIM, HEAD_DIM)
    return pl.pallas_call(
        functools.partial(_delta_kernel, chunks=chunks),
        grid=(bsz, nj + 1),
        in_specs=[prep(DN_WIDTH)] * 3 + [scan(DN_WIDTH), prep(LANES), _full_spec(state_shape),
                                         _full_spec(gain.shape)],
        out_specs=[scan(DN_WIDTH), pl.BlockSpec((None,) + state_shape, lambda b, j: (b, 0, 0, 0))],
        out_shape=[jax.ShapeDtypeStruct((bsz, seq, DN_WIDTH), BF16),
                   jax.ShapeDtypeStruct((bsz,) + state_shape, F32)],
        scratch_shapes=[pltpu.VMEM(state_shape, F32),
                        pltpu.VMEM((2, nb, CHUNK, HEAD_DIM), F32),
                        pltpu.VMEM((2, nb, 2 * CHUNK, HEAD_DIM), BF16),
                        pltpu.VMEM((2, nb, CHUNK, CHUNK), BF16),
                        pltpu.VMEM((2, nb, CHUNK, HEAD_DIM), BF16),
                        pltpu.VMEM((2, nb, SUBLANES, HEAD_DIM), F32)],
        compiler_params=pltpu.CompilerParams(dimension_semantics=("parallel", "arbitrary"),
                                             vmem_limit_bytes=VMEM_LIMIT),
        name="delta",
    )(q, k, v, z, bg, s0, gain)


def _mix_out_kernel(h_ref, o_ref, c_ref, wo_ref, g1_ref, b1_ref, wrh_ref, wrl_ref, br_ref,
                    dww_ref, dwb_ref, cvg_ref, cvb_ref, hc_in_ref,
                    h1_ref, h1p_ref, route_ref, route_t_ref, cnt_out_ref, cnt_ref, c_ext, *, alpha, tiles_per_seq):
    tm = h_ref.shape[0]

    @pl.when(pl.program_id(0) == 0)
    def _():
        cnt_ref[...] = jnp.zeros_like(cnt_ref)

    @pl.when(pl.program_id(0) % tiles_per_seq == 0)
    def _():
        c_ext[0:CONF_HALO, :] = hc_in_ref[...]

    th = min(tm, SUB_ROWS_OUT)
    n_slabs = h1p_ref.shape[0] // tm
    for r0 in range(0, tm, th):
        h_v, o_v, c_v, h1_v, route_v = _row_views((h_ref, o_ref, c_ref, h1_ref, route_ref), r0, th)
        cext_v, = _row_views((c_ext,), r0, th, CONF_HALO)
        h1p_v, = _row_views((h1p_ref,), r0 * n_slabs, th * n_slabs)
        _mix_out_rows(h_v, o_v, c_v, wo_ref, g1_ref, b1_ref, wrh_ref, wrl_ref, br_ref, dww_ref, dwb_ref, cvg_ref,
                      cvb_ref, h1_v, h1p_v, route_v, route_t_ref.at[:, pl.ds(r0, th)], cnt_ref, cext_v, alpha)
    c_ext[0:CONF_HALO, :] = c_ext[tm:tm + CONF_HALO, :]
    cnt_out_ref[...] = jnp.broadcast_to(cnt_ref[...], cnt_out_ref.shape)


def _mix_out_rows(h_ref, o_ref, c_ref, wo_ref, g1_ref, b1_ref, wrh_ref, wrl_ref, br_ref, dww_ref, dwb_ref, cvg_ref,
                  cvb_ref, h1_ref, h1p_ref, route_ref, route_t_ref, cnt_ref, c_ext, alpha):
    tm = h_ref.shape[0]

    c_ext[CONF_HALO:CONF_HALO + tm, :] = c_ref[...].astype(F32)
    conv = _causal_depthwise(c_ext, dww_ref, CONF_KERNEL, CONF_HALO, tm) + dwb_ref[...]
    conf = _silu(_layer_norm(conv, cvg_ref[...], cvb_ref[...]))

    h = h_ref[...]
    dn = o_ref.shape[1]
    mix = _dot(o_ref[...].astype(BF16), wo_ref[0:dn, :]) + _dot(conf.astype(BF16), wo_ref[dn:, :])
    h1 = _layer_norm(alpha * h + mix, g1_ref[...], b1_ref[...])
    h1_ref[...] = h1
    _store_slabs(h1p_ref, _pack_bf16_pairs(h1))

    hh, hl = _split2(h1)
    logits = _dot(hh, wrh_ref[...]) + _dot(hl, wrh_ref[...]) + _dot(hh, wrl_ref[...]) + br_ref[...]
    lane = lax.broadcasted_iota(jnp.int32, logits.shape, 1).astype(F32)
    big = float(LANES)
    neg = -jnp.inf

    def first_argmax(vals):
        top = jnp.max(vals, axis=-1, keepdims=True)
        return top, jnp.min(jnp.where(vals == top, lane, big), axis=-1, keepdims=True)

    grp = jnp.where(lane < N_GROUPS, logits, neg)
    g_top, g_sel = first_argmax(grp)
    p_group = 1.0 / jnp.sum(jnp.exp(grp - g_top), axis=-1, keepdims=True)
    lo = N_GROUPS + EXPERTS_PER_GROUP * g_sel
    in_grp = jnp.where((lane >= lo) & (lane < lo + EXPERTS_PER_GROUP), logits, neg)
    m1, i1 = first_argmax(in_grp)
    m2, i2 = first_argmax(jnp.where(lane == i1, neg, in_grp))
    s = jnp.exp(m2 - m1)
    w1 = p_group / (1.0 + s)
    w2 = p_group * s / (1.0 + s)
    e1 = i1 - N_GROUPS
    e2 = i2 - N_GROUPS

    tm = logits.shape[0]
    oh1 = (lane == e1).astype(F32)
    oh2 = (lane == e2).astype(F32)
    both = oh1 + oh2
    ti = lax.broadcasted_iota(jnp.int32, (tm, tm), 0)
    tj = lax.broadcasted_iota(jnp.int32, (tm, tm), 1)
    base = _dot((ti > tj).astype(BF16), both.astype(BF16)) + cnt_ref[...]
    r1 = jnp.sum(oh1 * base, axis=-1, keepdims=True)
    r2 = jnp.sum(oh2 * base, axis=-1, keepdims=True)
    cnt_ref[...] = cnt_ref[...] + jnp.sum(both, axis=0, keepdims=True)

    vals = (e1, e2, w1, w2, r1, r2)
    route = jnp.zeros_like(logits)
    for idx, val in enumerate(vals):
        route = jnp.where(lane == idx, val, route)
    route_ref[...] = route
    route_t_ref[...] = route.T[0:SUBLANES, :]


def _mix_out(h2d, o2d, c2d, p, halo_c, seq, tm, alpha):
    n, d = h2d.shape
    assert seq % tm == 0
    slabs = d // (2 * LANES)

    def row(width):
        return pl.BlockSpec((tm, width), lambda i: (i, 0))

    consts = [p['w_out'], p['ln1_g'], p['ln1_b'], p['w_r_hi'], p['w_r_lo'], p['b_r'],
              p['dw_w'], p['dw_b'], p['cv_g'], p['cv_b'], halo_c]
    return pl.pallas_call(
        functools.partial(_mix_out_kernel, alpha=alpha, tiles_per_seq=seq // tm),
        grid=(n // tm,),
        in_specs=[row(d), row(o2d.shape[1]), row(c2d.shape[1])] + [_full_spec(c.shape) for c in consts],
        out_specs=[row(d), pl.BlockSpec((tm * slabs, LANES), lambda i: (i, 0)), row(LANES),
                   pl.BlockSpec((SUBLANES, tm), lambda i: (0, i)), _full_spec((SUBLANES, LANES))],
        out_shape=[jax.ShapeDtypeStruct((n, d), F32), jax.ShapeDtypeStruct((n * slabs, LANES), jnp.uint32),
                   jax.ShapeDtypeStruct((n, LANES), F32),
                   jax.ShapeDtypeStruct((SUBLANES, n), F32), jax.ShapeDtypeStruct((SUBLANES, LANES), F32)],
        scratch_shapes=[pltpu.VMEM((1, LANES), F32), pltpu.VMEM((CONF_HALO + tm, c2d.shape[1]), F32)],
        compiler_params=pltpu.CompilerParams(dimension_semantics=("arbitrary",), vmem_limit_bytes=VMEM_LIMIT),
        name="mix_out",
    )(h2d, o2d, c2d, *consts)


def _dispatch_kernel(dest_ref, pad_lo_ref, pad_hi_ref, h1s_hbm, xs_hbm, ring, zslab, fsem, ssem, zsem, *,
                     td, n_slabs, n_tokens):
    i = pl.program_id(0)
    nb = pl.num_programs(0)
    slot = i % RING
    tile_rows = td * n_slabs

    def fetch(step):
        start = pl.multiple_of(step * tile_rows, tile_rows)
        return pltpu.make_async_copy(h1s_hbm.at[pl.ds(start, tile_rows), :], ring.at[step % RING],
                                     fsem.at[step % RING])

    def wait_scatter(step):
        for _ in range(TOP_K):
            pltpu.make_async_copy(ring.at[step % RING], xs_hbm.at[pl.ds(0, tile_rows), :],
                                  ssem.at[step % RING]).wait()

    zrows = zslab.shape[0] // n_slabs

    def pad_copy(row, size):
        return pltpu.make_async_copy(
            zslab.at[pl.ds(0, size * n_slabs), :],
            xs_hbm.at[pl.ds(pl.multiple_of(row * n_slabs, n_slabs), size * n_slabs), :], zsem.at[0])

    def for_each_pad_copy(fn):
        def per_expert(e, carry):
            lo = pad_lo_ref[e]
            length = pad_hi_ref[e] - lo
            for b in range(zrows.bit_length()):
                size = 1 << b

                @pl.when(((length >> b) & 1) == 1)
                def _():
                    fn(pad_copy(lo + (length & (size - 1)), size))
            return carry
        lax.fori_loop(0, N_EXPERTS, per_expert, 0)
        tail_lo = pad_lo_ref[N_EXPERTS]

        def per_piece(piece, carry):
            fn(pad_copy(tail_lo + piece * zrows, zrows))
            return carry
        lax.fori_loop(0, (pad_hi_ref[N_EXPERTS] - tail_lo) // zrows, per_piece, 0)

    @pl.when(i == 0)
    def _():
        fetch(0).start()
        zslab[...] = jnp.zeros_like(zslab)
        for_each_pad_copy(lambda cp: cp.start())

    @pl.when((i == 0) & (nb > 1))
    def _():
        fetch(1).start()

    fetch(i).wait()

    def issue_body(r, carry):
        src = ring.at[slot, pl.ds(pl.multiple_of(r * n_slabs, n_slabs), n_slabs), :]
        for k in range(TOP_K):
            dst_row = pl.multiple_of(dest_ref[k * n_tokens + i * td + r], n_slabs)
            pltpu.make_async_copy(src, xs_hbm.at[pl.ds(dst_row, n_slabs), :], ssem.at[slot]).start(priority=k)
        return carry
    lax.fori_loop(0, td, issue_body, 0, unroll=DMA_UNROLL)

    @pl.when(i > 0)
    def _():
        wait_scatter(i - 1)

    @pl.when(i + 2 < nb)
    def _():
        fetch(i + 2).start()

    @pl.when(i == nb - 1)
    def _():
        wait_scatter(i)
        for_each_pad_copy(lambda cp: cp.wait())


def _dispatch(dest_rows, pad_lo, pad_hi, h1s, cap_rows, td, n_slabs, bm):
    n = h1s.shape[0] // n_slabs
    assert n % td == 0
    grid_spec = pltpu.PrefetchScalarGridSpec(
        num_scalar_prefetch=3,
        grid=(n // td,),
        in_specs=[pl.BlockSpec(memory_space=pl.ANY)],
        out_specs=pl.BlockSpec(memory_space=pl.ANY),
        scratch_shapes=[pltpu.VMEM((RING, td * n_slabs, LANES), h1s.dtype),
                        pltpu.VMEM((bm // 2 * n_slabs, LANES), h1s.dtype),
                        pltpu.SemaphoreType.DMA((RING,)), pltpu.SemaphoreType.DMA((RING,)),
                        pltpu.SemaphoreType.DMA((1,))],
    )
    return pl.pallas_call(
        functools.partial(_dispatch_kernel, td=td, n_slabs=n_slabs, n_tokens=n),
        grid_spec=grid_spec,
        out_shape=jax.ShapeDtypeStruct((cap_rows * n_slabs, LANES), h1s.dtype),
        compiler_params=pltpu.CompilerParams(dimension_semantics=("arbitrary",), disable_bounds_checks=True),
        name="dispatch",
    )(dest_rows, pad_lo, pad_hi, h1s)


def _expert_kernel(be_ref, nu_ref, xs_ref, wg_ref, wu_ref, wd_ref, y_ref, wg_bf, wu_bf, wd_bf, *, bm, n_slabs):
    i = pl.program_id(0)
    used = i < nu_ref[0]

    @pl.when(used & ((i == 0) | (be_ref[i] != be_ref[jnp.maximum(i - 1, 0)])))
    def _():
        wg_bf[...] = wg_ref[...].astype(BF16)
        wu_bf[...] = wu_ref[...].astype(BF16)
        wd_bf[...] = wd_ref[...].astype(BF16)

    @pl.when(used)
    def _():
        xb = _unpack_bf16_pairs(_load_slabs(xs_ref, bm, n_slabs)).astype(BF16)
        hid = _silu(_dot(xb, wg_bf[...])) * _dot(xb, wu_bf[...])
        _store_slabs(y_ref, _pack_bf16_pairs(_dot(hid.astype(BF16), wd_bf[...])))

    @pl.when(jnp.logical_not(used))
    def _():
        y_ref[...] = jnp.zeros_like(y_ref)


def _experts(block_expert, n_used, xs, w_gate, w_up, w_down, bm):
    n_blocks = block_expert.shape[0]
    d = w_gate.shape[1]
    ff = w_gate.shape[2]
    n_slabs = d // (2 * LANES)

    def blk(i, be, nu):
        return jnp.minimum(i, nu[0] - 1)

    grid_spec = pltpu.PrefetchScalarGridSpec(
        num_scalar_prefetch=2,
        grid=(n_blocks,),
        in_specs=[pl.BlockSpec((bm * n_slabs, LANES), lambda i, be, nu: (blk(i, be, nu), 0)),
                  pl.BlockSpec((None, d, ff), lambda i, be, nu: (be[blk(i, be, nu)], 0, 0)),
                  pl.BlockSpec((None, d, ff), lambda i, be, nu: (be[blk(i, be, nu)], 0, 0)),
                  pl.BlockSpec((None, ff, d), lambda i, be, nu: (be[blk(i, be, nu)], 0, 0))],
        out_specs=pl.BlockSpec((bm * n_slabs, LANES), lambda i, be, nu: (i, 0)),
        scratch_shapes=[pltpu.VMEM((d, ff), BF16), pltpu.VMEM((d, ff), BF16), pltpu.VMEM((ff, d), BF16)],
    )
    return pl.pallas_call(
        functools.partial(_expert_kernel, bm=bm, n_slabs=n_slabs),
        grid_spec=grid_spec,
        out_shape=jax.ShapeDtypeStruct(xs.shape, xs.dtype),
        compiler_params=pltpu.CompilerParams(dimension_semantics=("arbitrary",), vmem_limit_bytes=VMEM_LIMIT),
        name="experts",
    )(block_expert, n_used, xs, w_gate, w_up, w_down)


def _combine_kernel(dest_ref, y_hbm, h1_ref, route_ref, g2_ref, b2_ref, out_ref, ybuf, sem, *, tm, n_slabs, alpha):
    i = pl.program_id(0)
    nb = pl.num_programs(0)
    part = tm * n_slabs

    def issue(blk, slot):
        def body(r, carry):
            for k in range(TOP_K):
                src_row = pl.multiple_of(dest_ref[k * (nb * tm) + blk * tm + r], n_slabs)
                dst_row = pl.multiple_of((slot * TOP_K + k) * part + r * n_slabs, n_slabs)
                pltpu.make_async_copy(y_hbm.at[pl.ds(src_row, n_slabs), :], ybuf.at[pl.ds(dst_row, n_slabs), :],
                                      sem.at[slot]).start(priority=1)
            return carry
        lax.fori_loop(0, tm, body, 0, unroll=DMA_UNROLL)

    @pl.when(i == 0)
    def _():
        issue(0, 0)

    @pl.when(i + 1 < nb)
    def _():
        issue(i + 1, (i + 1) % 2)

    slot = i % 2
    base = pl.multiple_of(slot * (TOP_K * part), TOP_K * part)
    pltpu.make_async_copy(y_hbm.at[pl.ds(0, TOP_K * part), :], ybuf.at[pl.ds(base, TOP_K * part), :],
                          sem.at[slot]).wait()

    route = route_ref[...]
    ffn = (_unpack_bf16_pairs(_load_slabs(ybuf, tm, n_slabs, base)) * route[:, 2:3]
           + _unpack_bf16_pairs(_load_slabs(ybuf, tm, n_slabs, base + part)) * route[:, 3:4])
    out_ref[...] = _layer_norm(alpha * h1_ref[...] + ffn, g2_ref[...], b2_ref[...])


def _combine(dest_rows, y_sorted, h1, route, ln2_g, ln2_b, tm, alpha):
    n, d = h1.shape
    n_slabs = d // (2 * LANES)
    assert n % tm == 0
    grid_spec = pltpu.PrefetchScalarGridSpec(
        num_scalar_prefetch=1,
        grid=(n // tm,),
        in_specs=[pl.BlockSpec(memory_space=pl.ANY),
                  pl.BlockSpec((tm, d), lambda i, dest: (i, 0)),
                  pl.BlockSpec((tm, LANES), lambda i, dest: (i, 0)),
                  pl.BlockSpec((1, d), lambda i, dest: (0, 0)),
                  pl.BlockSpec((1, d), lambda i, dest: (0, 0))],
        out_specs=pl.BlockSpec((tm, d), lambda i, dest: (i, 0)),
        scratch_shapes=[pltpu.VMEM((2 * TOP_K * tm * n_slabs, LANES), y_sorted.dtype),
                        pltpu.SemaphoreType.DMA((2,))],
    )
    return pl.pallas_call(
        functools.partial(_combine_kernel, tm=tm, n_slabs=n_slabs, alpha=alpha),
        grid_spec=grid_spec,
        out_shape=jax.ShapeDtypeStruct((n, d), F32),
        compiler_params=pltpu.CompilerParams(dimension_semantics=("arbitrary",), vmem_limit_bytes=VMEM_LIMIT,
                                             disable_bounds_checks=True),
        name="combine",
    )(dest_rows, y_sorted, h1, route, ln2_g, ln2_b)


def _dispatch_plan(route_t, counts, bm, n_slabs):
    n = route_t.shape[1]
    expert_id = route_t[0:TOP_K].astype(jnp.int32)
    rank = route_t[4:4 + TOP_K].astype(jnp.int32)
    padded = (counts + bm - 1) // bm * bm
    pad_end = jnp.cumsum(padded)
    pad_start = pad_end - padded
    experts = jnp.arange(N_EXPERTS, dtype=jnp.int32)[:, None, None]
    dest = jnp.sum(jnp.where(expert_id[None] == experts, pad_start[:, None, None], 0), axis=0) + rank
    n_blocks = (n * TOP_K + bm - 1) // bm + N_EXPERTS
    block_start = jnp.arange(n_blocks, dtype=jnp.int32) * bm
    block_expert = jnp.minimum(jnp.sum((block_start[:, None] >= pad_end[None, :]).astype(jnp.int32), axis=1),
                               N_EXPERTS - 1).astype(jnp.int32)
    n_used = (pad_end[-1:] // bm).astype(jnp.int32)
    dest_rows = (dest * n_slabs).reshape(-1).astype(jnp.int32)
    pad_lo = jnp.concatenate([pad_start + counts, pad_end[-1:]]).astype(jnp.int32)
    pad_hi = jnp.concatenate([pad_end, jnp.full((1,), n_blocks * bm, pad_end.dtype)]).astype(jnp.int32)
    return dest_rows, pad_lo, pad_hi, block_expert, n_used, n_blocks


def _pad_lanes(w, width=LANES):
    return jnp.pad(w, [(0, 0)] * (w.ndim - 1) + [(0, width - w.shape[-1])])


def kernel(x, meta_tokens, ln_emb_g, ln_emb_b, w_in, conv_qkv_w, a_log, dt_bias, dn_norm_g, conv_dw_w, conv_dw_b, cv_norm_g, cv_norm_b, w_out, ln1_g, ln1_b, w_group, b_group, w_router, b_router, w_exp_gate, w_exp_up, w_exp_down, ln2_g, ln2_b):
    depth = w_in.shape[0]
    assert depth == 1, "single-layer block"
    bsz, seq, d = x.shape
    alpha = (2.0 * depth) ** 0.25
    qkv_w = 3 * DN_WIDTH
    w_in0 = w_in[0]
    glu_off = 4 * DN_WIDTH + 2 * DN_HEADS
    row = lambda a: a.reshape(1, -1).astype(F32)
    p = {
        'ln_emb_g': row(ln_emb_g), 'ln_emb_b': row(ln_emb_b),
        'w_qkv': w_in0[:, :qkv_w].astype(BF16),
        'w_z': w_in0[:, qkv_w:4 * DN_WIDTH].astype(BF16),
        'w_ba': _pad_lanes(w_in0[:, 4 * DN_WIDTH:glu_off]).astype(BF16),
        'w_glu': w_in0[:, glu_off:].astype(BF16),
        'conv_w': conv_qkv_w[0].astype(F32),
        'neg_a': _pad_lanes(jnp.concatenate([jnp.zeros((DN_HEADS,), F32), -jnp.exp(a_log[0].astype(F32))])[None]),
        'dt_b': _pad_lanes(jnp.concatenate([jnp.zeros((DN_HEADS,), F32), dt_bias[0].astype(F32)])[None]),
        'dw_w': conv_dw_w[0].astype(F32), 'dw_b': row(conv_dw_b[0]),
        'cv_g': row(cv_norm_g[0]), 'cv_b': row(cv_norm_b[0]),
        'w_out': w_out[0].astype(BF16), 'ln1_g': row(ln1_g[0]), 'ln1_b': row(ln1_b[0]),
    }
    w_r = _pad_lanes(jnp.concatenate([w_group[0], w_router[0]], axis=1).astype(F32))
    p['w_r_hi'] = w_r.astype(BF16)
    p['w_r_lo'] = (w_r - p['w_r_hi'].astype(F32)).astype(BF16)
    p['b_r'] = _pad_lanes(jnp.concatenate([b_group[0], b_router[0]])[None].astype(F32))
    gain = row(dn_norm_g[0])

    conf_w = p['dw_w'].shape[1]
    zero_hq = jnp.zeros((QKV_HALO, qkv_w), F32)
    zero_hc = jnp.zeros((CONF_HALO, conf_w), F32)
    mq, mk, mv, mz, _, mbg, halo_q, halo_c, _ = _mix_in(meta_tokens[None].astype(F32), p, zero_hq, zero_hc, N_META)
    front = lambda a: jnp.pad(a, [(0, 0), (CHUNK - N_META, 0), (0, 0)])
    s_zero = jnp.zeros((DN_HEADS, HEAD_DIM, HEAD_DIM), F32)
    _, s_meta = _delta(front(mq), front(mk), front(mv), front(mz), front(mbg), s_zero, gain, 1)

    q, k, v, z, c, bg, _, _, h = _mix_in(x, p, halo_q[0], halo_c[0], TM_IN)
    o, _ = _delta(q, k, v, z, bg, s_meta[0], gain, DELTA_CHUNKS)

    n = bsz * seq
    h1, h1p, route, route_t, cnt = _mix_out(h.reshape(n, d), o.reshape(n, DN_WIDTH), c.reshape(n, conf_w), p,
                                            halo_c[0], seq, TM_OUT, alpha)

    n_slabs = d // (2 * LANES)
    counts = cnt[0, :N_EXPERTS].astype(jnp.int32)
    dest_rows, pad_lo, pad_hi, block_expert, n_used, n_blocks = _dispatch_plan(route_t, counts, BM_EXPERT, n_slabs)
    xs = _dispatch(dest_rows, pad_lo, pad_hi, h1p, n_blocks * BM_EXPERT, TD_DISPATCH, n_slabs, BM_EXPERT)
    first_layer = lambda w: w.reshape(w.shape[1:])
    y_sorted = _experts(block_expert, n_used, xs, first_layer(w_exp_gate), first_layer(w_exp_up),
                        first_layer(w_exp_down), BM_EXPERT)
    out = _combine(dest_rows, y_sorted, h1, route, row(ln2_g[0]), row(ln2_b[0]), TM_COMBINE, alpha)
    return out.reshape(bsz, seq, d)
```

```python
import functools

import jax
import jax.numpy as jnp
from jax import lax
from jax.experimental import pallas as pl
from jax.experimental.pallas import tpu as pltpu

F32 = jnp.float32
BF16 = jnp.bfloat16

NORM_EPS = 1e-5
N_META = 16
DN_HEADS = 4
HEAD_DIM = 128
DN_WIDTH = DN_HEADS * HEAD_DIM
CHUNK = 64
SHORT_CONV = 4
CONF_KERNEL = 31
N_GROUPS = 4
EXPERTS_PER_GROUP = 8
N_EXPERTS = N_GROUPS * EXPERTS_PER_GROUP
TOP_K = 2
LANES = 128
SUBLANES = 8
QKV_HALO = 8
CONF_HALO = 32
VMEM_LIMIT = 56 * 1024 * 1024

TM_IN = 512
DELTA_CHUNKS = 8
TM_OUT = 512
BM_EXPERT = 512
TM_COMBINE = 512
TD_DISPATCH = 512
RING = 3
SUB_ROWS = 256
SUB_ROWS_OUT = 512
CONV_BLOCK_ROWS = 64
CONV_BLOCK_COLS = 512
MIXED_COLS = 5
COL_Q, COL_K, COL_V, COL_Z, COL_C = range(MIXED_COLS)
DMA_UNROLL = 8


def _dot(a, b):
    return jnp.dot(a, b, preferred_element_type=F32)


def _split2(x):
    hi = x.astype(BF16)
    lo = (x - hi.astype(F32)).astype(BF16)
    return hi, lo


def _dot_hilo(a, b):
    ah, al = _split2(a)
    bh, bl = _split2(b)
    return _dot(ah, bh) + _dot(al, bh) + _dot(ah, bl)


def _dot_exact01(m01, x):
    x1 = x.astype(BF16)
    r1 = x - x1.astype(F32)
    x2 = r1.astype(BF16)
    x3 = (r1 - x2.astype(F32)).astype(BF16)
    return _dot(m01, x1) + _dot(m01, x2) + _dot(m01, x3)


def _sigmoid(x):
    return 1.0 / (1.0 + jnp.exp(-x))


def _silu(x):
    return x * _sigmoid(x)


def _layer_norm(x, g, b):
    mu = jnp.mean(x, axis=-1, keepdims=True)
    xc = x - mu
    var = jnp.mean(xc * xc, axis=-1, keepdims=True)
    return xc * lax.rsqrt(var + NORM_EPS) * g + b


def _pack_bf16_pairs(x):
    half = x.shape[1] // 2
    lo = lax.bitcast_convert_type(x[:, :half].astype(BF16).astype(F32), jnp.uint32)
    hi = lax.bitcast_convert_type(x[:, half:].astype(BF16).astype(F32), jnp.uint32)
    return (lo >> 16) | (hi & jnp.uint32(0xFFFF0000))


def _unpack_bf16_pairs(p):
    lo = lax.bitcast_convert_type(p << 16, F32)
    hi = lax.bitcast_convert_type(p & jnp.uint32(0xFFFF0000), F32)
    return jnp.concatenate([lo, hi], axis=1)


def _store_slabs(ref, val):
    rows, d = val.shape
    n_slabs = d // LANES
    for s in range(n_slabs):
        ref[pl.ds(s, rows, stride=n_slabs), :] = val[:, s * LANES:(s + 1) * LANES]


def _load_slabs(ref, rows, n_slabs, base=0):
    return jnp.concatenate([ref[pl.ds(base + s, rows, stride=n_slabs), :] for s in range(n_slabs)], axis=1)


def _full_spec(shape):
    nd = len(shape)
    return pl.BlockSpec(shape, lambda *_: (0,) * nd)


def _causal_depthwise(ext_ref, w_ref, n_taps, halo, tm):
    ext = ext_ref[...]
    rows, cols = ext.shape
    first = halo - (n_taps - 1)
    groups = ext.reshape(rows // SUBLANES, SUBLANES, cols)
    sub = lax.broadcasted_iota(jnp.int32, (1, SUBLANES, 1), 1)

    def shifted(phase):
        rolled = pltpu.roll(groups, SUBLANES - phase, axis=1)
        nxt = jnp.concatenate([rolled[1:], rolled[:1]], axis=0)
        return jnp.where(sub < SUBLANES - phase, rolled, nxt).reshape(rows, cols)

    phases = {}
    for k in range(n_taps):
        phase = (first + k) % SUBLANES
        if phase not in phases:
            phases[phase] = ext if phase == 0 else shifted(phase)

    rb = min(tm, CONV_BLOCK_ROWS)
    cb = min(cols, CONV_BLOCK_COLS)
    out_rows = []
    for r0 in range(0, tm, rb):
        out_cols = []
        for c0 in range(0, cols, cb):
            acc = None
            for k in range(n_taps):
                phase = (first + k) % SUBLANES
                base = first + k - phase + r0
                term = phases[phase][base:base + rb, c0:c0 + cb] * w_ref[k:k + 1, c0:c0 + cb]
                acc = term if acc is None else acc + term
            out_cols.append(acc)
        out_rows.append(out_cols[0] if len(out_cols) == 1 else jnp.concatenate(out_cols, axis=1))
    return out_rows[0] if len(out_rows) == 1 else jnp.concatenate(out_rows, axis=0)


def _row_views(refs, r0, rows, lead=0):
    return [ref.at[pl.ds(r0, lead + rows), :] for ref in refs]


def _mix_in_kernel(x_ref, lng_ref, lnb_ref, wqkv_ref, wz_ref, wglu_ref, wba_ref, cw_ref, nega_ref,
                   dtb_ref, hq_in_ref, hc_in_ref,
                   m_ref, bg_ref, hq_out_ref, hc_out_ref, h_ref,
                   qkv_ext, c_ext):
    tm = x_ref.shape[0]

    @pl.when(pl.program_id(1) == 0)
    def _():
        qkv_ext[0:QKV_HALO, :] = hq_in_ref[...]
        c_ext[0:CONF_HALO, :] = hc_in_ref[...]

    th = min(tm, SUB_ROWS)
    for r0 in range(0, tm, th):
        x_v, bg_v, h_v = _row_views((x_ref, bg_ref, h_ref), r0, th)
        q_v, k_v, v_v, z_v, c_v = (m_ref.at[pl.ds(r0, th), pl.ds(col * DN_WIDTH, DN_WIDTH)]
                                   for col in range(MIXED_COLS))
        qkv_v, = _row_views((qkv_ext,), r0, th, QKV_HALO)
        cext_v, = _row_views((c_ext,), r0, th, CONF_HALO)
        _mix_in_rows(x_v, lng_ref, lnb_ref, wqkv_ref, wz_ref, wglu_ref, wba_ref, cw_ref, nega_ref, dtb_ref,
                     q_v, k_v, v_v, z_v, c_v, bg_v, h_v, qkv_v, cext_v)

    q_tail = qkv_ext[tm:tm + QKV_HALO, :]
    c_tail = c_ext[tm:tm + CONF_HALO, :]
    qkv_ext[0:QKV_HALO, :] = q_tail
    c_ext[0:CONF_HALO, :] = c_tail
    hq_out_ref[...] = q_tail
    hc_out_ref[...] = c_tail


def _mix_in_rows(x_ref, lng_ref, lnb_ref, wqkv_ref, wz_ref, wglu_ref, wba_ref, cw_ref, nega_ref, dtb_ref,
                 q_ref, k_ref, v_ref, z_ref, c_ref, bg_ref, h_ref, qkv_ext, c_ext):
    tm = x_ref.shape[0]
    h = _layer_norm(x_ref[...], lng_ref[...], lnb_ref[...])
    h_ref[...] = h
    hb = h.astype(BF16)

    qkv_ext[QKV_HALO:QKV_HALO + tm, :] = _dot(hb, wqkv_ref[...])
    qkv = _silu(_causal_depthwise(qkv_ext, cw_ref, SHORT_CONV, QKV_HALO, tm))
    for hd in range(DN_HEADS):
        lo = hd * HEAD_DIM
        qh = qkv[:, lo:lo + HEAD_DIM]
        kh = qkv[:, DN_WIDTH + lo:DN_WIDTH + lo + HEAD_DIM]
        q_ref[:, lo:lo + HEAD_DIM] = (qh * (lax.rsqrt(jnp.sum(qh * qh, axis=-1, keepdims=True) + 1e-6)
                                            * (HEAD_DIM ** -0.5))).astype(q_ref.dtype)
        k_ref[:, lo:lo + HEAD_DIM] = (kh * lax.rsqrt(jnp.sum(kh * kh, axis=-1, keepdims=True) + 1e-6)
                                      ).astype(k_ref.dtype)
    v_ref[...] = qkv[:, 2 * DN_WIDTH:].astype(v_ref.dtype)
    z_ref[...] = _dot(hb, wz_ref[...]).astype(z_ref.dtype)

    ba = _dot(hb, wba_ref[...])
    lane = lax.broadcasted_iota(jnp.int32, ba.shape, 1)
    sp_in = ba + dtb_ref[...]
    softplus = jnp.maximum(sp_in, 0.0) + jnp.log(1.0 + jnp.exp(-jnp.abs(sp_in)))
    bg_ref[...] = jnp.where(lane < DN_HEADS, _sigmoid(ba), nega_ref[...] * softplus)

    glu = _dot(hb, wglu_ref[...])
    cw = glu.shape[1] // 2
    c_pre = glu[:, :cw] * _sigmoid(glu[:, cw:])
    c_ref[...] = c_pre.astype(c_ref.dtype)
    c_ext[CONF_HALO:CONF_HALO + tm, :] = c_pre


def _mix_in(x, p, halo_q, halo_c, tm):
    bsz, seq, d = x.shape
    assert seq % tm == 0
    qkv_w = 3 * DN_WIDTH
    conf_w = p['dw_w'].shape[1]

    def row(width):
        return pl.BlockSpec((None, tm, width), lambda b, t: (b, t, 0))

    def per_batch(rows, width):
        return pl.BlockSpec((None, rows, width), lambda b, t: (b, 0, 0))

    consts = [p['ln_emb_g'], p['ln_emb_b'], p['w_qkv'], p['w_z'], p['w_glu'], p['w_ba'], p['conv_w'],
              p['neg_a'], p['dt_b'], halo_q, halo_c]
    sds = jax.ShapeDtypeStruct
    assert conf_w == DN_WIDTH, "q, k, v, z and the conformer channels share one array of equal-width column blocks"
    out_shape = [sds((bsz, seq, MIXED_COLS * DN_WIDTH), BF16),
                 sds((bsz, seq, LANES), F32), sds((bsz, QKV_HALO, qkv_w), F32),
                 sds((bsz, CONF_HALO, conf_w), F32), sds((bsz, seq, d), F32)]
    out_specs = [row(MIXED_COLS * DN_WIDTH), row(LANES), per_batch(QKV_HALO, qkv_w),
                 per_batch(CONF_HALO, conf_w), row(d)]
    return pl.pallas_call(
        _mix_in_kernel,
        grid=(bsz, seq // tm),
        in_specs=[row(d)] + [_full_spec(c.shape) for c in consts],
        out_specs=out_specs,
        out_shape=out_shape,
        scratch_shapes=[pltpu.VMEM((QKV_HALO + tm, qkv_w), F32), pltpu.VMEM((CONF_HALO + tm, conf_w), F32)],
        compiler_params=pltpu.CompilerParams(dimension_semantics=("parallel", "arbitrary"),
                                             vmem_limit_bytes=VMEM_LIMIT),
        name="mix_in",
    )(x, *consts)


def _bmm(a, b):
    return jnp.einsum('nij,njk->nik', a, b, preferred_element_type=F32)


def _delta_kernel(m_ref, z_ref, bg_ref, s0_ref, gain_ref, o_ref, sfin_ref,
                  s_ref, u_s, wq_s, attn_s, kd_s, egl_s, *, chunks):
    j = pl.program_id(1)
    slot = j % 2
    prev = 1 - slot
    gain = gain_ref[...]

    @pl.when(j == 0)
    def _():
        s_ref[...] = s0_ref[...]
        u_s[1] = jnp.zeros(u_s.shape[1:], u_s.dtype)
        wq_s[1] = jnp.zeros(wq_s.shape[1:], wq_s.dtype)
        attn_s[1] = jnp.zeros(attn_s.shape[1:], attn_s.dtype)
        kd_s[1] = jnp.zeros(kd_s.shape[1:], kd_s.dtype)
        egl_s[1] = jnp.zeros(egl_s.shape[1:], egl_s.dtype)

    def recurrence():
        live = j > 0
        state = [s_ref[hd] for hd in range(DN_HEADS)]
        for c in range(chunks):
            idx = [hd * chunks + c for hd in range(DN_HEADS)]
            wq = [_dot(wq_s[prev, n], state[hd].astype(BF16)) for hd, n in enumerate(idx)]
            yield
            v_new = [(u_s[prev, n] - wq[hd][:CHUNK]).astype(BF16) for hd, n in enumerate(idx)]
            o = [wq[hd][CHUNK:] + _dot(attn_s[prev, n], v_new[hd]) for hd, n in enumerate(idx)]
            state = [state[hd] * egl_s[prev, n][0:1, :]
                     + lax.dot_general(kd_s[prev, n], v_new[hd], (((0,), (0,)), ((), ())),
                                       preferred_element_type=F32)
                     for hd, n in enumerate(idx)]
            yield
            for hd in range(DN_HEADS):
                cols = slice(hd * HEAD_DIM, (hd + 1) * HEAD_DIM)
                rows = slice(c * CHUNK, (c + 1) * CHUNK)
                r = o[hd] * lax.rsqrt(jnp.mean(o[hd] * o[hd], axis=-1, keepdims=True) + 1e-6)
                o_ref[rows, cols] = (r * gain * _silu(z_ref[rows, cols].astype(F32))).astype(o_ref.dtype)
        for hd in range(DN_HEADS):
            kept = jnp.where(live, state[hd], s_ref[hd])
            s_ref[hd] = kept
            sfin_ref[hd] = kept

    def preparation():
        yield from _delta_prepare(m_ref, bg_ref, u_s, wq_s, attn_s, kd_s, egl_s, slot, chunks)

    halves = [recurrence(), preparation()]
    while halves:
        for gen in list(halves):
            try:
                next(gen)
            except StopIteration:
                halves.remove(gen)


def _delta_prepare(m_ref, bg_ref, u_s, wq_s, attn_s, kd_s, egl_s, slot, chunks):
    ii = lax.broadcasted_iota(jnp.int32, (CHUNK, CHUNK), 0)
    jj = lax.broadcasted_iota(jnp.int32, (CHUNK, CHUNK), 1)
    causal = ii >= jj
    strict = ii > jj
    eye = (ii == jj).astype(F32)
    bg3 = bg_ref[...].reshape(chunks, CHUNK, LANES)
    tril_b = jnp.broadcast_to(causal.astype(BF16), (chunks, CHUNK, CHUNK))
    p1 = bg3.astype(BF16)
    r1 = bg3 - p1.astype(F32)
    p2 = r1.astype(BF16)
    p3 = (r1 - p2.astype(F32)).astype(BF16)
    gc3 = _bmm(tril_b, p1) + _bmm(tril_b, p2) + _bmm(tril_b, p3)
    yield

    def heads(col):
        lo = col * DN_WIDTH
        return jnp.concatenate([m_ref[:, lo + hd * HEAD_DIM:lo + (hd + 1) * HEAD_DIM].astype(F32)
                                .reshape(chunks, CHUNK, HEAD_DIM) for hd in range(DN_HEADS)], axis=0)
    q = heads(COL_Q)
    k = heads(COL_K)
    v = heads(COL_V)
    bet = jnp.concatenate([bg3[:, :, hd:hd + 1] for hd in range(DN_HEADS)], axis=0)
    gc = jnp.concatenate([gc3[:, :, DN_HEADS + hd:DN_HEADS + hd + 1] for hd in range(DN_HEADS)], axis=0)
    gc_t = [gc3[c].T for c in range(chunks)]
    decay = jnp.stack([
        jnp.exp(jnp.where(causal, gc3[c][:, DN_HEADS + hd:DN_HEADS + hd + 1]
                          - gc_t[c][DN_HEADS + hd:DN_HEADS + hd + 1, :], -jnp.inf))
        for hd in range(DN_HEADS) for c in range(chunks)], axis=0)

    kb = k * bet
    g_all = jnp.einsum('nid,njd->nij', jnp.concatenate([kb, q], axis=1).astype(BF16), k.astype(BF16),
                       preferred_element_type=F32)
    yield
    a_low = jnp.where(strict, g_all[:, :CHUNK] * decay, 0.0)
    attn = (g_all[:, CHUNK:] * decay).astype(BF16)

    l_mat = eye + a_low
    l_bf = l_mat.astype(BF16)
    t_mat = eye - a_low
    for _ in range(4):
        res = eye - _bmm(l_bf, t_mat.astype(BF16))
        yield
        t_mat = t_mat + _bmm(t_mat.astype(BF16), res.astype(BF16))
        yield
    l_lo = (l_mat - l_bf.astype(F32)).astype(BF16)
    t_hi = t_mat.astype(BF16)
    t_lo = (t_mat - t_hi.astype(F32)).astype(BF16)
    res = eye - (_bmm(l_bf, t_hi) + _bmm(l_lo, t_hi) + _bmm(l_bf, t_lo))
    yield
    t_mat = t_mat + _bmm(t_hi, res.astype(BF16))
    yield

    eg = jnp.exp(gc)
    uw = _bmm(t_mat.astype(BF16), jnp.concatenate([v * bet, kb * eg], axis=2).astype(BF16))
    yield
    u = uw[:, :, :HEAD_DIM]
    wq_lhs = jnp.concatenate([uw[:, :, HEAD_DIM:], q * eg], axis=1).astype(BF16)
    g_last = gc[:, CHUNK - 1:CHUNK, :]
    k_dec = (k * jnp.exp(g_last - gc)).astype(BF16)
    eg_last = jnp.exp(g_last)

    u_s[slot] = u
    wq_s[slot] = wq_lhs
    attn_s[slot] = attn
    kd_s[slot] = k_dec
    egl_s[slot] = jnp.broadcast_to(eg_last, egl_s.shape[1:])


def _delta(mixed, bg, s0, gain, chunks):
    bsz, seq, _ = mixed.shape
    rows = chunks * CHUNK
    assert seq % rows == 0

    nj = seq // rows
    nb = DN_HEADS * chunks

    def prep(width):
        return pl.BlockSpec((None, rows, width), lambda b, j: (b, jnp.minimum(j, nj - 1), 0))

    def scan(width, col=0):
        return pl.BlockSpec((None, rows, width), lambda b, j: (b, jnp.maximum(j - 1, 0), col))

    state_shape = (DN_HEADS, HEAD_DIM, HEAD_DIM)
    return pl.pallas_call(
        functools.partial(_delta_kernel, chunks=chunks),
        grid=(bsz, nj + 1),
        in_specs=[prep(mixed.shape[2]), scan(DN_WIDTH, COL_Z), prep(LANES), _full_spec(state_shape),
                  _full_spec(gain.shape)],
        out_specs=[scan(DN_WIDTH), pl.BlockSpec((None,) + state_shape, lambda b, j: (b, 0, 0, 0))],
        out_shape=[jax.ShapeDtypeStruct((bsz, seq, DN_WIDTH), BF16),
                   jax.ShapeDtypeStruct((bsz,) + state_shape, F32)],
        scratch_shapes=[pltpu.VMEM(state_shape, F32),
                        pltpu.VMEM((2, nb, CHUNK, HEAD_DIM), F32),
                        pltpu.VMEM((2, nb, 2 * CHUNK, HEAD_DIM), BF16),
                        pltpu.VMEM((2, nb, CHUNK, CHUNK), BF16),
                        pltpu.VMEM((2, nb, CHUNK, HEAD_DIM), BF16),
                        pltpu.VMEM((2, nb, SUBLANES, HEAD_DIM), F32)],
        compiler_params=pltpu.CompilerParams(dimension_semantics=("parallel", "arbitrary"),
                                             vmem_limit_bytes=VMEM_LIMIT),
        name="delta",
    )(mixed, mixed, bg, s0, gain)


def _mix_out_kernel(h_ref, o_ref, c_ref, wo_ref, g1_ref, b1_ref, wrh_ref, wrl_ref, br_ref,
                    dww_ref, dwb_ref, cvg_ref, cvb_ref, hc_in_ref,
                    h1_ref, h1p_ref, route_ref, route_t_ref, cnt_out_ref, cnt_ref, c_ext, *, alpha, tiles_per_seq):
    tm = h_ref.shape[0]

    @pl.when(pl.program_id(0) == 0)
    def _():
        cnt_ref[...] = jnp.zeros_like(cnt_ref)

    @pl.when(pl.program_id(0) % tiles_per_seq == 0)
    def _():
        c_ext[0:CONF_HALO, :] = hc_in_ref[...]

    th = min(tm, SUB_ROWS_OUT)
    n_slabs = h1p_ref.shape[0] // tm
    for r0 in range(0, tm, th):
        h_v, o_v, c_v, h1_v, route_v = _row_views((h_ref, o_ref, c_ref, h1_ref, route_ref), r0, th)
        cext_v, = _row_views((c_ext,), r0, th, CONF_HALO)
        h1p_v, = _row_views((h1p_ref,), r0 * n_slabs, th * n_slabs)
        _mix_out_rows(h_v, o_v, c_v, wo_ref, g1_ref, b1_ref, wrh_ref, wrl_ref, br_ref, dww_ref, dwb_ref, cvg_ref,
                      cvb_ref, h1_v, h1p_v, route_v, route_t_ref.at[:, pl.ds(r0, th)], cnt_ref, cext_v, alpha)
    c_ext[0:CONF_HALO, :] = c_ext[tm:tm + CONF_HALO, :]
    cnt_out_ref[...] = jnp.broadcast_to(cnt_ref[...], cnt_out_ref.shape)


def _mix_out_rows(h_ref, o_ref, c_ref, wo_ref, g1_ref, b1_ref, wrh_ref, wrl_ref, br_ref, dww_ref, dwb_ref, cvg_ref,
                  cvb_ref, h1_ref, h1p_ref, route_ref, route_t_ref, cnt_ref, c_ext, alpha):
    tm = h_ref.shape[0]

    c_ext[CONF_HALO:CONF_HALO + tm, :] = c_ref[...].astype(F32)
    conv = _causal_depthwise(c_ext, dww_ref, CONF_KERNEL, CONF_HALO, tm) + dwb_ref[...]
    conf = _silu(_layer_norm(conv, cvg_ref[...], cvb_ref[...]))

    h = h_ref[...]
    dn = o_ref.shape[1]
    mix = _dot(o_ref[...].astype(BF16), wo_ref[0:dn, :]) + _dot(conf.astype(BF16), wo_ref[dn:, :])
    h1 = _layer_norm(alpha * h + mix, g1_ref[...], b1_ref[...])
    h1_ref[...] = h1
    _store_slabs(h1p_ref, _pack_bf16_pairs(h1))

    hh, hl = _split2(h1)
    logits = _dot(hh, wrh_ref[...]) + _dot(hl, wrh_ref[...]) + _dot(hh, wrl_ref[...]) + br_ref[...]
    lane = lax.broadcasted_iota(jnp.int32, logits.shape, 1).astype(F32)
    big = float(LANES)
    neg = -jnp.inf

    def first_argmax(vals):
        top = jnp.max(vals, axis=-1, keepdims=True)
        return top, jnp.min(jnp.where(vals == top, lane, big), axis=-1, keepdims=True)

    grp = jnp.where(lane < N_GROUPS, logits, neg)
    g_top, g_sel = first_argmax(grp)
    p_group = 1.0 / jnp.sum(jnp.exp(grp - g_top), axis=-1, keepdims=True)
    lo = N_GROUPS + EXPERTS_PER_GROUP * g_sel
    in_grp = jnp.where((lane >= lo) & (lane < lo + EXPERTS_PER_GROUP), logits, neg)
    m1, i1 = first_argmax(in_grp)
    m2, i2 = first_argmax(jnp.where(lane == i1, neg, in_grp))
    s = jnp.exp(m2 - m1)
    w1 = p_group / (1.0 + s)
    w2 = p_group * s / (1.0 + s)
    e1 = i1 - N_GROUPS
    e2 = i2 - N_GROUPS

    tm = logits.shape[0]
    oh1 = (lane == e1).astype(F32)
    oh2 = (lane == e2).astype(F32)
    both = oh1 + oh2
    ti = lax.broadcasted_iota(jnp.int32, (tm, tm), 0)
    tj = lax.broadcasted_iota(jnp.int32, (tm, tm), 1)
    base = _dot((ti > tj).astype(BF16), both.astype(BF16)) + cnt_ref[...]
    r1 = jnp.sum(oh1 * base, axis=-1, keepdims=True)
    r2 = jnp.sum(oh2 * base, axis=-1, keepdims=True)
    cnt_ref[...] = cnt_ref[...] + jnp.sum(both, axis=0, keepdims=True)

    vals = (e1, e2, w1, w2, r1, r2)
    route = jnp.zeros_like(logits)
    for idx, val in enumerate(vals):
        route = jnp.where(lane == idx, val, route)
    route_ref[...] = route
    route_t_ref[...] = route.T[0:SUBLANES, :]


def _mix_out(h2d, o2d, mixed2d, p, halo_c, seq, tm, alpha):
    n, d = h2d.shape
    assert seq % tm == 0
    slabs = d // (2 * LANES)
    conf_w = p['dw_w'].shape[1]

    def row(width):
        return pl.BlockSpec((tm, width), lambda i: (i, 0))

    consts = [p['w_out'], p['ln1_g'], p['ln1_b'], p['w_r_hi'], p['w_r_lo'], p['b_r'],
              p['dw_w'], p['dw_b'], p['cv_g'], p['cv_b'], halo_c]
    return pl.pallas_call(
        functools.partial(_mix_out_kernel, alpha=alpha, tiles_per_seq=seq // tm),
        grid=(n // tm,),
        in_specs=[row(d), row(o2d.shape[1]), pl.BlockSpec((tm, conf_w), lambda i: (i, COL_C))]
                 + [_full_spec(c.shape) for c in consts],
        out_specs=[row(d), pl.BlockSpec((tm * slabs, LANES), lambda i: (i, 0)), row(LANES),
                   pl.BlockSpec((SUBLANES, tm), lambda i: (0, i)), _full_spec((SUBLANES, LANES))],
        out_shape=[jax.ShapeDtypeStruct((n, d), F32), jax.ShapeDtypeStruct((n * slabs, LANES), jnp.uint32),
                   jax.ShapeDtypeStruct((n, LANES), F32),
                   jax.ShapeDtypeStruct((SUBLANES, n), F32), jax.ShapeDtypeStruct((SUBLANES, LANES), F32)],
        scratch_shapes=[pltpu.VMEM((1, LANES), F32), pltpu.VMEM((CONF_HALO + tm, conf_w), F32)],
        compiler_params=pltpu.CompilerParams(dimension_semantics=("arbitrary",), vmem_limit_bytes=VMEM_LIMIT),
        name="mix_out",
    )(h2d, o2d, mixed2d, *consts)


def _dispatch_kernel(dest_ref, pad_lo_ref, pad_hi_ref, h1s_hbm, xs_hbm, ring, zslab, fsem, ssem, zsem, *,
                     td, n_slabs, n_tokens):
    i = pl.program_id(0)
    nb = pl.num_programs(0)
    slot = i % RING
    tile_rows = td * n_slabs

    def fetch(step):
        start = pl.multiple_of(step * tile_rows, tile_rows)
        return pltpu.make_async_copy(h1s_hbm.at[pl.ds(start, tile_rows), :], ring.at[step % RING],
                                     fsem.at[step % RING])

    def wait_scatter(step):
        for _ in range(TOP_K):
            pltpu.make_async_copy(ring.at[step % RING], xs_hbm.at[pl.ds(0, tile_rows), :],
                                  ssem.at[step % RING]).wait()

    zrows = zslab.shape[0] // n_slabs

    def pad_copy(row, size):
        return pltpu.make_async_copy(
            zslab.at[pl.ds(0, size * n_slabs), :],
            xs_hbm.at[pl.ds(pl.multiple_of(row * n_slabs, n_slabs), size * n_slabs), :], zsem.at[0])

    def for_each_pad_copy(fn):
        def per_expert(e, carry):
            lo = pad_lo_ref[e]
            length = pad_hi_ref[e] - lo
            for b in range(zrows.bit_length()):
                size = 1 << b

                @pl.when(((length >> b) & 1) == 1)
                def _():
                    fn(pad_copy(lo + (length & (size - 1)), size))
            return carry
        lax.fori_loop(0, N_EXPERTS, per_expert, 0)
        tail_lo = pad_lo_ref[N_EXPERTS]

        def per_piece(piece, carry):
            fn(pad_copy(tail_lo + piece * zrows, zrows))
            return carry
        lax.fori_loop(0, (pad_hi_ref[N_EXPERTS] - tail_lo) // zrows, per_piece, 0)

    @pl.when(i == 0)
    def _():
        fetch(0).start()
        zslab[...] = jnp.zeros_like(zslab)
        for_each_pad_copy(lambda cp: cp.start())

    @pl.when((i == 0) & (nb > 1))
    def _():
        fetch(1).start()

    fetch(i).wait()

    def issue_body(r, carry):
        src = ring.at[slot, pl.ds(pl.multiple_of(r * n_slabs, n_slabs), n_slabs), :]
        for k in range(TOP_K):
            dst_row = pl.multiple_of(dest_ref[k * n_tokens + i * td + r], n_slabs)
            pltpu.make_async_copy(src, xs_hbm.at[pl.ds(dst_row, n_slabs), :], ssem.at[slot]).start(priority=k)
        return carry
    lax.fori_loop(0, td, issue_body, 0, unroll=DMA_UNROLL)

    @pl.when(i > 0)
    def _():
        wait_scatter(i - 1)

    @pl.when(i + 2 < nb)
    def _():
        fetch(i + 2).start()

    @pl.when(i == nb - 1)
    def _():
        wait_scatter(i)
        for_each_pad_copy(lambda cp: cp.wait())


def _dispatch(dest_rows, pad_lo, pad_hi, h1s, cap_rows, td, n_slabs, bm):
    n = h1s.shape[0] // n_slabs
    assert n % td == 0
    grid_spec = pltpu.PrefetchScalarGridSpec(
        num_scalar_prefetch=3,
        grid=(n // td,),
        in_specs=[pl.BlockSpec(memory_space=pl.ANY)],
        out_specs=pl.BlockSpec(memory_space=pl.ANY),
        scratch_shapes=[pltpu.VMEM((RING, td * n_slabs, LANES), h1s.dtype),
                        pltpu.VMEM((bm // 2 * n_slabs, LANES), h1s.dtype),
                        pltpu.SemaphoreType.DMA((RING,)), pltpu.SemaphoreType.DMA((RING,)),
                        pltpu.SemaphoreType.DMA((1,))],
    )
    return pl.pallas_call(
        functools.partial(_dispatch_kernel, td=td, n_slabs=n_slabs, n_tokens=n),
        grid_spec=grid_spec,
        out_shape=jax.ShapeDtypeStruct((cap_rows * n_slabs, LANES), h1s.dtype),
        compiler_params=pltpu.CompilerParams(dimension_semantics=("arbitrary",), disable_bounds_checks=True),
        name="dispatch",
    )(dest_rows, pad_lo, pad_hi, h1s)


def _expert_kernel(be_ref, nu_ref, xs_ref, wg_ref, wu_ref, wd_ref, y_ref, wg_bf, wu_bf, wd_bf, *, bm, n_slabs):
    i = pl.program_id(0)
    used = i < nu_ref[0]

    @pl.when(used & ((i == 0) | (be_ref[i] != be_ref[jnp.maximum(i - 1, 0)])))
    def _():
        wg_bf[...] = wg_ref[...].astype(BF16)
        wu_bf[...] = wu_ref[...].astype(BF16)
        wd_bf[...] = wd_ref[...].astype(BF16)

    @pl.when(used)
    def _():
        xb = _unpack_bf16_pairs(_load_slabs(xs_ref, bm, n_slabs)).astype(BF16)
        hid = _silu(_dot(xb, wg_bf[...])) * _dot(xb, wu_bf[...])
        _store_slabs(y_ref, _pack_bf16_pairs(_dot(hid.astype(BF16), wd_bf[...])))

    @pl.when(jnp.logical_not(used))
    def _():
        y_ref[...] = jnp.zeros_like(y_ref)


def _experts(block_expert, n_used, xs, w_gate, w_up, w_down, bm):
    n_blocks = block_expert.shape[0]
    d = w_gate.shape[1]
    ff = w_gate.shape[2]
    n_slabs = d // (2 * LANES)

    def blk(i, be, nu):
        return jnp.minimum(i, nu[0] - 1)

    grid_spec = pltpu.PrefetchScalarGridSpec(
        num_scalar_prefetch=2,
        grid=(n_blocks,),
        in_specs=[pl.BlockSpec((bm * n_slabs, LANES), lambda i, be, nu: (blk(i, be, nu), 0)),
                  pl.BlockSpec((None, d, ff), lambda i, be, nu: (be[blk(i, be, nu)], 0, 0)),
                  pl.BlockSpec((None, d, ff), lambda i, be, nu: (be[blk(i, be, nu)], 0, 0)),
                  pl.BlockSpec((None, ff, d), lambda i, be, nu: (be[blk(i, be, nu)], 0, 0))],
        out_specs=pl.BlockSpec((bm * n_slabs, LANES), lambda i, be, nu: (i, 0)),
        scratch_shapes=[pltpu.VMEM((d, ff), BF16), pltpu.VMEM((d, ff), BF16), pltpu.VMEM((ff, d), BF16)],
    )
    return pl.pallas_call(
        functools.partial(_expert_kernel, bm=bm, n_slabs=n_slabs),
        grid_spec=grid_spec,
        out_shape=jax.ShapeDtypeStruct(xs.shape, xs.dtype),
        compiler_params=pltpu.CompilerParams(dimension_semantics=("arbitrary",), vmem_limit_bytes=VMEM_LIMIT),
        name="experts",
    )(block_expert, n_used, xs, w_gate, w_up, w_down)


def _combine_kernel(dest_ref, y_hbm, h1_ref, route_ref, g2_ref, b2_ref, out_ref, ybuf, sem, *, tm, n_slabs, alpha):
    i = pl.program_id(0)
    nb = pl.num_programs(0)
    part = tm * n_slabs

    def issue(blk, slot):
        def body(r, carry):
            for k in range(TOP_K):
                src_row = pl.multiple_of(dest_ref[k * (nb * tm) + blk * tm + r], n_slabs)
                dst_row = pl.multiple_of((slot * TOP_K + k) * part + r * n_slabs, n_slabs)
                pltpu.make_async_copy(y_hbm.at[pl.ds(src_row, n_slabs), :], ybuf.at[pl.ds(dst_row, n_slabs), :],
                                      sem.at[slot]).start(priority=k)
            return carry
        lax.fori_loop(0, tm, body, 0, unroll=DMA_UNROLL)

    @pl.when(i == 0)
    def _():
        issue(0, 0)

    @pl.when(i + 1 < nb)
    def _():
        issue(i + 1, (i + 1) % 2)

    slot = i % 2
    base = pl.multiple_of(slot * (TOP_K * part), TOP_K * part)
    pltpu.make_async_copy(y_hbm.at[pl.ds(0, TOP_K * part), :], ybuf.at[pl.ds(base, TOP_K * part), :],
                          sem.at[slot]).wait()

    route = route_ref[...]
    ffn = (_unpack_bf16_pairs(_load_slabs(ybuf, tm, n_slabs, base)) * route[:, 2:3]
           + _unpack_bf16_pairs(_load_slabs(ybuf, tm, n_slabs, base + part)) * route[:, 3:4])
    out_ref[...] = _layer_norm(alpha * h1_ref[...] + ffn, g2_ref[...], b2_ref[...])


def _combine(dest_rows, y_sorted, h1, route, ln2_g, ln2_b, tm, alpha):
    n, d = h1.shape
    n_slabs = d // (2 * LANES)
    assert n % tm == 0
    grid_spec = pltpu.PrefetchScalarGridSpec(
        num_scalar_prefetch=1,
        grid=(n // tm,),
        in_specs=[pl.BlockSpec(memory_space=pl.ANY),
                  pl.BlockSpec((tm, d), lambda i, dest: (i, 0)),
                  pl.BlockSpec((tm, LANES), lambda i, dest: (i, 0)),
                  pl.BlockSpec((1, d), lambda i, dest: (0, 0)),
                  pl.BlockSpec((1, d), lambda i, dest: (0, 0))],
        out_specs=pl.BlockSpec((tm, d), lambda i, dest: (i, 0)),
        scratch_shapes=[pltpu.VMEM((2 * TOP_K * tm * n_slabs, LANES), y_sorted.dtype),
                        pltpu.SemaphoreType.DMA((2,))],
    )
    return pl.pallas_call(
        functools.partial(_combine_kernel, tm=tm, n_slabs=n_slabs, alpha=alpha),
        grid_spec=grid_spec,
        out_shape=jax.ShapeDtypeStruct((n, d), F32),
        compiler_params=pltpu.CompilerParams(dimension_semantics=("arbitrary",), vmem_limit_bytes=VMEM_LIMIT,
                                             disable_bounds_checks=True),
        name="combine",
    )(dest_rows, y_sorted, h1, route, ln2_g, ln2_b)


def _dispatch_plan(route_t, counts, bm, n_slabs):
    n = route_t.shape[1]
    expert_id = route_t[0:TOP_K].astype(jnp.int32)
    rank = route_t[4:4 + TOP_K].astype(jnp.int32)
    padded = (counts + bm - 1) // bm * bm
    pad_end = jnp.cumsum(padded)
    pad_start = pad_end - padded
    experts = jnp.arange(N_EXPERTS, dtype=jnp.int32)[:, None, None]
    dest = jnp.sum(jnp.where(expert_id[None] == experts, pad_start[:, None, None], 0), axis=0) + rank
    n_blocks = (n * TOP_K + bm - 1) // bm + N_EXPERTS
    block_start = jnp.arange(n_blocks, dtype=jnp.int32) * bm
    block_expert = jnp.minimum(jnp.sum((block_start[:, None] >= pad_end[None, :]).astype(jnp.int32), axis=1),
                               N_EXPERTS - 1).astype(jnp.int32)
    n_used = (pad_end[-1:] // bm).astype(jnp.int32)
    dest_rows = (dest * n_slabs).reshape(-1).astype(jnp.int32)
    pad_lo = jnp.concatenate([pad_start + counts, pad_end[-1:]]).astype(jnp.int32)
    pad_hi = jnp.concatenate([pad_end, jnp.full((1,), n_blocks * bm, pad_end.dtype)]).astype(jnp.int32)
    return dest_rows, pad_lo, pad_hi, block_expert, n_used, n_blocks


def _pad_lanes(w, width=LANES):
    return jnp.pad(w, [(0, 0)] * (w.ndim - 1) + [(0, width - w.shape[-1])])


def kernel(x, meta_tokens, ln_emb_g, ln_emb_b, w_in, conv_qkv_w, a_log, dt_bias, dn_norm_g, conv_dw_w, conv_dw_b, cv_norm_g, cv_norm_b, w_out, ln1_g, ln1_b, w_group, b_group, w_router, b_router, w_exp_gate, w_exp_up, w_exp_down, ln2_g, ln2_b):
    depth = w_in.shape[0]
    assert depth == 1, "single-layer block"
    bsz, seq, d = x.shape
    alpha = (2.0 * depth) ** 0.25
    qkv_w = 3 * DN_WIDTH
    w_in0 = w_in[0]
    glu_off = 4 * DN_WIDTH + 2 * DN_HEADS
    row = lambda a: a.reshape(1, -1).astype(F32)
    p = {
        'ln_emb_g': row(ln_emb_g), 'ln_emb_b': row(ln_emb_b),
        'w_qkv': w_in0[:, :qkv_w].astype(BF16),
        'w_z': w_in0[:, qkv_w:4 * DN_WIDTH].astype(BF16),
        'w_ba': _pad_lanes(w_in0[:, 4 * DN_WIDTH:glu_off]).astype(BF16),
        'w_glu': w_in0[:, glu_off:].astype(BF16),
        'conv_w': conv_qkv_w[0].astype(F32),
        'neg_a': _pad_lanes(jnp.concatenate([jnp.zeros((DN_HEADS,), F32), -jnp.exp(a_log[0].astype(F32))])[None]),
        'dt_b': _pad_lanes(jnp.concatenate([jnp.zeros((DN_HEADS,), F32), dt_bias[0].astype(F32)])[None]),
        'dw_w': conv_dw_w[0].astype(F32), 'dw_b': row(conv_dw_b[0]),
        'cv_g': row(cv_norm_g[0]), 'cv_b': row(cv_norm_b[0]),
        'w_out': w_out[0].astype(BF16), 'ln1_g': row(ln1_g[0]), 'ln1_b': row(ln1_b[0]),
    }
    w_r = _pad_lanes(jnp.concatenate([w_group[0], w_router[0]], axis=1).astype(F32))
    p['w_r_hi'] = w_r.astype(BF16)
    p['w_r_lo'] = (w_r - p['w_r_hi'].astype(F32)).astype(BF16)
    p['b_r'] = _pad_lanes(jnp.concatenate([b_group[0], b_router[0]])[None].astype(F32))
    gain = row(dn_norm_g[0])

    conf_w = p['dw_w'].shape[1]
    zero_hq = jnp.zeros((QKV_HALO, qkv_w), F32)
    zero_hc = jnp.zeros((CONF_HALO, conf_w), F32)
    m_mixed, mbg, halo_q, halo_c, _ = _mix_in(meta_tokens[None].astype(F32), p, zero_hq, zero_hc, N_META)
    front = lambda a: jnp.pad(a, [(0, 0), (CHUNK - N_META, 0), (0, 0)])
    s_zero = jnp.zeros((DN_HEADS, HEAD_DIM, HEAD_DIM), F32)
    _, s_meta = _delta(front(m_mixed), front(mbg), s_zero, gain, 1)

    mixed, bg, _, _, h = _mix_in(x, p, halo_q[0], halo_c[0], TM_IN)
    o, _ = _delta(mixed, bg, s_meta[0], gain, DELTA_CHUNKS)

    n = bsz * seq
    h1, h1p, route, route_t, cnt = _mix_out(h.reshape(n, d), o.reshape(n, DN_WIDTH),
                                            mixed.reshape(n, mixed.shape[2]), p, halo_c[0], seq, TM_OUT, alpha)

    n_slabs = d // (2 * LANES)
    counts = cnt[0, :N_EXPERTS].astype(jnp.int32)
    dest_rows, pad_lo, pad_hi, block_expert, n_used, n_blocks = _dispatch_plan(route_t, counts, BM_EXPERT, n_slabs)
    xs = _dispatch(dest_rows, pad_lo, pad_hi, h1p, n_blocks * BM_EXPERT, TD_DISPATCH, n_slabs, BM_EXPERT)
    first_layer = lambda w: w.reshape(w.shape[1:])
    y_sorted = _experts(block_expert, n_used, xs, first_layer(w_exp_gate), first_layer(w_exp_up),
                        first_layer(w_exp_down), BM_EXPERT)
    out = _combine(dest_rows, y_sorted, h1, route, row(ln2_g[0]), row(ln2_b[0]), TM_COMBINE, alpha)
    return out.reshape(bsz, seq, d)
```

```python
import functools

import jax
import jax.numpy as jnp
from jax import lax
from jax.experimental import pallas as pl
from jax.experimental.pallas import tpu as pltpu

F32 = jnp.float32
BF16 = jnp.bfloat16

NORM_EPS = 1e-5
N_META = 16
DN_HEADS = 4
HEAD_DIM = 128
DN_WIDTH = DN_HEADS * HEAD_DIM
CHUNK = 64
SHORT_CONV = 4
CONF_KERNEL = 31
N_GROUPS = 4
EXPERTS_PER_GROUP = 8
N_EXPERTS = N_GROUPS * EXPERTS_PER_GROUP
TOP_K = 2
LANES = 128
SUBLANES = 8
QKV_HALO = 8
CONF_HALO = 32
VMEM_LIMIT = 56 * 1024 * 1024

TM_IN = 512
DELTA_CHUNKS = 8
TM_OUT = 1024
BM_EXPERT = 512
TM_COMBINE = 1024
TD_DISPATCH = 1024
RING = 3
SUB_ROWS = 256
SUB_ROWS_OUT = 512
CONV_BLOCK_ROWS = 64
CONV_BLOCK_COLS = 512
MIXED_COLS = 5
COL_Q, COL_K, COL_V, COL_Z, COL_C = range(MIXED_COLS)
DMA_UNROLL = 8


def _dot(a, b):
    return jnp.dot(a, b, preferred_element_type=F32)


def _split2(x):
    hi = x.astype(BF16)
    lo = (x - hi.astype(F32)).astype(BF16)
    return hi, lo


def _dot_hilo(a, b):
    ah, al = _split2(a)
    bh, bl = _split2(b)
    return _dot(ah, bh) + _dot(al, bh) + _dot(ah, bl)


def _dot_exact01(m01, x):
    x1 = x.astype(BF16)
    r1 = x - x1.astype(F32)
    x2 = r1.astype(BF16)
    x3 = (r1 - x2.astype(F32)).astype(BF16)
    return _dot(m01, x1) + _dot(m01, x2) + _dot(m01, x3)


def _sigmoid(x):
    return 1.0 / (1.0 + jnp.exp(-x))


def _silu(x):
    return x * _sigmoid(x)


def _layer_norm(x, g, b):
    mu = jnp.mean(x, axis=-1, keepdims=True)
    xc = x - mu
    var = jnp.mean(xc * xc, axis=-1, keepdims=True)
    return xc * lax.rsqrt(var + NORM_EPS) * g + b


def _pack_bf16_pairs(x):
    half = x.shape[1] // 2
    lo = lax.bitcast_convert_type(x[:, :half].astype(BF16).astype(F32), jnp.uint32)
    hi = lax.bitcast_convert_type(x[:, half:].astype(BF16).astype(F32), jnp.uint32)
    return (lo >> 16) | (hi & jnp.uint32(0xFFFF0000))


def _unpack_bf16_pairs(p):
    lo = lax.bitcast_convert_type(p << 16, F32)
    hi = lax.bitcast_convert_type(p & jnp.uint32(0xFFFF0000), F32)
    return jnp.concatenate([lo, hi], axis=1)


def _store_slabs(ref, val):
    rows, d = val.shape
    n_slabs = d // LANES
    for s in range(n_slabs):
        ref[pl.ds(s, rows, stride=n_slabs), :] = val[:, s * LANES:(s + 1) * LANES]


def _load_slabs(ref, rows, n_slabs, base=0):
    return jnp.concatenate([ref[pl.ds(base + s, rows, stride=n_slabs), :] for s in range(n_slabs)], axis=1)


def _full_spec(shape):
    nd = len(shape)
    return pl.BlockSpec(shape, lambda *_: (0,) * nd)


def _causal_depthwise(ext_ref, w_ref, n_taps, halo, tm):
    ext = ext_ref[...]
    rows, cols = ext.shape
    first = halo - (n_taps - 1)
    groups = ext.reshape(rows // SUBLANES, SUBLANES, cols)
    sub = lax.broadcasted_iota(jnp.int32, (1, SUBLANES, 1), 1)

    def shifted(phase):
        rolled = pltpu.roll(groups, SUBLANES - phase, axis=1)
        nxt = jnp.concatenate([rolled[1:], rolled[:1]], axis=0)
        return jnp.where(sub < SUBLANES - phase, rolled, nxt).reshape(rows, cols)

    phases = {}
    for k in range(n_taps):
        phase = (first + k) % SUBLANES
        if phase not in phases:
            phases[phase] = ext if phase == 0 else shifted(phase)

    rb = min(tm, CONV_BLOCK_ROWS)
    cb = min(cols, CONV_BLOCK_COLS)
    out_rows = []
    for r0 in range(0, tm, rb):
        out_cols = []
        for c0 in range(0, cols, cb):
            acc = None
            for k in range(n_taps):
                phase = (first + k) % SUBLANES
                base = first + k - phase + r0
                term = phases[phase][base:base + rb, c0:c0 + cb] * w_ref[k:k + 1, c0:c0 + cb]
                acc = term if acc is None else acc + term
            out_cols.append(acc)
        out_rows.append(out_cols[0] if len(out_cols) == 1 else jnp.concatenate(out_cols, axis=1))
    return out_rows[0] if len(out_rows) == 1 else jnp.concatenate(out_rows, axis=0)


def _row_views(refs, r0, rows, lead=0):
    return [ref.at[pl.ds(r0, lead + rows), :] for ref in refs]


def _mix_in_kernel(x_ref, lng_ref, lnb_ref, wqkv_ref, wz_ref, wglu_ref, wba_ref, cw_ref, nega_ref,
                   dtb_ref, hq_in_ref, hc_in_ref,
                   m_ref, bg_ref, hq_out_ref, hc_out_ref, h_ref,
                   qkv_ext, c_ext):
    tm = x_ref.shape[0]

    @pl.when(pl.program_id(1) == 0)
    def _():
        qkv_ext[0:QKV_HALO, :] = hq_in_ref[...]
        c_ext[0:CONF_HALO, :] = hc_in_ref[...]

    th = min(tm, SUB_ROWS)
    for r0 in range(0, tm, th):
        x_v, bg_v, h_v = _row_views((x_ref, bg_ref, h_ref), r0, th)
        q_v, k_v, v_v, z_v, c_v = (m_ref.at[pl.ds(r0, th), pl.ds(col * DN_WIDTH, DN_WIDTH)]
                                   for col in range(MIXED_COLS))
        qkv_v, = _row_views((qkv_ext,), r0, th, QKV_HALO)
        cext_v, = _row_views((c_ext,), r0, th, CONF_HALO)
        _mix_in_rows(x_v, lng_ref, lnb_ref, wqkv_ref, wz_ref, wglu_ref, wba_ref, cw_ref, nega_ref, dtb_ref,
                     q_v, k_v, v_v, z_v, c_v, bg_v, h_v, qkv_v, cext_v)

    q_tail = qkv_ext[tm:tm + QKV_HALO, :]
    c_tail = c_ext[tm:tm + CONF_HALO, :]
    qkv_ext[0:QKV_HALO, :] = q_tail
    c_ext[0:CONF_HALO, :] = c_tail
    hq_out_ref[...] = q_tail
    hc_out_ref[...] = c_tail


def _mix_in_rows(x_ref, lng_ref, lnb_ref, wqkv_ref, wz_ref, wglu_ref, wba_ref, cw_ref, nega_ref, dtb_ref,
                 q_ref, k_ref, v_ref, z_ref, c_ref, bg_ref, h_ref, qkv_ext, c_ext):
    tm = x_ref.shape[0]
    h = _layer_norm(x_ref[...], lng_ref[...], lnb_ref[...])
    h_ref[...] = h
    hb = h.astype(BF16)

    qkv_ext[QKV_HALO:QKV_HALO + tm, :] = _dot(hb, wqkv_ref[...])
    qkv = _silu(_causal_depthwise(qkv_ext, cw_ref, SHORT_CONV, QKV_HALO, tm))
    for hd in range(DN_HEADS):
        lo = hd * HEAD_DIM
        qh = qkv[:, lo:lo + HEAD_DIM]
        kh = qkv[:, DN_WIDTH + lo:DN_WIDTH + lo + HEAD_DIM]
        q_ref[:, lo:lo + HEAD_DIM] = (qh * (lax.rsqrt(jnp.sum(qh * qh, axis=-1, keepdims=True) + 1e-6)
                                            * (HEAD_DIM ** -0.5))).astype(q_ref.dtype)
        k_ref[:, lo:lo + HEAD_DIM] = (kh * lax.rsqrt(jnp.sum(kh * kh, axis=-1, keepdims=True) + 1e-6)
                                      ).astype(k_ref.dtype)
    v_ref[...] = qkv[:, 2 * DN_WIDTH:].astype(v_ref.dtype)
    z_ref[...] = _dot(hb, wz_ref[...]).astype(z_ref.dtype)

    ba = _dot(hb, wba_ref[...])
    lane = lax.broadcasted_iota(jnp.int32, ba.shape, 1)
    sp_in = ba + dtb_ref[...]
    softplus = jnp.maximum(sp_in, 0.0) + jnp.log(1.0 + jnp.exp(-jnp.abs(sp_in)))
    bg_ref[...] = jnp.where(lane < DN_HEADS, _sigmoid(ba), nega_ref[...] * softplus)

    glu = _dot(hb, wglu_ref[...])
    cw = glu.shape[1] // 2
    c_pre = glu[:, :cw] * _sigmoid(glu[:, cw:])
    c_ref[...] = c_pre.astype(c_ref.dtype)
    c_ext[CONF_HALO:CONF_HALO + tm, :] = c_pre


def _mix_in(x, p, halo_q, halo_c, tm):
    bsz, seq, d = x.shape
    assert seq % tm == 0
    qkv_w = 3 * DN_WIDTH
    conf_w = p['dw_w'].shape[1]

    def row(width):
        return pl.BlockSpec((None, tm, width), lambda b, t: (b, t, 0))

    def per_batch(rows, width):
        return pl.BlockSpec((None, rows, width), lambda b, t: (b, 0, 0))

    consts = [p['ln_emb_g'], p['ln_emb_b'], p['w_qkv'], p['w_z'], p['w_glu'], p['w_ba'], p['conv_w'],
              p['neg_a'], p['dt_b'], halo_q, halo_c]
    sds = jax.ShapeDtypeStruct
    assert conf_w == DN_WIDTH, "q, k, v, z and the conformer channels share one array of equal-width column blocks"
    out_shape = [sds((bsz, seq, MIXED_COLS * DN_WIDTH), BF16),
                 sds((bsz, seq, LANES), F32), sds((bsz, QKV_HALO, qkv_w), F32),
                 sds((bsz, CONF_HALO, conf_w), F32), sds((bsz, seq, d), F32)]
    out_specs = [row(MIXED_COLS * DN_WIDTH), row(LANES), per_batch(QKV_HALO, qkv_w),
                 per_batch(CONF_HALO, conf_w), row(d)]
    return pl.pallas_call(
        _mix_in_kernel,
        grid=(bsz, seq // tm),
        in_specs=[row(d)] + [_full_spec(c.shape) for c in consts],
        out_specs=out_specs,
        out_shape=out_shape,
        scratch_shapes=[pltpu.VMEM((QKV_HALO + tm, qkv_w), F32), pltpu.VMEM((CONF_HALO + tm, conf_w), F32)],
        compiler_params=pltpu.CompilerParams(dimension_semantics=("parallel", "arbitrary"),
                                             vmem_limit_bytes=VMEM_LIMIT),
        name="mix_in",
    )(x, *consts)


def _bmm(a, b):
    return jnp.einsum('nij,njk->nik', a, b, preferred_element_type=F32)


def _delta_kernel(m_ref, z_ref, bg_ref, s0_ref, gain_ref, o_ref, sfin_ref,
                  s_ref, u_s, wq_s, attn_s, kd_s, egl_s, *, chunks):
    j = pl.program_id(1)
    slot = j % 2
    prev = 1 - slot
    gain = gain_ref[...]

    @pl.when(j == 0)
    def _():
        s_ref[...] = s0_ref[...]
        u_s[1] = jnp.zeros(u_s.shape[1:], u_s.dtype)
        wq_s[1] = jnp.zeros(wq_s.shape[1:], wq_s.dtype)
        attn_s[1] = jnp.zeros(attn_s.shape[1:], attn_s.dtype)
        kd_s[1] = jnp.zeros(kd_s.shape[1:], kd_s.dtype)
        egl_s[1] = jnp.zeros(egl_s.shape[1:], egl_s.dtype)

    def recurrence():
        live = j > 0
        state = [s_ref[hd] for hd in range(DN_HEADS)]
        for c in range(chunks):
            idx = [hd * chunks + c for hd in range(DN_HEADS)]
            wq = [_dot(wq_s[prev, n], state[hd].astype(BF16)) for hd, n in enumerate(idx)]
            yield
            v_new = [(u_s[prev, n] - wq[hd][:CHUNK]).astype(BF16) for hd, n in enumerate(idx)]
            o = [wq[hd][CHUNK:] + _dot(attn_s[prev, n], v_new[hd]) for hd, n in enumerate(idx)]
            state = [state[hd] * egl_s[prev, n][0:1, :]
                     + lax.dot_general(kd_s[prev, n], v_new[hd], (((0,), (0,)), ((), ())),
                                       preferred_element_type=F32)
                     for hd, n in enumerate(idx)]
            yield
            for hd in range(DN_HEADS):
                cols = slice(hd * HEAD_DIM, (hd + 1) * HEAD_DIM)
                rows = slice(c * CHUNK, (c + 1) * CHUNK)
                r = o[hd] * lax.rsqrt(jnp.mean(o[hd] * o[hd], axis=-1, keepdims=True) + 1e-6)
                o_ref[rows, cols] = (r * gain * _silu(z_ref[rows, cols].astype(F32))).astype(o_ref.dtype)
        for hd in range(DN_HEADS):
            kept = jnp.where(live, state[hd], s_ref[hd])
            s_ref[hd] = kept
            sfin_ref[hd] = kept

    def preparation():
        yield from _delta_prepare(m_ref, bg_ref, u_s, wq_s, attn_s, kd_s, egl_s, slot, chunks)

    halves = [recurrence(), preparation()]
    while halves:
        for gen in list(halves):
            try:
                next(gen)
            except StopIteration:
                halves.remove(gen)


def _delta_prepare(m_ref, bg_ref, u_s, wq_s, attn_s, kd_s, egl_s, slot, chunks):
    ii = lax.broadcasted_iota(jnp.int32, (CHUNK, CHUNK), 0)
    jj = lax.broadcasted_iota(jnp.int32, (CHUNK, CHUNK), 1)
    causal = ii >= jj
    strict = ii > jj
    eye = (ii == jj).astype(F32)
    bg3 = bg_ref[...].reshape(chunks, CHUNK, LANES)
    tril_b = jnp.broadcast_to(causal.astype(BF16), (chunks, CHUNK, CHUNK))
    p1 = bg3.astype(BF16)
    r1 = bg3 - p1.astype(F32)
    p2 = r1.astype(BF16)
    p3 = (r1 - p2.astype(F32)).astype(BF16)
    gc3 = _bmm(tril_b, p1) + _bmm(tril_b, p2) + _bmm(tril_b, p3)
    yield

    def heads(col):
        lo = col * DN_WIDTH
        return jnp.concatenate([m_ref[:, lo + hd * HEAD_DIM:lo + (hd + 1) * HEAD_DIM].astype(F32)
                                .reshape(chunks, CHUNK, HEAD_DIM) for hd in range(DN_HEADS)], axis=0)
    q = heads(COL_Q)
    k = heads(COL_K)
    v = heads(COL_V)
    bet = jnp.concatenate([bg3[:, :, hd:hd + 1] for hd in range(DN_HEADS)], axis=0)
    gc = jnp.concatenate([gc3[:, :, DN_HEADS + hd:DN_HEADS + hd + 1] for hd in range(DN_HEADS)], axis=0)
    gc_t = [gc3[c].T for c in range(chunks)]
    decay = jnp.stack([
        jnp.exp(jnp.where(causal, gc3[c][:, DN_HEADS + hd:DN_HEADS + hd + 1]
                          - gc_t[c][DN_HEADS + hd:DN_HEADS + hd + 1, :], -jnp.inf))
        for hd in range(DN_HEADS) for c in range(chunks)], axis=0)

    kb = k * bet
    g_all = jnp.einsum('nid,njd->nij', jnp.concatenate([kb, q], axis=1).astype(BF16), k.astype(BF16),
                       preferred_element_type=F32)
    yield
    a_low = jnp.where(strict, g_all[:, :CHUNK] * decay, 0.0)
    attn = (g_all[:, CHUNK:] * decay).astype(BF16)

    l_mat = eye + a_low
    l_bf = l_mat.astype(BF16)
    t_mat = eye - a_low
    for _ in range(4):
        res = eye - _bmm(l_bf, t_mat.astype(BF16))
        yield
        t_mat = t_mat + _bmm(t_mat.astype(BF16), res.astype(BF16))
        yield
    l_lo = (l_mat - l_bf.astype(F32)).astype(BF16)
    t_hi = t_mat.astype(BF16)
    t_lo = (t_mat - t_hi.astype(F32)).astype(BF16)
    res = eye - (_bmm(l_bf, t_hi) + _bmm(l_lo, t_hi) + _bmm(l_bf, t_lo))
    yield
    t_mat = t_mat + _bmm(t_hi, res.astype(BF16))
    yield

    eg = jnp.exp(gc)
    uw = _bmm(t_mat.astype(BF16), jnp.concatenate([v * bet, kb * eg], axis=2).astype(BF16))
    yield
    u = uw[:, :, :HEAD_DIM]
    wq_lhs = jnp.concatenate([uw[:, :, HEAD_DIM:], q * eg], axis=1).astype(BF16)
    g_last = gc[:, CHUNK - 1:CHUNK, :]
    k_dec = (k * jnp.exp(g_last - gc)).astype(BF16)
    eg_last = jnp.exp(g_last)

    u_s[slot] = u
    wq_s[slot] = wq_lhs
    attn_s[slot] = attn
    kd_s[slot] = k_dec
    egl_s[slot] = jnp.broadcast_to(eg_last, egl_s.shape[1:])


def _delta(mixed, bg, s0, gain, chunks):
    bsz, seq, _ = mixed.shape
    rows = chunks * CHUNK
    assert seq % rows == 0

    nj = seq // rows
    nb = DN_HEADS * chunks

    def prep(width):
        return pl.BlockSpec((None, rows, width), lambda b, j: (b, jnp.minimum(j, nj - 1), 0))

    def scan(width, col=0):
        return pl.BlockSpec((None, rows, width), lambda b, j: (b, jnp.maximum(j - 1, 0), col))

    state_shape = (DN_HEADS, HEAD_DIM, HEAD_DIM)
    return pl.pallas_call(
        functools.partial(_delta_kernel, chunks=chunks),
        grid=(bsz, nj + 1),
        in_specs=[prep(mixed.shape[2]), scan(DN_WIDTH, COL_Z), prep(LANES), _full_spec(state_shape),
                  _full_spec(gain.shape)],
        out_specs=[scan(DN_WIDTH), pl.BlockSpec((None,) + state_shape, lambda b, j: (b, 0, 0, 0))],
        out_shape=[jax.ShapeDtypeStruct((bsz, seq, DN_WIDTH), BF16),
                   jax.ShapeDtypeStruct((bsz,) + state_shape, F32)],
        scratch_shapes=[pltpu.VMEM(state_shape, F32),
                        pltpu.VMEM((2, nb, CHUNK, HEAD_DIM), F32),
                        pltpu.VMEM((2, nb, 2 * CHUNK, HEAD_DIM), BF16),
                        pltpu.VMEM((2, nb, CHUNK, CHUNK), BF16),
                        pltpu.VMEM((2, nb, CHUNK, HEAD_DIM), BF16),
                        pltpu.VMEM((2, nb, SUBLANES, HEAD_DIM), F32)],
        compiler_params=pltpu.CompilerParams(dimension_semantics=("parallel", "arbitrary"),
                                             vmem_limit_bytes=VMEM_LIMIT),
        name="delta",
    )(mixed, mixed, bg, s0, gain)


def _mix_out_kernel(h_ref, o_ref, c_ref, wo_ref, g1_ref, b1_ref, wrh_ref, wrl_ref, br_ref,
                    dww_ref, dwb_ref, cvg_ref, cvb_ref, hc_in_ref,
                    h1_ref, h1p_ref, route_ref, route_t_ref, cnt_out_ref, cnt_ref, c_ext, *, alpha, tiles_per_seq):
    tm = h_ref.shape[0]

    @pl.when(pl.program_id(0) == 0)
    def _():
        cnt_ref[...] = jnp.zeros_like(cnt_ref)

    @pl.when(pl.program_id(0) % tiles_per_seq == 0)
    def _():
        c_ext[0:CONF_HALO, :] = hc_in_ref[...]

    th = min(tm, SUB_ROWS_OUT)
    n_slabs = h1p_ref.shape[0] // tm
    for r0 in range(0, tm, th):
        h_v, o_v, c_v, h1_v, route_v = _row_views((h_ref, o_ref, c_ref, h1_ref, route_ref), r0, th)
        cext_v, = _row_views((c_ext,), r0, th, CONF_HALO)
        h1p_v, = _row_views((h1p_ref,), r0 * n_slabs, th * n_slabs)
        _mix_out_rows(h_v, o_v, c_v, wo_ref, g1_ref, b1_ref, wrh_ref, wrl_ref, br_ref, dww_ref, dwb_ref, cvg_ref,
                      cvb_ref, h1_v, h1p_v, route_v, route_t_ref.at[:, pl.ds(r0, th)], cnt_ref, cext_v, alpha)
    c_ext[0:CONF_HALO, :] = c_ext[tm:tm + CONF_HALO, :]
    cnt_out_ref[...] = jnp.broadcast_to(cnt_ref[...], cnt_out_ref.shape)


def _mix_out_rows(h_ref, o_ref, c_ref, wo_ref, g1_ref, b1_ref, wrh_ref, wrl_ref, br_ref, dww_ref, dwb_ref, cvg_ref,
                  cvb_ref, h1_ref, h1p_ref, route_ref, route_t_ref, cnt_ref, c_ext, alpha):
    tm = h_ref.shape[0]

    c_ext[CONF_HALO:CONF_HALO + tm, :] = c_ref[...].astype(F32)
    conv = _causal_depthwise(c_ext, dww_ref, CONF_KERNEL, CONF_HALO, tm) + dwb_ref[...]
    conf = _silu(_layer_norm(conv, cvg_ref[...], cvb_ref[...]))

    h = h_ref[...]
    dn = o_ref.shape[1]
    mix = _dot(o_ref[...].astype(BF16), wo_ref[0:dn, :]) + _dot(conf.astype(BF16), wo_ref[dn:, :])
    h1 = _layer_norm(alpha * h + mix, g1_ref[...], b1_ref[...])
    h1_ref[...] = h1
    _store_slabs(h1p_ref, _pack_bf16_pairs(h1))

    hh, hl = _split2(h1)
    logits = _dot(hh, wrh_ref[...]) + _dot(hl, wrh_ref[...]) + _dot(hh, wrl_ref[...]) + br_ref[...]
    lane = lax.broadcasted_iota(jnp.int32, logits.shape, 1).astype(F32)
    big = float(LANES)
    neg = -jnp.inf

    def first_argmax(vals):
        top = jnp.max(vals, axis=-1, keepdims=True)
        return top, jnp.min(jnp.where(vals == top, lane, big), axis=-1, keepdims=True)

    grp = jnp.where(lane < N_GROUPS, logits, neg)
    g_top, g_sel = first_argmax(grp)
    p_group = 1.0 / jnp.sum(jnp.exp(grp - g_top), axis=-1, keepdims=True)
    lo = N_GROUPS + EXPERTS_PER_GROUP * g_sel
    in_grp = jnp.where((lane >= lo) & (lane < lo + EXPERTS_PER_GROUP), logits, neg)
    m1, i1 = first_argmax(in_grp)
    m2, i2 = first_argmax(jnp.where(lane == i1, neg, in_grp))
    s = jnp.exp(m2 - m1)
    w1 = p_group / (1.0 + s)
    w2 = p_group * s / (1.0 + s)
    e1 = i1 - N_GROUPS
    e2 = i2 - N_GROUPS

    tm = logits.shape[0]
    oh1 = (lane == e1).astype(F32)
    oh2 = (lane == e2).astype(F32)
    both = oh1 + oh2
    ti = lax.broadcasted_iota(jnp.int32, (tm, tm), 0)
    tj = lax.broadcasted_iota(jnp.int32, (tm, tm), 1)
    base = _dot((ti > tj).astype(BF16), both.astype(BF16)) + cnt_ref[...]
    r1 = jnp.sum(oh1 * base, axis=-1, keepdims=True)
    r2 = jnp.sum(oh2 * base, axis=-1, keepdims=True)
    cnt_ref[...] = cnt_ref[...] + jnp.sum(both, axis=0, keepdims=True)

    vals = (e1, e2, w1, w2, r1, r2)
    route = jnp.zeros_like(logits)
    for idx, val in enumerate(vals):
        route = jnp.where(lane == idx, val, route)
    route_ref[...] = route
    route_t_ref[...] = route.T[0:SUBLANES, :]


def _mix_out(h2d, o2d, mixed2d, p, halo_c, seq, tm, alpha):
    n, d = h2d.shape
    assert seq % tm == 0
    slabs = d // (2 * LANES)
    conf_w = p['dw_w'].shape[1]

    def row(width):
        return pl.BlockSpec((tm, width), lambda i: (i, 0))

    consts = [p['w_out'], p['ln1_g'], p['ln1_b'], p['w_r_hi'], p['w_r_lo'], p['b_r'],
              p['dw_w'], p['dw_b'], p['cv_g'], p['cv_b'], halo_c]
    return pl.pallas_call(
        functools.partial(_mix_out_kernel, alpha=alpha, tiles_per_seq=seq // tm),
        grid=(n // tm,),
        in_specs=[row(d), row(o2d.shape[1]), pl.BlockSpec((tm, conf_w), lambda i: (i, COL_C))]
                 + [_full_spec(c.shape) for c in consts],
        out_specs=[row(d), pl.BlockSpec((tm * slabs, LANES), lambda i: (i, 0)), row(LANES),
                   pl.BlockSpec((SUBLANES, tm), lambda i: (0, i)), _full_spec((SUBLANES, LANES))],
        out_shape=[jax.ShapeDtypeStruct((n, d), F32), jax.ShapeDtypeStruct((n * slabs, LANES), jnp.uint32),
                   jax.ShapeDtypeStruct((n, LANES), F32),
                   jax.ShapeDtypeStruct((SUBLANES, n), F32), jax.ShapeDtypeStruct((SUBLANES, LANES), F32)],
        scratch_shapes=[pltpu.VMEM((1, LANES), F32), pltpu.VMEM((CONF_HALO + tm, conf_w), F32)],
        compiler_params=pltpu.CompilerParams(dimension_semantics=("arbitrary",), vmem_limit_bytes=VMEM_LIMIT),
        name="mix_out",
    )(h2d, o2d, mixed2d, *consts)


def _dispatch_kernel(dest_ref, pad_lo_ref, pad_hi_ref, h1s_hbm, xs_hbm, ring, zslab, fsem, ssem, zsem, *,
                     td, n_slabs, n_tokens):
    i = pl.program_id(0)
    nb = pl.num_programs(0)
    slot = i % RING
    tile_rows = td * n_slabs

    def fetch(step):
        start = pl.multiple_of(step * tile_rows, tile_rows)
        return pltpu.make_async_copy(h1s_hbm.at[pl.ds(start, tile_rows), :], ring.at[step % RING],
                                     fsem.at[step % RING])

    def wait_scatter(step):
        for _ in range(TOP_K):
            pltpu.make_async_copy(ring.at[step % RING], xs_hbm.at[pl.ds(0, tile_rows), :],
                                  ssem.at[step % RING]).wait()

    zrows = zslab.shape[0] // n_slabs

    def pad_copy(row, size):
        return pltpu.make_async_copy(
            zslab.at[pl.ds(0, size * n_slabs), :],
            xs_hbm.at[pl.ds(pl.multiple_of(row * n_slabs, n_slabs), size * n_slabs), :], zsem.at[0])

    def for_each_pad_copy(fn):
        def per_expert(e, carry):
            lo = pad_lo_ref[e]
            length = pad_hi_ref[e] - lo
            for b in range(zrows.bit_length()):
                size = 1 << b

                @pl.when(((length >> b) & 1) == 1)
                def _():
                    fn(pad_copy(lo + (length & (size - 1)), size))
            return carry
        lax.fori_loop(0, N_EXPERTS, per_expert, 0)
        tail_lo = pad_lo_ref[N_EXPERTS]

        def per_piece(piece, carry):
            fn(pad_copy(tail_lo + piece * zrows, zrows))
            return carry
        lax.fori_loop(0, (pad_hi_ref[N_EXPERTS] - tail_lo) // zrows, per_piece, 0)

    @pl.when(i == 0)
    def _():
        fetch(0).start()
        zslab[...] = jnp.zeros_like(zslab)
        for_each_pad_copy(lambda cp: cp.start())

    @pl.when((i == 0) & (nb > 1))
    def _():
        fetch(1).start()

    fetch(i).wait()

    def issue_body(r, carry):
        src = ring.at[slot, pl.ds(pl.multiple_of(r * n_slabs, n_slabs), n_slabs), :]
        for k in range(TOP_K):
            dst_row = pl.multiple_of(dest_ref[k * n_tokens + i * td + r], n_slabs)
            pltpu.make_async_copy(src, xs_hbm.at[pl.ds(dst_row, n_slabs), :], ssem.at[slot]).start(priority=k)
        return carry
    lax.fori_loop(0, td, issue_body, 0, unroll=DMA_UNROLL)

    @pl.when(i > 0)
    def _():
        wait_scatter(i - 1)

    @pl.when(i + 2 < nb)
    def _():
        fetch(i + 2).start()

    @pl.when(i == nb - 1)
    def _():
        wait_scatter(i)
        for_each_pad_copy(lambda cp: cp.wait())


def _dispatch(dest_rows, pad_lo, pad_hi, h1s, cap_rows, td, n_slabs, bm):
    n = h1s.shape[0] // n_slabs
    assert n % td == 0
    grid_spec = pltpu.PrefetchScalarGridSpec(
        num_scalar_prefetch=3,
        grid=(n // td,),
        in_specs=[pl.BlockSpec(memory_space=pl.ANY)],
        out_specs=pl.BlockSpec(memory_space=pl.ANY),
        scratch_shapes=[pltpu.VMEM((RING, td * n_slabs, LANES), h1s.dtype),
                        pltpu.VMEM((bm // 2 * n_slabs, LANES), h1s.dtype),
                        pltpu.SemaphoreType.DMA((RING,)), pltpu.SemaphoreType.DMA((RING,)),
                        pltpu.SemaphoreType.DMA((1,))],
    )
    return pl.pallas_call(
        functools.partial(_dispatch_kernel, td=td, n_slabs=n_slabs, n_tokens=n),
        grid_spec=grid_spec,
        out_shape=jax.ShapeDtypeStruct((cap_rows * n_slabs, LANES), h1s.dtype),
        compiler_params=pltpu.CompilerParams(dimension_semantics=("arbitrary",), disable_bounds_checks=True),
        name="dispatch",
    )(dest_rows, pad_lo, pad_hi, h1s)


def _expert_kernel(be_ref, nu_ref, xs_ref, wg_ref, wu_ref, wd_ref, y_ref, wg_bf, wu_bf, wd_bf, *, bm, n_slabs):
    i = pl.program_id(0)
    used = i < nu_ref[0]

    @pl.when(used & ((i == 0) | (be_ref[i] != be_ref[jnp.maximum(i - 1, 0)])))
    def _():
        wg_bf[...] = wg_ref[...].astype(BF16)
        wu_bf[...] = wu_ref[...].astype(BF16)
        wd_bf[...] = wd_ref[...].astype(BF16)

    @pl.when(used)
    def _():
        xb = _unpack_bf16_pairs(_load_slabs(xs_ref, bm, n_slabs)).astype(BF16)
        hid = _silu(_dot(xb, wg_bf[...])) * _dot(xb, wu_bf[...])
        _store_slabs(y_ref, _pack_bf16_pairs(_dot(hid.astype(BF16), wd_bf[...])))

    @pl.when(jnp.logical_not(used))
    def _():
        y_ref[...] = jnp.zeros_like(y_ref)


def _experts(block_expert, n_used, xs, w_gate, w_up, w_down, bm):
    n_blocks = block_expert.shape[0]
    d = w_gate.shape[1]
    ff = w_gate.shape[2]
    n_slabs = d // (2 * LANES)

    def blk(i, be, nu):
        return jnp.minimum(i, nu[0] - 1)

    grid_spec = pltpu.PrefetchScalarGridSpec(
        num_scalar_prefetch=2,
        grid=(n_blocks,),
        in_specs=[pl.BlockSpec((bm * n_slabs, LANES), lambda i, be, nu: (blk(i, be, nu), 0)),
                  pl.BlockSpec((None, d, ff), lambda i, be, nu: (be[blk(i, be, nu)], 0, 0)),
                  pl.BlockSpec((None, d, ff), lambda i, be, nu: (be[blk(i, be, nu)], 0, 0)),
                  pl.BlockSpec((None, ff, d), lambda i, be, nu: (be[blk(i, be, nu)], 0, 0))],
        out_specs=pl.BlockSpec((bm * n_slabs, LANES), lambda i, be, nu: (i, 0)),
        scratch_shapes=[pltpu.VMEM((d, ff), BF16), pltpu.VMEM((d, ff), BF16), pltpu.VMEM((ff, d), BF16)],
    )
    return pl.pallas_call(
        functools.partial(_expert_kernel, bm=bm, n_slabs=n_slabs),
        grid_spec=grid_spec,
        out_shape=jax.ShapeDtypeStruct(xs.shape, xs.dtype),
        compiler_params=pltpu.CompilerParams(dimension_semantics=("arbitrary",), vmem_limit_bytes=VMEM_LIMIT),
        name="experts",
    )(block_expert, n_used, xs, w_gate, w_up, w_down)


def _combine_kernel(dest_ref, y_hbm, h1_ref, route_ref, g2_ref, b2_ref, out_ref, ybuf, sem, *, tm, n_slabs, alpha):
    i = pl.program_id(0)
    nb = pl.num_programs(0)
    part = tm * n_slabs

    def issue(blk, slot):
        def body(r, carry):
            for k in range(TOP_K):
                src_row = pl.multiple_of(dest_ref[k * (nb * tm) + blk * tm + r], n_slabs)
                dst_row = pl.multiple_of((slot * TOP_K + k) * part + r * n_slabs, n_slabs)
                pltpu.make_async_copy(y_hbm.at[pl.ds(src_row, n_slabs), :], ybuf.at[pl.ds(dst_row, n_slabs), :],
                                      sem.at[slot]).start(priority=k)
            return carry
        lax.fori_loop(0, tm, body, 0, unroll=DMA_UNROLL)

    @pl.when(i == 0)
    def _():
        issue(0, 0)

    @pl.when(i + 1 < nb)
    def _():
        issue(i + 1, (i + 1) % 2)

    slot = i % 2
    base = pl.multiple_of(slot * (TOP_K * part), TOP_K * part)
    pltpu.make_async_copy(y_hbm.at[pl.ds(0, TOP_K * part), :], ybuf.at[pl.ds(base, TOP_K * part), :],
                          sem.at[slot]).wait()

    route = route_ref[...]
    ffn = (_unpack_bf16_pairs(_load_slabs(ybuf, tm, n_slabs, base)) * route[:, 2:3]
           + _unpack_bf16_pairs(_load_slabs(ybuf, tm, n_slabs, base + part)) * route[:, 3:4])
    out_ref[...] = _layer_norm(alpha * h1_ref[...] + ffn, g2_ref[...], b2_ref[...])


def _combine(dest_rows, y_sorted, h1, route, ln2_g, ln2_b, tm, alpha):
    n, d = h1.shape
    n_slabs = d // (2 * LANES)
    assert n % tm == 0
    grid_spec = pltpu.PrefetchScalarGridSpec(
        num_scalar_prefetch=1,
        grid=(n // tm,),
        in_specs=[pl.BlockSpec(memory_space=pl.ANY),
                  pl.BlockSpec((tm, d), lambda i, dest: (i, 0)),
                  pl.BlockSpec((tm, LANES), lambda i, dest: (i, 0)),
                  pl.BlockSpec((1, d), lambda i, dest: (0, 0)),
                  pl.BlockSpec((1, d), lambda i, dest: (0, 0))],
        out_specs=pl.BlockSpec((tm, d), lambda i, dest: (i, 0)),
        scratch_shapes=[pltpu.VMEM((2 * TOP_K * tm * n_slabs, LANES), y_sorted.dtype),
                        pltpu.SemaphoreType.DMA((2,))],
    )
    return pl.pallas_call(
        functools.partial(_combine_kernel, tm=tm, n_slabs=n_slabs, alpha=alpha),
        grid_spec=grid_spec,
        out_shape=jax.ShapeDtypeStruct((n, d), F32),
        compiler_params=pltpu.CompilerParams(dimension_semantics=("arbitrary",), vmem_limit_bytes=VMEM_LIMIT,
                                             disable_bounds_checks=True),
        name="combine",
    )(dest_rows, y_sorted, h1, route, ln2_g, ln2_b)


def _dispatch_plan(route_t, counts, bm, n_slabs):
    n = route_t.shape[1]
    expert_id = route_t[0:TOP_K].astype(jnp.int32)
    rank = route_t[4:4 + TOP_K].astype(jnp.int32)
    padded = (counts + bm - 1) // bm * bm
    pad_end = jnp.cumsum(padded)
    pad_start = pad_end - padded
    experts = jnp.arange(N_EXPERTS, dtype=jnp.int32)[:, None, None]
    dest = jnp.sum(jnp.where(expert_id[None] == experts, pad_start[:, None, None], 0), axis=0) + rank
    n_blocks = (n * TOP_K + bm - 1) // bm + N_EXPERTS
    block_start = jnp.arange(n_blocks, dtype=jnp.int32) * bm
    block_expert = jnp.minimum(jnp.sum((block_start[:, None] >= pad_end[None, :]).astype(jnp.int32), axis=1),
                               N_EXPERTS - 1).astype(jnp.int32)
    n_used = (pad_end[-1:] // bm).astype(jnp.int32)
    dest_rows = (dest * n_slabs).reshape(-1).astype(jnp.int32)
    pad_lo = jnp.concatenate([pad_start + counts, pad_end[-1:]]).astype(jnp.int32)
    pad_hi = jnp.concatenate([pad_end, jnp.full((1,), n_blocks * bm, pad_end.dtype)]).astype(jnp.int32)
    return dest_rows, pad_lo, pad_hi, block_expert, n_used, n_blocks


def _pad_lanes(w, width=LANES):
    return jnp.pad(w, [(0, 0)] * (w.ndim - 1) + [(0, width - w.shape[-1])])


def kernel(x, meta_tokens, ln_emb_g, ln_emb_b, w_in, conv_qkv_w, a_log, dt_bias, dn_norm_g, conv_dw_w, conv_dw_b, cv_norm_g, cv_norm_b, w_out, ln1_g, ln1_b, w_group, b_group, w_router, b_router, w_exp_gate, w_exp_up, w_exp_down, ln2_g, ln2_b):
    depth = w_in.shape[0]
    assert depth == 1, "single-layer block"
    bsz, seq, d = x.shape
    alpha = (2.0 * depth) ** 0.25
    qkv_w = 3 * DN_WIDTH
    w_in0 = w_in[0]
    glu_off = 4 * DN_WIDTH + 2 * DN_HEADS
    row = lambda a: a.reshape(1, -1).astype(F32)
    p = {
        'ln_emb_g': row(ln_emb_g), 'ln_emb_b': row(ln_emb_b),
        'w_qkv': w_in0[:, :qkv_w].astype(BF16),
        'w_z': w_in0[:, qkv_w:4 * DN_WIDTH].astype(BF16),
        'w_ba': _pad_lanes(w_in0[:, 4 * DN_WIDTH:glu_off]).astype(BF16),
        'w_glu': w_in0[:, glu_off:].astype(BF16),
        'conv_w': conv_qkv_w[0].astype(F32),
        'neg_a': _pad_lanes(jnp.concatenate([jnp.zeros((DN_HEADS,), F32), -jnp.exp(a_log[0].astype(F32))])[None]),
        'dt_b': _pad_lanes(jnp.concatenate([jnp.zeros((DN_HEADS,), F32), dt_bias[0].astype(F32)])[None]),
        'dw_w': conv_dw_w[0].astype(F32), 'dw_b': row(conv_dw_b[0]),
        'cv_g': row(cv_norm_g[0]), 'cv_b': row(cv_norm_b[0]),
        'w_out': w_out[0].astype(BF16), 'ln1_g': row(ln1_g[0]), 'ln1_b': row(ln1_b[0]),
    }
    w_r = _pad_lanes(jnp.concatenate([w_group[0], w_router[0]], axis=1).astype(F32))
    p['w_r_hi'] = w_r.astype(BF16)
    p['w_r_lo'] = (w_r - p['w_r_hi'].astype(F32)).astype(BF16)
    p['b_r'] = _pad_lanes(jnp.concatenate([b_group[0], b_router[0]])[None].astype(F32))
    gain = row(dn_norm_g[0])

    conf_w = p['dw_w'].shape[1]
    zero_hq = jnp.zeros((QKV_HALO, qkv_w), F32)
    zero_hc = jnp.zeros((CONF_HALO, conf_w), F32)
    m_mixed, mbg, halo_q, halo_c, _ = _mix_in(meta_tokens[None].astype(F32), p, zero_hq, zero_hc, N_META)
    front = lambda a: jnp.pad(a, [(0, 0), (CHUNK - N_META, 0), (0, 0)])
    s_zero = jnp.zeros((DN_HEADS, HEAD_DIM, HEAD_DIM), F32)
    _, s_meta = _delta(front(m_mixed), front(mbg), s_zero, gain, 1)

    mixed, bg, _, _, h = _mix_in(x, p, halo_q[0], halo_c[0], TM_IN)
    o, _ = _delta(mixed, bg, s_meta[0], gain, DELTA_CHUNKS)

    n = bsz * seq
    h1, h1p, route, route_t, cnt = _mix_out(h.reshape(n, d), o.reshape(n, DN_WIDTH),
                                            mixed.reshape(n, mixed.shape[2]), p, halo_c[0], seq, TM_OUT, alpha)

    n_slabs = d // (2 * LANES)
    counts = cnt[0, :N_EXPERTS].astype(jnp.int32)
    dest_rows, pad_lo, pad_hi, block_expert, n_used, n_blocks = _dispatch_plan(route_t, counts, BM_EXPERT, n_slabs)
    xs = _dispatch(dest_rows, pad_lo, pad_hi, h1p, n_blocks * BM_EXPERT, TD_DISPATCH, n_slabs, BM_EXPERT)
    first_layer = lambda w: w.reshape(w.shape[1:])
    y_sorted = _experts(block_expert, n_used, xs, first_layer(w_exp_gate), first_layer(w_exp_up),
                        first_layer(w_exp_down), BM_EXPERT)
    out = _combine(dest_rows, y_sorted, h1, route, row(ln2_g[0]), row(ln2_b[0]), TM_COMBINE, alpha)
    return out.reshape(bsz, seq, d)
```

```python
import functools

import jax
import jax.numpy as jnp
from jax import lax
from jax.experimental import pallas as pl
from jax.experimental.pallas import tpu as pltpu

F32 = jnp.float32
BF16 = jnp.bfloat16

NORM_EPS = 1e-5
N_META = 16
DN_HEADS = 4
HEAD_DIM = 128
DN_WIDTH = DN_HEADS * HEAD_DIM
CHUNK = 64
SHORT_CONV = 4
CONF_KERNEL = 31
N_GROUPS = 4
EXPERTS_PER_GROUP = 8
N_EXPERTS = N_GROUPS * EXPERTS_PER_GROUP
TOP_K = 2
LANES = 128
SUBLANES = 8
QKV_HALO = 8
CONF_HALO = 32
VMEM_LIMIT = 56 * 1024 * 1024

TM_IN = 512
DELTA_CHUNKS = 8
TM_OUT = 1024
BM_EXPERT = 512
TM_COMBINE = 512
TD_DISPATCH = 512
RING = 3
SUB_ROWS = 256
SUB_ROWS_OUT = 512
CONV_BLOCK_ROWS = 64
CONV_BLOCK_COLS = 512
MIXED_COLS = 5
COL_Q, COL_K, COL_V, COL_Z, COL_C = range(MIXED_COLS)
DMA_UNROLL = 8


def _dot(a, b):
    return jnp.dot(a, b, preferred_element_type=F32)


def _split2(x):
    hi = x.astype(BF16)
    lo = (x - hi.astype(F32)).astype(BF16)
    return hi, lo


def _sigmoid(x):
    return 1.0 / (1.0 + jnp.exp(-x))


def _silu(x):
    return x * _sigmoid(x)


def _layer_norm(x, g, b):
    mu = jnp.mean(x, axis=-1, keepdims=True)
    xc = x - mu
    var = jnp.mean(xc * xc, axis=-1, keepdims=True)
    return xc * lax.rsqrt(var + NORM_EPS) * g + b


def _pack_bf16_pairs(x):
    half = x.shape[1] // 2
    lo = lax.bitcast_convert_type(x[:, :half].astype(BF16).astype(F32), jnp.uint32)
    hi = lax.bitcast_convert_type(x[:, half:].astype(BF16).astype(F32), jnp.uint32)
    return (lo >> 16) | (hi & jnp.uint32(0xFFFF0000))


def _unpack_bf16_pairs(p):
    lo = lax.bitcast_convert_type(p << 16, F32)
    hi = lax.bitcast_convert_type(p & jnp.uint32(0xFFFF0000), F32)
    return jnp.concatenate([lo, hi], axis=1)


def _store_slabs(ref, val):
    rows, d = val.shape
    n_slabs = d // LANES
    for s in range(n_slabs):
        ref[pl.ds(s, rows, stride=n_slabs), :] = val[:, s * LANES:(s + 1) * LANES]


def _load_slabs(ref, rows, n_slabs, base=0):
    return jnp.concatenate([ref[pl.ds(base + s, rows, stride=n_slabs), :] for s in range(n_slabs)], axis=1)


def _full_spec(shape):
    nd = len(shape)
    return pl.BlockSpec(shape, lambda *_: (0,) * nd)


def _causal_depthwise(ext_ref, w_ref, n_taps, halo, tm):
    ext = ext_ref[...]
    rows, cols = ext.shape
    first = halo - (n_taps - 1)
    groups = ext.reshape(rows // SUBLANES, SUBLANES, cols)
    sub = lax.broadcasted_iota(jnp.int32, (1, SUBLANES, 1), 1)

    def shifted(phase):
        rolled = pltpu.roll(groups, SUBLANES - phase, axis=1)
        nxt = jnp.concatenate([rolled[1:], rolled[:1]], axis=0)
        return jnp.where(sub < SUBLANES - phase, rolled, nxt).reshape(rows, cols)

    phases = {}
    for k in range(n_taps):
        phase = (first + k) % SUBLANES
        if phase not in phases:
            phases[phase] = ext if phase == 0 else shifted(phase)

    rb = min(tm, CONV_BLOCK_ROWS)
    cb = min(cols, CONV_BLOCK_COLS)
    out_rows = []
    for r0 in range(0, tm, rb):
        out_cols = []
        for c0 in range(0, cols, cb):
            acc = None
            for k in range(n_taps):
                phase = (first + k) % SUBLANES
                base = first + k - phase + r0
                term = phases[phase][base:base + rb, c0:c0 + cb] * w_ref[k:k + 1, c0:c0 + cb]
                acc = term if acc is None else acc + term
            out_cols.append(acc)
        out_rows.append(out_cols[0] if len(out_cols) == 1 else jnp.concatenate(out_cols, axis=1))
    return out_rows[0] if len(out_rows) == 1 else jnp.concatenate(out_rows, axis=0)


def _row_views(refs, r0, rows, lead=0):
    return [ref.at[pl.ds(r0, lead + rows), :] for ref in refs]


def _mix_in_kernel(x_ref, lng_ref, lnb_ref, wqkv_ref, wz_ref, wglu_ref, wba_ref, cw_ref, nega_ref,
                   dtb_ref, hq_in_ref, hc_in_ref,
                   m_ref, bg_ref, hq_out_ref, hc_out_ref, h_ref,
                   qkv_ext, c_ext):
    tm = x_ref.shape[0]

    @pl.when(pl.program_id(1) == 0)
    def _():
        qkv_ext[0:QKV_HALO, :] = hq_in_ref[...]
        c_ext[0:CONF_HALO, :] = hc_in_ref[...]

    th = min(tm, SUB_ROWS)
    for r0 in range(0, tm, th):
        x_v, bg_v, h_v = _row_views((x_ref, bg_ref, h_ref), r0, th)
        q_v, k_v, v_v, z_v, c_v = (m_ref.at[pl.ds(r0, th), pl.ds(col * DN_WIDTH, DN_WIDTH)]
                                   for col in range(MIXED_COLS))
        qkv_v, = _row_views((qkv_ext,), r0, th, QKV_HALO)
        cext_v, = _row_views((c_ext,), r0, th, CONF_HALO)
        _mix_in_rows(x_v, lng_ref, lnb_ref, wqkv_ref, wz_ref, wglu_ref, wba_ref, cw_ref, nega_ref, dtb_ref,
                     q_v, k_v, v_v, z_v, c_v, bg_v, h_v, qkv_v, cext_v)

    q_tail = qkv_ext[tm:tm + QKV_HALO, :]
    c_tail = c_ext[tm:tm + CONF_HALO, :]
    qkv_ext[0:QKV_HALO, :] = q_tail
    c_ext[0:CONF_HALO, :] = c_tail
    hq_out_ref[...] = q_tail
    hc_out_ref[...] = c_tail


def _mix_in_rows(x_ref, lng_ref, lnb_ref, wqkv_ref, wz_ref, wglu_ref, wba_ref, cw_ref, nega_ref, dtb_ref,
                 q_ref, k_ref, v_ref, z_ref, c_ref, bg_ref, h_ref, qkv_ext, c_ext):
    tm = x_ref.shape[0]
    h = _layer_norm(x_ref[...], lng_ref[...], lnb_ref[...])
    h_ref[...] = h
    hb = h.astype(BF16)

    qkv_ext[QKV_HALO:QKV_HALO + tm, :] = _dot(hb, wqkv_ref[...])
    qkv = _silu(_causal_depthwise(qkv_ext, cw_ref, SHORT_CONV, QKV_HALO, tm))
    for hd in range(DN_HEADS):
        lo = hd * HEAD_DIM
        qh = qkv[:, lo:lo + HEAD_DIM]
        kh = qkv[:, DN_WIDTH + lo:DN_WIDTH + lo + HEAD_DIM]
        q_ref[:, lo:lo + HEAD_DIM] = (qh * (lax.rsqrt(jnp.sum(qh * qh, axis=-1, keepdims=True) + 1e-6)
                                            * (HEAD_DIM ** -0.5))).astype(q_ref.dtype)
        k_ref[:, lo:lo + HEAD_DIM] = (kh * lax.rsqrt(jnp.sum(kh * kh, axis=-1, keepdims=True) + 1e-6)
                                      ).astype(k_ref.dtype)
    v_ref[...] = qkv[:, 2 * DN_WIDTH:].astype(v_ref.dtype)
    z_ref[...] = _dot(hb, wz_ref[...]).astype(z_ref.dtype)

    ba = _dot(hb, wba_ref[...])
    lane = lax.broadcasted_iota(jnp.int32, ba.shape, 1)
    sp_in = ba + dtb_ref[...]
    softplus = jnp.maximum(sp_in, 0.0) + jnp.log(1.0 + jnp.exp(-jnp.abs(sp_in)))
    bg_ref[...] = jnp.where(lane < DN_HEADS, _sigmoid(ba), nega_ref[...] * softplus)

    glu = _dot(hb, wglu_ref[...])
    cw = glu.shape[1] // 2
    c_pre = glu[:, :cw] * _sigmoid(glu[:, cw:])
    c_ref[...] = c_pre.astype(c_ref.dtype)
    c_ext[CONF_HALO:CONF_HALO + tm, :] = c_pre


def _mix_in(x, p, halo_q, halo_c, tm):
    bsz, seq, d = x.shape
    assert seq % tm == 0
    qkv_w = 3 * DN_WIDTH
    conf_w = p['dw_w'].shape[1]

    def row(width):
        return pl.BlockSpec((None, tm, width), lambda b, t: (b, t, 0))

    def per_batch(rows, width):
        return pl.BlockSpec((None, rows, width), lambda b, t: (b, 0, 0))

    consts = [p['ln_emb_g'], p['ln_emb_b'], p['w_qkv'], p['w_z'], p['w_glu'], p['w_ba'], p['conv_w'],
              p['neg_a'], p['dt_b'], halo_q, halo_c]
    sds = jax.ShapeDtypeStruct
    assert conf_w == DN_WIDTH, "q, k, v, z and the conformer channels share one array of equal-width column blocks"
    out_shape = [sds((bsz, seq, MIXED_COLS * DN_WIDTH), BF16),
                 sds((bsz, seq, LANES), F32), sds((bsz, QKV_HALO, qkv_w), F32),
                 sds((bsz, CONF_HALO, conf_w), F32), sds((bsz, seq, d), F32)]
    out_specs = [row(MIXED_COLS * DN_WIDTH), row(LANES), per_batch(QKV_HALO, qkv_w),
                 per_batch(CONF_HALO, conf_w), row(d)]
    return pl.pallas_call(
        _mix_in_kernel,
        grid=(bsz, seq // tm),
        in_specs=[row(d)] + [_full_spec(c.shape) for c in consts],
        out_specs=out_specs,
        out_shape=out_shape,
        scratch_shapes=[pltpu.VMEM((QKV_HALO + tm, qkv_w), F32), pltpu.VMEM((CONF_HALO + tm, conf_w), F32)],
        compiler_params=pltpu.CompilerParams(dimension_semantics=("parallel", "arbitrary"),
                                             vmem_limit_bytes=VMEM_LIMIT),
        name="mix_in",
    )(x, *consts)


def _bmm(a, b):
    return jnp.einsum('nij,njk->nik', a, b, preferred_element_type=F32)


def _delta_kernel(m_ref, z_ref, bg_ref, s0_ref, gain_ref, o_ref, sfin_ref,
                  s_ref, u_s, wq_s, attn_s, kd_s, egl_s, *, chunks):
    j = pl.program_id(1)
    slot = j % 2
    prev = 1 - slot
    gain = gain_ref[...]

    @pl.when(j == 0)
    def _():
        s_ref[...] = s0_ref[...]
        u_s[1] = jnp.zeros(u_s.shape[1:], u_s.dtype)
        wq_s[1] = jnp.zeros(wq_s.shape[1:], wq_s.dtype)
        attn_s[1] = jnp.zeros(attn_s.shape[1:], attn_s.dtype)
        kd_s[1] = jnp.zeros(kd_s.shape[1:], kd_s.dtype)
        egl_s[1] = jnp.zeros(egl_s.shape[1:], egl_s.dtype)

    def recurrence():
        live = j > 0
        state = [s_ref[hd] for hd in range(DN_HEADS)]
        for c in range(chunks):
            idx = [hd * chunks + c for hd in range(DN_HEADS)]
            wq = [_dot(wq_s[prev, n], state[hd].astype(BF16)) for hd, n in enumerate(idx)]
            yield
            v_new = [(u_s[prev, n] - wq[hd][:CHUNK]).astype(BF16) for hd, n in enumerate(idx)]
            o = [wq[hd][CHUNK:] + _dot(attn_s[prev, n], v_new[hd]) for hd, n in enumerate(idx)]
            state = [state[hd] * egl_s[prev, n][0:1, :]
                     + lax.dot_general(kd_s[prev, n], v_new[hd], (((0,), (0,)), ((), ())),
                                       preferred_element_type=F32)
                     for hd, n in enumerate(idx)]
            yield
            for hd in range(DN_HEADS):
                cols = slice(hd * HEAD_DIM, (hd + 1) * HEAD_DIM)
                rows = slice(c * CHUNK, (c + 1) * CHUNK)
                r = o[hd] * lax.rsqrt(jnp.mean(o[hd] * o[hd], axis=-1, keepdims=True) + 1e-6)
                o_ref[rows, cols] = (r * gain * _silu(z_ref[rows, cols].astype(F32))).astype(o_ref.dtype)
        for hd in range(DN_HEADS):
            kept = jnp.where(live, state[hd], s_ref[hd])
            s_ref[hd] = kept
            sfin_ref[hd] = kept

    def preparation():
        yield from _delta_prepare(m_ref, bg_ref, u_s, wq_s, attn_s, kd_s, egl_s, slot, chunks)

    halves = [recurrence(), preparation()]
    while halves:
        for gen in list(halves):
            try:
                next(gen)
            except StopIteration:
                halves.remove(gen)


def _delta_prepare(m_ref, bg_ref, u_s, wq_s, attn_s, kd_s, egl_s, slot, chunks):
    ii = lax.broadcasted_iota(jnp.int32, (CHUNK, CHUNK), 0)
    jj = lax.broadcasted_iota(jnp.int32, (CHUNK, CHUNK), 1)
    causal = ii >= jj
    strict = ii > jj
    eye = (ii == jj).astype(F32)
    bg3 = bg_ref[...].reshape(chunks, CHUNK, LANES)
    tril_b = jnp.broadcast_to(causal.astype(BF16), (chunks, CHUNK, CHUNK))
    p1 = bg3.astype(BF16)
    r1 = bg3 - p1.astype(F32)
    p2 = r1.astype(BF16)
    p3 = (r1 - p2.astype(F32)).astype(BF16)
    gc3 = _bmm(tril_b, p1) + _bmm(tril_b, p2) + _bmm(tril_b, p3)
    yield

    def heads(col):
        lo = col * DN_WIDTH
        return jnp.concatenate([m_ref[:, lo + hd * HEAD_DIM:lo + (hd + 1) * HEAD_DIM].astype(F32)
                                .reshape(chunks, CHUNK, HEAD_DIM) for hd in range(DN_HEADS)], axis=0)
    q = heads(COL_Q)
    k = heads(COL_K)
    v = heads(COL_V)
    bet = jnp.concatenate([bg3[:, :, hd:hd + 1] for hd in range(DN_HEADS)], axis=0)
    gc = jnp.concatenate([gc3[:, :, DN_HEADS + hd:DN_HEADS + hd + 1] for hd in range(DN_HEADS)], axis=0)
    gc_t = [gc3[c].T for c in range(chunks)]
    decay = jnp.stack([
        jnp.exp(jnp.where(causal, gc3[c][:, DN_HEADS + hd:DN_HEADS + hd + 1]
                          - gc_t[c][DN_HEADS + hd:DN_HEADS + hd + 1, :], -jnp.inf))
        for hd in range(DN_HEADS) for c in range(chunks)], axis=0)

    kb = k * bet
    g_all = jnp.einsum('nid,njd->nij', jnp.concatenate([kb, q], axis=1).astype(BF16), k.astype(BF16),
                       preferred_element_type=F32)
    yield
    a_low = jnp.where(strict, g_all[:, :CHUNK] * decay, 0.0)
    attn = (g_all[:, CHUNK:] * decay).astype(BF16)

    l_mat = eye + a_low
    l_bf = l_mat.astype(BF16)
    t_mat = eye - a_low
    for _ in range(4):
        res = eye - _bmm(l_bf, t_mat.astype(BF16))
        yield
        t_mat = t_mat + _bmm(t_mat.astype(BF16), res.astype(BF16))
        yield
    l_lo = (l_mat - l_bf.astype(F32)).astype(BF16)
    t_hi = t_mat.astype(BF16)
    t_lo = (t_mat - t_hi.astype(F32)).astype(BF16)
    res = eye - (_bmm(l_bf, t_hi) + _bmm(l_lo, t_hi) + _bmm(l_bf, t_lo))
    yield
    t_mat = t_mat + _bmm(t_hi, res.astype(BF16))
    yield

    eg = jnp.exp(gc)
    uw = _bmm(t_mat.astype(BF16), jnp.concatenate([v * bet, kb * eg], axis=2).astype(BF16))
    yield
    u = uw[:, :, :HEAD_DIM]
    wq_lhs = jnp.concatenate([uw[:, :, HEAD_DIM:], q * eg], axis=1).astype(BF16)
    g_last = gc[:, CHUNK - 1:CHUNK, :]
    k_dec = (k * jnp.exp(g_last - gc)).astype(BF16)
    eg_last = jnp.exp(g_last)

    u_s[slot] = u
    wq_s[slot] = wq_lhs
    attn_s[slot] = attn
    kd_s[slot] = k_dec
    egl_s[slot] = jnp.broadcast_to(eg_last, egl_s.shape[1:])


def _delta(mixed, bg, s0, gain, chunks):
    bsz, seq, _ = mixed.shape
    rows = chunks * CHUNK
    assert seq % rows == 0

    nj = seq // rows
    nb = DN_HEADS * chunks

    def prep(width):
        return pl.BlockSpec((None, rows, width), lambda b, j: (b, jnp.minimum(j, nj - 1), 0))

    def scan(width, col=0):
        return pl.BlockSpec((None, rows, width), lambda b, j: (b, jnp.maximum(j - 1, 0), col))

    state_shape = (DN_HEADS, HEAD_DIM, HEAD_DIM)
    return pl.pallas_call(
        functools.partial(_delta_kernel, chunks=chunks),
        grid=(bsz, nj + 1),
        in_specs=[prep(mixed.shape[2]), scan(DN_WIDTH, COL_Z), prep(LANES), _full_spec(state_shape),
                  _full_spec(gain.shape)],
        out_specs=[scan(DN_WIDTH), pl.BlockSpec((None,) + state_shape, lambda b, j: (b, 0, 0, 0))],
        out_shape=[jax.ShapeDtypeStruct((bsz, seq, DN_WIDTH), BF16),
                   jax.ShapeDtypeStruct((bsz,) + state_shape, F32)],
        scratch_shapes=[pltpu.VMEM(state_shape, F32),
                        pltpu.VMEM((2, nb, CHUNK, HEAD_DIM), F32),
                        pltpu.VMEM((2, nb, 2 * CHUNK, HEAD_DIM), BF16),
                        pltpu.VMEM((2, nb, CHUNK, CHUNK), BF16),
                        pltpu.VMEM((2, nb, CHUNK, HEAD_DIM), BF16),
                        pltpu.VMEM((2, nb, SUBLANES, HEAD_DIM), F32)],
        compiler_params=pltpu.CompilerParams(dimension_semantics=("parallel", "arbitrary"),
                                             vmem_limit_bytes=VMEM_LIMIT),
        name="delta",
    )(mixed, mixed, bg, s0, gain)


def _mix_out_kernel(h_ref, o_ref, c_ref, wo_ref, g1_ref, b1_ref, wrh_ref, wrl_ref, br_ref,
                    dww_ref, dwb_ref, cvg_ref, cvb_ref, hc_in_ref,
                    h1_ref, h1p_ref, route_ref, route_t_ref, cnt_out_ref, cnt_ref, c_ext, *, alpha, tiles_per_seq):
    tm = h_ref.shape[0]

    @pl.when(pl.program_id(0) == 0)
    def _():
        cnt_ref[...] = jnp.zeros_like(cnt_ref)

    @pl.when(pl.program_id(0) % tiles_per_seq == 0)
    def _():
        c_ext[0:CONF_HALO, :] = hc_in_ref[...]

    th = min(tm, SUB_ROWS_OUT)
    n_slabs = h1p_ref.shape[0] // tm
    for r0 in range(0, tm, th):
        h_v, o_v, c_v, h1_v, route_v = _row_views((h_ref, o_ref, c_ref, h1_ref, route_ref), r0, th)
        cext_v, = _row_views((c_ext,), r0, th, CONF_HALO)
        h1p_v, = _row_views((h1p_ref,), r0 * n_slabs, th * n_slabs)
        _mix_out_rows(h_v, o_v, c_v, wo_ref, g1_ref, b1_ref, wrh_ref, wrl_ref, br_ref, dww_ref, dwb_ref, cvg_ref,
                      cvb_ref, h1_v, h1p_v, route_v, route_t_ref.at[:, pl.ds(r0, th)], cnt_ref, cext_v, alpha)
    c_ext[0:CONF_HALO, :] = c_ext[tm:tm + CONF_HALO, :]
    cnt_out_ref[...] = jnp.broadcast_to(cnt_ref[...], cnt_out_ref.shape)


def _mix_out_rows(h_ref, o_ref, c_ref, wo_ref, g1_ref, b1_ref, wrh_ref, wrl_ref, br_ref, dww_ref, dwb_ref, cvg_ref,
                  cvb_ref, h1_ref, h1p_ref, route_ref, route_t_ref, cnt_ref, c_ext, alpha):
    tm = h_ref.shape[0]

    c_ext[CONF_HALO:CONF_HALO + tm, :] = c_ref[...].astype(F32)
    conv = _causal_depthwise(c_ext, dww_ref, CONF_KERNEL, CONF_HALO, tm) + dwb_ref[...]
    conf = _silu(_layer_norm(conv, cvg_ref[...], cvb_ref[...]))

    h = h_ref[...]
    dn = o_ref.shape[1]
    mix = _dot(o_ref[...].astype(BF16), wo_ref[0:dn, :]) + _dot(conf.astype(BF16), wo_ref[dn:, :])
    h1 = _layer_norm(alpha * h + mix, g1_ref[...], b1_ref[...])
    h1_ref[...] = h1
    _store_slabs(h1p_ref, _pack_bf16_pairs(h1))

    hh, hl = _split2(h1)
    logits = _dot(hh, wrh_ref[...]) + _dot(hl, wrh_ref[...]) + _dot(hh, wrl_ref[...]) + br_ref[...]
    lane = lax.broadcasted_iota(jnp.int32, logits.shape, 1).astype(F32)
    big = float(LANES)
    neg = -jnp.inf

    def first_argmax(vals):
        top = jnp.max(vals, axis=-1, keepdims=True)
        return top, jnp.min(jnp.where(vals == top, lane, big), axis=-1, keepdims=True)

    grp = jnp.where(lane < N_GROUPS, logits, neg)
    g_top, g_sel = first_argmax(grp)
    p_group = 1.0 / jnp.sum(jnp.exp(grp - g_top), axis=-1, keepdims=True)
    lo = N_GROUPS + EXPERTS_PER_GROUP * g_sel
    in_grp = jnp.where((lane >= lo) & (lane < lo + EXPERTS_PER_GROUP), logits, neg)
    m1, i1 = first_argmax(in_grp)
    m2, i2 = first_argmax(jnp.where(lane == i1, neg, in_grp))
    s = jnp.exp(m2 - m1)
    w1 = p_group / (1.0 + s)
    w2 = p_group * s / (1.0 + s)
    e1 = i1 - N_GROUPS
    e2 = i2 - N_GROUPS

    tm = logits.shape[0]
    oh1 = (lane == e1).astype(F32)
    oh2 = (lane == e2).astype(F32)
    both = oh1 + oh2
    ti = lax.broadcasted_iota(jnp.int32, (tm, tm), 0)
    tj = lax.broadcasted_iota(jnp.int32, (tm, tm), 1)
    base = _dot((ti > tj).astype(BF16), both.astype(BF16)) + cnt_ref[...]
    r1 = jnp.sum(oh1 * base, axis=-1, keepdims=True)
    r2 = jnp.sum(oh2 * base, axis=-1, keepdims=True)
    cnt_ref[...] = cnt_ref[...] + jnp.sum(both, axis=0, keepdims=True)

    vals = (e1, e2, w1, w2, r1, r2)
    route = jnp.zeros_like(logits)
    for idx, val in enumerate(vals):
        route = jnp.where(lane == idx, val, route)
    route_ref[...] = route
    route_t_ref[...] = route.T[0:SUBLANES, :]


def _mix_out(h2d, o2d, mixed2d, p, halo_c, seq, tm, alpha):
    n, d = h2d.shape
    assert seq % tm == 0
    slabs = d // (2 * LANES)
    conf_w = p['dw_w'].shape[1]

    def row(width):
        return pl.BlockSpec((tm, width), lambda i: (i, 0))

    consts = [p['w_out'], p['ln1_g'], p['ln1_b'], p['w_r_hi'], p['w_r_lo'], p['b_r'],
              p['dw_w'], p['dw_b'], p['cv_g'], p['cv_b'], halo_c]
    return pl.pallas_call(
        functools.partial(_mix_out_kernel, alpha=alpha, tiles_per_seq=seq // tm),
        grid=(n // tm,),
        in_specs=[row(d), row(o2d.shape[1]), pl.BlockSpec((tm, conf_w), lambda i: (i, COL_C))]
                 + [_full_spec(c.shape) for c in consts],
        out_specs=[row(d), pl.BlockSpec((tm * slabs, LANES), lambda i: (i, 0)), row(LANES),
                   pl.BlockSpec((SUBLANES, tm), lambda i: (0, i)), _full_spec((SUBLANES, LANES))],
        out_shape=[jax.ShapeDtypeStruct((n, d), F32), jax.ShapeDtypeStruct((n * slabs, LANES), jnp.uint32),
                   jax.ShapeDtypeStruct((n, LANES), F32),
                   jax.ShapeDtypeStruct((SUBLANES, n), F32), jax.ShapeDtypeStruct((SUBLANES, LANES), F32)],
        scratch_shapes=[pltpu.VMEM((1, LANES), F32), pltpu.VMEM((CONF_HALO + tm, conf_w), F32)],
        compiler_params=pltpu.CompilerParams(dimension_semantics=("arbitrary",), vmem_limit_bytes=VMEM_LIMIT),
        name="mix_out",
    )(h2d, o2d, mixed2d, *consts)


def _dispatch_kernel(dest_ref, pad_lo_ref, pad_hi_ref, h1s_hbm, xs_hbm, ring, zslab, fsem, ssem, zsem, *,
                     td, n_slabs, n_tokens):
    i = pl.program_id(0)
    nb = pl.num_programs(0)
    slot = i % RING
    tile_rows = td * n_slabs

    def fetch(step):
        start = pl.multiple_of(step * tile_rows, tile_rows)
        return pltpu.make_async_copy(h1s_hbm.at[pl.ds(start, tile_rows), :], ring.at[step % RING],
                                     fsem.at[step % RING])

    def wait_scatter(step):
        for _ in range(TOP_K):
            pltpu.make_async_copy(ring.at[step % RING], xs_hbm.at[pl.ds(0, tile_rows), :],
                                  ssem.at[step % RING]).wait()

    zrows = zslab.shape[0] // n_slabs

    def pad_copy(row, size):
        return pltpu.make_async_copy(
            zslab.at[pl.ds(0, size * n_slabs), :],
            xs_hbm.at[pl.ds(pl.multiple_of(row * n_slabs, n_slabs), size * n_slabs), :], zsem.at[0])

    def for_each_pad_copy(fn):
        def per_expert(e, carry):
            lo = pad_lo_ref[e]
            length = pad_hi_ref[e] - lo
            for b in range(zrows.bit_length()):
                size = 1 << b

                @pl.when(((length >> b) & 1) == 1)
                def _():
                    fn(pad_copy(lo + (length & (size - 1)), size))
            return carry
        lax.fori_loop(0, N_EXPERTS, per_expert, 0)
        tail_lo = pad_lo_ref[N_EXPERTS]

        def per_piece(piece, carry):
            fn(pad_copy(tail_lo + piece * zrows, zrows))
            return carry
        lax.fori_loop(0, (pad_hi_ref[N_EXPERTS] - tail_lo) // zrows, per_piece, 0)

    @pl.when(i == 0)
    def _():
        fetch(0).start()
        zslab[...] = jnp.zeros_like(zslab)
        for_each_pad_copy(lambda cp: cp.start())

    @pl.when((i == 0) & (nb > 1))
    def _():
        fetch(1).start()

    fetch(i).wait()

    def issue_body(r, carry):
        src = ring.at[slot, pl.ds(pl.multiple_of(r * n_slabs, n_slabs), n_slabs), :]
        for k in range(TOP_K):
            dst_row = pl.multiple_of(dest_ref[k * n_tokens + i * td + r], n_slabs)
            pltpu.make_async_copy(src, xs_hbm.at[pl.ds(dst_row, n_slabs), :], ssem.at[slot]).start(priority=k)
        return carry
    lax.fori_loop(0, td, issue_body, 0, unroll=DMA_UNROLL)

    @pl.when(i > 0)
    def _():
        wait_scatter(i - 1)

    @pl.when(i + 2 < nb)
    def _():
        fetch(i + 2).start()

    @pl.when(i == nb - 1)
    def _():
        wait_scatter(i)
        for_each_pad_copy(lambda cp: cp.wait())


def _dispatch(dest_rows, pad_lo, pad_hi, h1s, cap_rows, td, n_slabs, bm):
    n = h1s.shape[0] // n_slabs
    assert n % td == 0
    grid_spec = pltpu.PrefetchScalarGridSpec(
        num_scalar_prefetch=3,
        grid=(n // td,),
        in_specs=[pl.BlockSpec(memory_space=pl.ANY)],
        out_specs=pl.BlockSpec(memory_space=pl.ANY),
        scratch_shapes=[pltpu.VMEM((RING, td * n_slabs, LANES), h1s.dtype),
                        pltpu.VMEM((bm // 2 * n_slabs, LANES), h1s.dtype),
                        pltpu.SemaphoreType.DMA((RING,)), pltpu.SemaphoreType.DMA((RING,)),
                        pltpu.SemaphoreType.DMA((1,))],
    )
    return pl.pallas_call(
        functools.partial(_dispatch_kernel, td=td, n_slabs=n_slabs, n_tokens=n),
        grid_spec=grid_spec,
        out_shape=jax.ShapeDtypeStruct((cap_rows * n_slabs, LANES), h1s.dtype),
        compiler_params=pltpu.CompilerParams(dimension_semantics=("arbitrary",), disable_bounds_checks=True),
        name="dispatch",
    )(dest_rows, pad_lo, pad_hi, h1s)


def _expert_kernel(be_ref, nu_ref, xs_ref, wg_ref, wu_ref, wd_ref, y_ref, wg_bf, wu_bf, wd_bf, *, bm, n_slabs):
    i = pl.program_id(0)
    used = i < nu_ref[0]

    @pl.when(used & ((i == 0) | (be_ref[i] != be_ref[jnp.maximum(i - 1, 0)])))
    def _():
        wg_bf[...] = wg_ref[...].astype(BF16)
        wu_bf[...] = wu_ref[...].astype(BF16)
        wd_bf[...] = wd_ref[...].astype(BF16)

    @pl.when(used)
    def _():
        xb = _unpack_bf16_pairs(_load_slabs(xs_ref, bm, n_slabs)).astype(BF16)
        hid = _silu(_dot(xb, wg_bf[...])) * _dot(xb, wu_bf[...])
        _store_slabs(y_ref, _pack_bf16_pairs(_dot(hid.astype(BF16), wd_bf[...])))

    @pl.when(jnp.logical_not(used))
    def _():
        y_ref[...] = jnp.zeros_like(y_ref)


def _experts(block_expert, n_used, xs, w_gate, w_up, w_down, bm):
    n_blocks = block_expert.shape[0]
    d = w_gate.shape[1]
    ff = w_gate.shape[2]
    n_slabs = d // (2 * LANES)

    def blk(i, be, nu):
        return jnp.minimum(i, nu[0] - 1)

    grid_spec = pltpu.PrefetchScalarGridSpec(
        num_scalar_prefetch=2,
        grid=(n_blocks,),
        in_specs=[pl.BlockSpec((bm * n_slabs, LANES), lambda i, be, nu: (blk(i, be, nu), 0)),
                  pl.BlockSpec((None, d, ff), lambda i, be, nu: (be[blk(i, be, nu)], 0, 0)),
                  pl.BlockSpec((None, d, ff), lambda i, be, nu: (be[blk(i, be, nu)], 0, 0)),
                  pl.BlockSpec((None, ff, d), lambda i, be, nu: (be[blk(i, be, nu)], 0, 0))],
        out_specs=pl.BlockSpec((bm * n_slabs, LANES), lambda i, be, nu: (i, 0)),
        scratch_shapes=[pltpu.VMEM((d, ff), BF16), pltpu.VMEM((d, ff), BF16), pltpu.VMEM((ff, d), BF16)],
    )
    return pl.pallas_call(
        functools.partial(_expert_kernel, bm=bm, n_slabs=n_slabs),
        grid_spec=grid_spec,
        out_shape=jax.ShapeDtypeStruct(xs.shape, xs.dtype),
        compiler_params=pltpu.CompilerParams(dimension_semantics=("arbitrary",), vmem_limit_bytes=VMEM_LIMIT),
        name="experts",
    )(block_expert, n_used, xs, w_gate, w_up, w_down)


def _combine_kernel(dest_ref, y_hbm, h1_ref, route_ref, g2_ref, b2_ref, out_ref, ybuf, sem, *, tm, n_slabs, alpha):
    i = pl.program_id(0)
    nb = pl.num_programs(0)
    part = tm * n_slabs

    def issue(blk, slot):
        def body(r, carry):
            for k in range(TOP_K):
                src_row = pl.multiple_of(dest_ref[k * (nb * tm) + blk * tm + r], n_slabs)
                dst_row = pl.multiple_of((slot * TOP_K + k) * part + r * n_slabs, n_slabs)
                pltpu.make_async_copy(y_hbm.at[pl.ds(src_row, n_slabs), :], ybuf.at[pl.ds(dst_row, n_slabs), :],
                                      sem.at[slot]).start(priority=k)
            return carry
        lax.fori_loop(0, tm, body, 0, unroll=DMA_UNROLL)

    @pl.when(i == 0)
    def _():
        issue(0, 0)

    @pl.when(i + 1 < nb)
    def _():
        issue(i + 1, (i + 1) % 2)

    slot = i % 2
    base = pl.multiple_of(slot * (TOP_K * part), TOP_K * part)
    pltpu.make_async_copy(y_hbm.at[pl.ds(0, TOP_K * part), :], ybuf.at[pl.ds(base, TOP_K * part), :],
                          sem.at[slot]).wait()

    route = route_ref[...]
    ffn = (_unpack_bf16_pairs(_load_slabs(ybuf, tm, n_slabs, base)) * route[:, 2:3]
           + _unpack_bf16_pairs(_load_slabs(ybuf, tm, n_slabs, base + part)) * route[:, 3:4])
    out_ref[...] = _layer_norm(alpha * h1_ref[...] + ffn, g2_ref[...], b2_ref[...])


def _combine(dest_rows, y_sorted, h1, route, ln2_g, ln2_b, tm, alpha):
    n, d = h1.shape
    n_slabs = d // (2 * LANES)
    assert n % tm == 0
    grid_spec = pltpu.PrefetchScalarGridSpec(
        num_scalar_prefetch=1,
        grid=(n // tm,),
        in_specs=[pl.BlockSpec(memory_space=pl.ANY),
                  pl.BlockSpec((tm, d), lambda i, dest: (i, 0)),
                  pl.BlockSpec((tm, LANES), lambda i, dest: (i, 0)),
                  pl.BlockSpec((1, d), lambda i, dest: (0, 0)),
                  pl.BlockSpec((1, d), lambda i, dest: (0, 0))],
        out_specs=pl.BlockSpec((tm, d), lambda i, dest: (i, 0)),
        scratch_shapes=[pltpu.VMEM((2 * TOP_K * tm * n_slabs, LANES), y_sorted.dtype),
                        pltpu.SemaphoreType.DMA((2,))],
    )
    return pl.pallas_call(
        functools.partial(_combine_kernel, tm=tm, n_slabs=n_slabs, alpha=alpha),
        grid_spec=grid_spec,
        out_shape=jax.ShapeDtypeStruct((n, d), F32),
        compiler_params=pltpu.CompilerParams(dimension_semantics=("arbitrary",), vmem_limit_bytes=VMEM_LIMIT,
                                             disable_bounds_checks=True),
        name="combine",
    )(dest_rows, y_sorted, h1, route, ln2_g, ln2_b)


def _dispatch_plan(route_t, counts, bm, n_slabs):
    n = route_t.shape[1]
    expert_id = route_t[0:TOP_K].astype(jnp.int32)
    rank = route_t[4:4 + TOP_K].astype(jnp.int32)
    padded = (counts + bm - 1) // bm * bm
    pad_end = jnp.cumsum(padded)
    pad_start = pad_end - padded
    experts = jnp.arange(N_EXPERTS, dtype=jnp.int32)[:, None, None]
    dest = jnp.sum(jnp.where(expert_id[None] == experts, pad_start[:, None, None], 0), axis=0) + rank
    n_blocks = (n * TOP_K + bm - 1) // bm + N_EXPERTS
    block_start = jnp.arange(n_blocks, dtype=jnp.int32) * bm
    block_expert = jnp.minimum(jnp.sum((block_start[:, None] >= pad_end[None, :]).astype(jnp.int32), axis=1),
                               N_EXPERTS - 1).astype(jnp.int32)
    n_used = (pad_end[-1:] // bm).astype(jnp.int32)
    dest_rows = (dest * n_slabs).reshape(-1).astype(jnp.int32)
    pad_lo = jnp.concatenate([pad_start + counts, pad_end[-1:]]).astype(jnp.int32)
    pad_hi = jnp.concatenate([pad_end, jnp.full((1,), n_blocks * bm, pad_end.dtype)]).astype(jnp.int32)
    return dest_rows, pad_lo, pad_hi, block_expert, n_used, n_blocks


def _pad_lanes(w, width=LANES):
    return jnp.pad(w, [(0, 0)] * (w.ndim - 1) + [(0, width - w.shape[-1])])


def kernel(x, meta_tokens, ln_emb_g, ln_emb_b, w_in, conv_qkv_w, a_log, dt_bias, dn_norm_g, conv_dw_w, conv_dw_b, cv_norm_g, cv_norm_b, w_out, ln1_g, ln1_b, w_group, b_group, w_router, b_router, w_exp_gate, w_exp_up, w_exp_down, ln2_g, ln2_b):
    depth = w_in.shape[0]
    assert depth == 1, "single-layer block"
    bsz, seq, d = x.shape
    alpha = (2.0 * depth) ** 0.25
    qkv_w = 3 * DN_WIDTH
    w_in0 = w_in[0]
    glu_off = 4 * DN_WIDTH + 2 * DN_HEADS
    row = lambda a: a.reshape(1, -1).astype(F32)
    p = {
        'ln_emb_g': row(ln_emb_g), 'ln_emb_b': row(ln_emb_b),
        'w_qkv': w_in0[:, :qkv_w].astype(BF16),
        'w_z': w_in0[:, qkv_w:4 * DN_WIDTH].astype(BF16),
        'w_ba': _pad_lanes(w_in0[:, 4 * DN_WIDTH:glu_off]).astype(BF16),
        'w_glu': w_in0[:, glu_off:].astype(BF16),
        'conv_w': conv_qkv_w[0].astype(F32),
        'neg_a': _pad_lanes(jnp.concatenate([jnp.zeros((DN_HEADS,), F32), -jnp.exp(a_log[0].astype(F32))])[None]),
        'dt_b': _pad_lanes(jnp.concatenate([jnp.zeros((DN_HEADS,), F32), dt_bias[0].astype(F32)])[None]),
        'dw_w': conv_dw_w[0].astype(F32), 'dw_b': row(conv_dw_b[0]),
        'cv_g': row(cv_norm_g[0]), 'cv_b': row(cv_norm_b[0]),
        'w_out': w_out[0].astype(BF16), 'ln1_g': row(ln1_g[0]), 'ln1_b': row(ln1_b[0]),
    }
    w_r = _pad_lanes(jnp.concatenate([w_group[0], w_router[0]], axis=1).astype(F32))
    p['w_r_hi'] = w_r.astype(BF16)
    p['w_r_lo'] = (w_r - p['w_r_hi'].astype(F32)).astype(BF16)
    p['b_r'] = _pad_lanes(jnp.concatenate([b_group[0], b_router[0]])[None].astype(F32))
    gain = row(dn_norm_g[0])

    conf_w = p['dw_w'].shape[1]
    zero_hq = jnp.zeros((QKV_HALO, qkv_w), F32)
    zero_hc = jnp.zeros((CONF_HALO, conf_w), F32)
    m_mixed, mbg, halo_q, halo_c, _ = _mix_in(meta_tokens[None].astype(F32), p, zero_hq, zero_hc, N_META)
    front = lambda a: jnp.pad(a, [(0, 0), (CHUNK - N_META, 0), (0, 0)])
    s_zero = jnp.zeros((DN_HEADS, HEAD_DIM, HEAD_DIM), F32)
    _, s_meta = _delta(front(m_mixed), front(mbg), s_zero, gain, 1)

    mixed, bg, _, _, h = _mix_in(x, p, halo_q[0], halo_c[0], TM_IN)
    o, _ = _delta(mixed, bg, s_meta[0], gain, DELTA_CHUNKS)

    n = bsz * seq
    h1, h1p, route, route_t, cnt = _mix_out(h.reshape(n, d), o.reshape(n, DN_WIDTH),
                                            mixed.reshape(n, mixed.shape[2]), p, halo_c[0], seq, TM_OUT, alpha)

    n_slabs = d // (2 * LANES)
    counts = cnt[0, :N_EXPERTS].astype(jnp.int32)
    dest_rows, pad_lo, pad_hi, block_expert, n_used, n_blocks = _dispatch_plan(route_t, counts, BM_EXPERT, n_slabs)
    xs = _dispatch(dest_rows, pad_lo, pad_hi, h1p, n_blocks * BM_EXPERT, TD_DISPATCH, n_slabs, BM_EXPERT)
    first_layer = lambda w: w.reshape(w.shape[1:])
    y_sorted = _experts(block_expert, n_used, xs, first_layer(w_exp_gate), first_layer(w_exp_up),
                        first_layer(w_exp_down), BM_EXPERT)
    out = _combine(dest_rows, y_sorted, h1, route, row(ln2_g[0]), row(ln2_b[0]), TM_COMBINE, alpha)
    return out.reshape(bsz, seq, d)
```

```python
import functools

import jax
import jax.numpy as jnp
from jax import lax
from jax.experimental import pallas as pl
from jax.experimental.pallas import tpu as pltpu

F32 = jnp.float32
BF16 = jnp.bfloat16

NORM_EPS = 1e-5
N_META = 16
DN_HEADS = 4
HEAD_DIM = 128
DN_WIDTH = DN_HEADS * HEAD_DIM
CHUNK = 64
SHORT_CONV = 4
CONF_KERNEL = 31
N_GROUPS = 4
EXPERTS_PER_GROUP = 8
N_EXPERTS = N_GROUPS * EXPERTS_PER_GROUP
TOP_K = 2
LANES = 128
SUBLANES = 8
QKV_HALO = 8
CONF_HALO = 32
VMEM_LIMIT = 56 * 1024 * 1024

TM_IN = 1024
DELTA_CHUNKS = 8
TM_OUT = 1024
BM_EXPERT = 512
TM_COMBINE = 512
TD_DISPATCH = 512
RING = 3
SUB_ROWS = 256
SUB_ROWS_OUT = 512
CONV_BLOCK_ROWS = 64
CONV_BLOCK_COLS = 512
MIXED_COLS = 5
COL_Q, COL_K, COL_V, COL_Z, COL_C = range(MIXED_COLS)
DMA_UNROLL = 8


def _dot(a, b):
    return jnp.dot(a, b, preferred_element_type=F32)


def _split2(x):
    hi = x.astype(BF16)
    lo = (x - hi.astype(F32)).astype(BF16)
    return hi, lo


def _sigmoid(x):
    return 1.0 / (1.0 + jnp.exp(-x))


def _silu(x):
    return x * _sigmoid(x)


def _layer_norm(x, g, b):
    mu = jnp.mean(x, axis=-1, keepdims=True)
    xc = x - mu
    var = jnp.mean(xc * xc, axis=-1, keepdims=True)
    return xc * lax.rsqrt(var + NORM_EPS) * g + b


def _pack_bf16_pairs(x):
    half = x.shape[1] // 2
    lo = lax.bitcast_convert_type(x[:, :half].astype(BF16).astype(F32), jnp.uint32)
    hi = lax.bitcast_convert_type(x[:, half:].astype(BF16).astype(F32), jnp.uint32)
    return (lo >> 16) | (hi & jnp.uint32(0xFFFF0000))


def _unpack_bf16_pairs(p):
    lo = lax.bitcast_convert_type(p << 16, F32)
    hi = lax.bitcast_convert_type(p & jnp.uint32(0xFFFF0000), F32)
    return jnp.concatenate([lo, hi], axis=1)


def _store_slabs(ref, val):
    rows, d = val.shape
    n_slabs = d // LANES
    for s in range(n_slabs):
        ref[pl.ds(s, rows, stride=n_slabs), :] = val[:, s * LANES:(s + 1) * LANES]


def _load_slabs(ref, rows, n_slabs, base=0):
    return jnp.concatenate([ref[pl.ds(base + s, rows, stride=n_slabs), :] for s in range(n_slabs)], axis=1)


def _full_spec(shape):
    nd = len(shape)
    return pl.BlockSpec(shape, lambda *_: (0,) * nd)


def _causal_depthwise(ext_ref, w_ref, n_taps, halo, tm):
    ext = ext_ref[...]
    rows, cols = ext.shape
    first = halo - (n_taps - 1)
    groups = ext.reshape(rows // SUBLANES, SUBLANES, cols)
    sub = lax.broadcasted_iota(jnp.int32, (1, SUBLANES, 1), 1)

    def shifted(phase):
        rolled = pltpu.roll(groups, SUBLANES - phase, axis=1)
        nxt = jnp.concatenate([rolled[1:], rolled[:1]], axis=0)
        return jnp.where(sub < SUBLANES - phase, rolled, nxt).reshape(rows, cols)

    phases = {}
    for k in range(n_taps):
        phase = (first + k) % SUBLANES
        if phase not in phases:
            phases[phase] = ext if phase == 0 else shifted(phase)

    rb = min(tm, CONV_BLOCK_ROWS)
    cb = min(cols, CONV_BLOCK_COLS)
    out_rows = []
    for r0 in range(0, tm, rb):
        out_cols = []
        for c0 in range(0, cols, cb):
            acc = None
            for k in range(n_taps):
                phase = (first + k) % SUBLANES
                base = first + k - phase + r0
                term = phases[phase][base:base + rb, c0:c0 + cb] * w_ref[k:k + 1, c0:c0 + cb]
                acc = term if acc is None else acc + term
            out_cols.append(acc)
        out_rows.append(out_cols[0] if len(out_cols) == 1 else jnp.concatenate(out_cols, axis=1))
    return out_rows[0] if len(out_rows) == 1 else jnp.concatenate(out_rows, axis=0)


def _row_views(refs, r0, rows, lead=0):
    return [ref.at[pl.ds(r0, lead + rows), :] for ref in refs]


def _mix_in_kernel(x_ref, lng_ref, lnb_ref, wqkv_ref, wz_ref, wglu_ref, wba_ref, cw_ref, nega_ref,
                   dtb_ref, hq_in_ref, hc_in_ref,
                   m_ref, bg_ref, hq_out_ref, hc_out_ref, h_ref,
                   qkv_ext, c_ext):
    tm = x_ref.shape[0]

    @pl.when(pl.program_id(1) == 0)
    def _():
        qkv_ext[0:QKV_HALO, :] = hq_in_ref[...]
        c_ext[0:CONF_HALO, :] = hc_in_ref[...]

    th = min(tm, SUB_ROWS)
    for r0 in range(0, tm, th):
        x_v, bg_v, h_v = _row_views((x_ref, bg_ref, h_ref), r0, th)
        q_v, k_v, v_v, z_v, c_v = (m_ref.at[pl.ds(r0, th), pl.ds(col * DN_WIDTH, DN_WIDTH)]
                                   for col in range(MIXED_COLS))
        qkv_v, = _row_views((qkv_ext,), r0, th, QKV_HALO)
        cext_v, = _row_views((c_ext,), r0, th, CONF_HALO)
        _mix_in_rows(x_v, lng_ref, lnb_ref, wqkv_ref, wz_ref, wglu_ref, wba_ref, cw_ref, nega_ref, dtb_ref,
                     q_v, k_v, v_v, z_v, c_v, bg_v, h_v, qkv_v, cext_v)

    q_tail = qkv_ext[tm:tm + QKV_HALO, :]
    c_tail = c_ext[tm:tm + CONF_HALO, :]
    qkv_ext[0:QKV_HALO, :] = q_tail
    c_ext[0:CONF_HALO, :] = c_tail
    hq_out_ref[...] = q_tail
    hc_out_ref[...] = c_tail


def _mix_in_rows(x_ref, lng_ref, lnb_ref, wqkv_ref, wz_ref, wglu_ref, wba_ref, cw_ref, nega_ref, dtb_ref,
                 q_ref, k_ref, v_ref, z_ref, c_ref, bg_ref, h_ref, qkv_ext, c_ext):
    tm = x_ref.shape[0]
    h = _layer_norm(x_ref[...], lng_ref[...], lnb_ref[...])
    h_ref[...] = h
    hb = h.astype(BF16)

    qkv_ext[QKV_HALO:QKV_HALO + tm, :] = _dot(hb, wqkv_ref[...])
    qkv = _silu(_causal_depthwise(qkv_ext, cw_ref, SHORT_CONV, QKV_HALO, tm))
    for hd in range(DN_HEADS):
        lo = hd * HEAD_DIM
        qh = qkv[:, lo:lo + HEAD_DIM]
        kh = qkv[:, DN_WIDTH + lo:DN_WIDTH + lo + HEAD_DIM]
        q_ref[:, lo:lo + HEAD_DIM] = (qh * (lax.rsqrt(jnp.sum(qh * qh, axis=-1, keepdims=True) + 1e-6)
                                            * (HEAD_DIM ** -0.5))).astype(q_ref.dtype)
        k_ref[:, lo:lo + HEAD_DIM] = (kh * lax.rsqrt(jnp.sum(kh * kh, axis=-1, keepdims=True) + 1e-6)
                                      ).astype(k_ref.dtype)
    v_ref[...] = qkv[:, 2 * DN_WIDTH:].astype(v_ref.dtype)
    z_ref[...] = _dot(hb, wz_ref[...]).astype(z_ref.dtype)

    ba = _dot(hb, wba_ref[...])
    lane = lax.broadcasted_iota(jnp.int32, ba.shape, 1)
    sp_in = ba + dtb_ref[...]
    softplus = jnp.maximum(sp_in, 0.0) + jnp.log(1.0 + jnp.exp(-jnp.abs(sp_in)))
    bg_ref[...] = jnp.where(lane < DN_HEADS, _sigmoid(ba), nega_ref[...] * softplus)

    glu = _dot(hb, wglu_ref[...])
    cw = glu.shape[1] // 2
    c_pre = glu[:, :cw] * _sigmoid(glu[:, cw:])
    c_ref[...] = c_pre.astype(c_ref.dtype)
    c_ext[CONF_HALO:CONF_HALO + tm, :] = c_pre


def _mix_in(x, p, halo_q, halo_c, tm):
    bsz, seq, d = x.shape
    assert seq % tm == 0
    qkv_w = 3 * DN_WIDTH
    conf_w = p['dw_w'].shape[1]

    def row(width):
        return pl.BlockSpec((None, tm, width), lambda b, t: (b, t, 0))

    def per_batch(rows, width):
        return pl.BlockSpec((None, rows, width), lambda b, t: (b, 0, 0))

    consts = [p['ln_emb_g'], p['ln_emb_b'], p['w_qkv'], p['w_z'], p['w_glu'], p['w_ba'], p['conv_w'],
              p['neg_a'], p['dt_b'], halo_q, halo_c]
    sds = jax.ShapeDtypeStruct
    assert conf_w == DN_WIDTH, "q, k, v, z and the conformer channels share one array of equal-width column blocks"
    out_shape = [sds((bsz, seq, MIXED_COLS * DN_WIDTH), BF16),
                 sds((bsz, seq, LANES), F32), sds((bsz, QKV_HALO, qkv_w), F32),
                 sds((bsz, CONF_HALO, conf_w), F32), sds((bsz, seq, d), F32)]
    out_specs = [row(MIXED_COLS * DN_WIDTH), row(LANES), per_batch(QKV_HALO, qkv_w),
                 per_batch(CONF_HALO, conf_w), row(d)]
    return pl.pallas_call(
        _mix_in_kernel,
        grid=(bsz, seq // tm),
        in_specs=[row(d)] + [_full_spec(c.shape) for c in consts],
        out_specs=out_specs,
        out_shape=out_shape,
        scratch_shapes=[pltpu.VMEM((QKV_HALO + tm, qkv_w), F32), pltpu.VMEM((CONF_HALO + tm, conf_w), F32)],
        compiler_params=pltpu.CompilerParams(dimension_semantics=("parallel", "arbitrary"),
                                             vmem_limit_bytes=VMEM_LIMIT),
        name="mix_in",
    )(x, *consts)


def _bmm(a, b):
    return jnp.einsum('nij,njk->nik', a, b, preferred_element_type=F32)


def _delta_kernel(m_ref, z_ref, bg_ref, s0_ref, gain_ref, o_ref, sfin_ref,
                  s_ref, u_s, wq_s, attn_s, kd_s, egl_s, *, chunks, blocks_per_seq):
    j = pl.program_id(0)
    slot = j % 2
    prev = 1 - slot
    gain = gain_ref[...]

    @pl.when(j == 0)
    def _():
        u_s[1] = jnp.zeros(u_s.shape[1:], u_s.dtype)
        wq_s[1] = jnp.zeros(wq_s.shape[1:], wq_s.dtype)
        attn_s[1] = jnp.zeros(attn_s.shape[1:], attn_s.dtype)
        kd_s[1] = jnp.zeros(kd_s.shape[1:], kd_s.dtype)
        egl_s[1] = jnp.zeros(egl_s.shape[1:], egl_s.dtype)

    def recurrence():
        live = j > 0
        restart = (jnp.maximum(j - 1, 0) % blocks_per_seq) == 0
        start = [jnp.where(restart, s0_ref[hd], s_ref[hd]) for hd in range(DN_HEADS)]
        state = list(start)
        for c in range(chunks):
            idx = [hd * chunks + c for hd in range(DN_HEADS)]
            wq = [_dot(wq_s[prev, n], state[hd].astype(BF16)) for hd, n in enumerate(idx)]
            yield
            v_new = [(u_s[prev, n] - wq[hd][:CHUNK]).astype(BF16) for hd, n in enumerate(idx)]
            o = [wq[hd][CHUNK:] + _dot(attn_s[prev, n], v_new[hd]) for hd, n in enumerate(idx)]
            state = [state[hd] * egl_s[prev, n][0:1, :]
                     + lax.dot_general(kd_s[prev, n], v_new[hd], (((0,), (0,)), ((), ())),
                                       preferred_element_type=F32)
                     for hd, n in enumerate(idx)]
            yield
            for hd in range(DN_HEADS):
                cols = slice(hd * HEAD_DIM, (hd + 1) * HEAD_DIM)
                rows = slice(c * CHUNK, (c + 1) * CHUNK)
                r = o[hd] * lax.rsqrt(jnp.mean(o[hd] * o[hd], axis=-1, keepdims=True) + 1e-6)
                o_ref[rows, cols] = (r * gain * _silu(z_ref[rows, cols].astype(F32))).astype(o_ref.dtype)
        for hd in range(DN_HEADS):
            kept = jnp.where(live, state[hd], start[hd])
            s_ref[hd] = kept
            sfin_ref[hd] = kept

    def preparation():
        yield from _delta_prepare(m_ref, bg_ref, u_s, wq_s, attn_s, kd_s, egl_s, slot, chunks)

    halves = [recurrence(), preparation()]
    while halves:
        for gen in list(halves):
            try:
                next(gen)
            except StopIteration:
                halves.remove(gen)


def _delta_prepare(m_ref, bg_ref, u_s, wq_s, attn_s, kd_s, egl_s, slot, chunks):
    ii = lax.broadcasted_iota(jnp.int32, (CHUNK, CHUNK), 0)
    jj = lax.broadcasted_iota(jnp.int32, (CHUNK, CHUNK), 1)
    causal = ii >= jj
    strict = ii > jj
    eye = (ii == jj).astype(F32)
    bg3 = bg_ref[...].reshape(chunks, CHUNK, LANES)
    tril_b = jnp.broadcast_to(causal.astype(BF16), (chunks, CHUNK, CHUNK))
    p1 = bg3.astype(BF16)
    r1 = bg3 - p1.astype(F32)
    p2 = r1.astype(BF16)
    p3 = (r1 - p2.astype(F32)).astype(BF16)
    gc3 = _bmm(tril_b, p1) + _bmm(tril_b, p2) + _bmm(tril_b, p3)
    yield

    def heads(col):
        lo = col * DN_WIDTH
        return jnp.concatenate([m_ref[:, lo + hd * HEAD_DIM:lo + (hd + 1) * HEAD_DIM].astype(F32)
                                .reshape(chunks, CHUNK, HEAD_DIM) for hd in range(DN_HEADS)], axis=0)
    q = heads(COL_Q)
    k = heads(COL_K)
    v = heads(COL_V)
    bet = jnp.concatenate([bg3[:, :, hd:hd + 1] for hd in range(DN_HEADS)], axis=0)
    gc = jnp.concatenate([gc3[:, :, DN_HEADS + hd:DN_HEADS + hd + 1] for hd in range(DN_HEADS)], axis=0)
    gc_t = [gc3[c].T for c in range(chunks)]
    decay = jnp.stack([
        jnp.exp(jnp.where(causal, gc3[c][:, DN_HEADS + hd:DN_HEADS + hd + 1]
                          - gc_t[c][DN_HEADS + hd:DN_HEADS + hd + 1, :], -jnp.inf))
        for hd in range(DN_HEADS) for c in range(chunks)], axis=0)

    kb = k * bet
    g_all = jnp.einsum('nid,njd->nij', jnp.concatenate([kb, q], axis=1).astype(BF16), k.astype(BF16),
                       preferred_element_type=F32)
    yield
    a_low = jnp.where(strict, g_all[:, :CHUNK] * decay, 0.0)
    attn = (g_all[:, CHUNK:] * decay).astype(BF16)

    l_mat = eye + a_low
    l_bf = l_mat.astype(BF16)
    t_mat = eye - a_low
    for _ in range(4):
        res = eye - _bmm(l_bf, t_mat.astype(BF16))
        yield
        t_mat = t_mat + _bmm(t_mat.astype(BF16), res.astype(BF16))
        yield
    l_lo = (l_mat - l_bf.astype(F32)).astype(BF16)
    t_hi = t_mat.astype(BF16)
    t_lo = (t_mat - t_hi.astype(F32)).astype(BF16)
    res = eye - (_bmm(l_bf, t_hi) + _bmm(l_lo, t_hi) + _bmm(l_bf, t_lo))
    yield
    t_mat = t_mat + _bmm(t_hi, res.astype(BF16))
    yield

    eg = jnp.exp(gc)
    uw = _bmm(t_mat.astype(BF16), jnp.concatenate([v * bet, kb * eg], axis=2).astype(BF16))
    yield
    u = uw[:, :, :HEAD_DIM]
    wq_lhs = jnp.concatenate([uw[:, :, HEAD_DIM:], q * eg], axis=1).astype(BF16)
    g_last = gc[:, CHUNK - 1:CHUNK, :]
    k_dec = (k * jnp.exp(g_last - gc)).astype(BF16)
    eg_last = jnp.exp(g_last)

    u_s[slot] = u
    wq_s[slot] = wq_lhs
    attn_s[slot] = attn
    kd_s[slot] = k_dec
    egl_s[slot] = jnp.broadcast_to(eg_last, egl_s.shape[1:])


def _delta(mixed, bg, s0, gain, chunks):
    bsz, seq, _ = mixed.shape
    rows = chunks * CHUNK
    assert seq % rows == 0

    nj = seq // rows
    nb = DN_HEADS * chunks
    total = bsz * nj

    def prep(width):
        def index(j):
            blk = jnp.minimum(j, total - 1)
            return (blk // nj, blk % nj, 0)
        return pl.BlockSpec((None, rows, width), index)

    def scan(width, col=0):
        def index(j):
            blk = jnp.maximum(j - 1, 0)
            return (blk // nj, blk % nj, col)
        return pl.BlockSpec((None, rows, width), index)

    state_shape = (DN_HEADS, HEAD_DIM, HEAD_DIM)
    return pl.pallas_call(
        functools.partial(_delta_kernel, chunks=chunks, blocks_per_seq=nj),
        grid=(total + 1,),
        in_specs=[prep(mixed.shape[2]), scan(DN_WIDTH, COL_Z), prep(LANES), _full_spec(state_shape),
                  _full_spec(gain.shape)],
        out_specs=[scan(DN_WIDTH),
                   pl.BlockSpec((None,) + state_shape, lambda j: (jnp.maximum(j - 1, 0) // nj, 0, 0, 0))],
        out_shape=[jax.ShapeDtypeStruct((bsz, seq, DN_WIDTH), BF16),
                   jax.ShapeDtypeStruct((bsz,) + state_shape, F32)],
        scratch_shapes=[pltpu.VMEM(state_shape, F32),
                        pltpu.VMEM((2, nb, CHUNK, HEAD_DIM), F32),
                        pltpu.VMEM((2, nb, 2 * CHUNK, HEAD_DIM), BF16),
                        pltpu.VMEM((2, nb, CHUNK, CHUNK), BF16),
                        pltpu.VMEM((2, nb, CHUNK, HEAD_DIM), BF16),
                        pltpu.VMEM((2, nb, SUBLANES, HEAD_DIM), F32)],
        compiler_params=pltpu.CompilerParams(dimension_semantics=("arbitrary",), vmem_limit_bytes=VMEM_LIMIT),
        name="delta",
    )(mixed, mixed, bg, s0, gain)


def _mix_out_kernel(h_ref, o_ref, c_ref, wo_ref, g1_ref, b1_ref, wrh_ref, wrl_ref, br_ref,
                    dww_ref, dwb_ref, cvg_ref, cvb_ref, hc_in_ref,
                    h1_ref, h1p_ref, route_ref, route_t_ref, cnt_out_ref, cnt_ref, c_ext, *, alpha, tiles_per_seq):
    tm = h_ref.shape[0]

    @pl.when(pl.program_id(0) == 0)
    def _():
        cnt_ref[...] = jnp.zeros_like(cnt_ref)

    @pl.when(pl.program_id(0) % tiles_per_seq == 0)
    def _():
        c_ext[0:CONF_HALO, :] = hc_in_ref[...]

    th = min(tm, SUB_ROWS_OUT)
    n_slabs = h1p_ref.shape[0] // tm
    for r0 in range(0, tm, th):
        h_v, o_v, c_v, h1_v, route_v = _row_views((h_ref, o_ref, c_ref, h1_ref, route_ref), r0, th)
        cext_v, = _row_views((c_ext,), r0, th, CONF_HALO)
        h1p_v, = _row_views((h1p_ref,), r0 * n_slabs, th * n_slabs)
        _mix_out_rows(h_v, o_v, c_v, wo_ref, g1_ref, b1_ref, wrh_ref, wrl_ref, br_ref, dww_ref, dwb_ref, cvg_ref,
                      cvb_ref, h1_v, h1p_v, route_v, route_t_ref.at[:, pl.ds(r0, th)], cnt_ref, cext_v, alpha)
    c_ext[0:CONF_HALO, :] = c_ext[tm:tm + CONF_HALO, :]
    cnt_out_ref[...] = jnp.broadcast_to(cnt_ref[...], cnt_out_ref.shape)


def _mix_out_rows(h_ref, o_ref, c_ref, wo_ref, g1_ref, b1_ref, wrh_ref, wrl_ref, br_ref, dww_ref, dwb_ref, cvg_ref,
                  cvb_ref, h1_ref, h1p_ref, route_ref, route_t_ref, cnt_ref, c_ext, alpha):
    tm = h_ref.shape[0]

    c_ext[CONF_HALO:CONF_HALO + tm, :] = c_ref[...].astype(F32)
    conv = _causal_depthwise(c_ext, dww_ref, CONF_KERNEL, CONF_HALO, tm) + dwb_ref[...]
    conf = _silu(_layer_norm(conv, cvg_ref[...], cvb_ref[...]))

    h = h_ref[...]
    dn = o_ref.shape[1]
    mix = _dot(o_ref[...].astype(BF16), wo_ref[0:dn, :]) + _dot(conf.astype(BF16), wo_ref[dn:, :])
    h1 = _layer_norm(alpha * h + mix, g1_ref[...], b1_ref[...])
    h1_ref[...] = h1
    _store_slabs(h1p_ref, _pack_bf16_pairs(h1))

    hh, hl = _split2(h1)
    logits = _dot(hh, wrh_ref[...]) + _dot(hl, wrh_ref[...]) + _dot(hh, wrl_ref[...]) + br_ref[...]
    lane = lax.broadcasted_iota(jnp.int32, logits.shape, 1).astype(F32)
    big = float(LANES)
    neg = -jnp.inf

    def first_argmax(vals):
        top = jnp.max(vals, axis=-1, keepdims=True)
        return top, jnp.min(jnp.where(vals == top, lane, big), axis=-1, keepdims=True)

    grp = jnp.where(lane < N_GROUPS, logits, neg)
    g_top, g_sel = first_argmax(grp)
    p_group = 1.0 / jnp.sum(jnp.exp(grp - g_top), axis=-1, keepdims=True)
    lo = N_GROUPS + EXPERTS_PER_GROUP * g_sel
    in_grp = jnp.where((lane >= lo) & (lane < lo + EXPERTS_PER_GROUP), logits, neg)
    m1, i1 = first_argmax(in_grp)
    m2, i2 = first_argmax(jnp.where(lane == i1, neg, in_grp))
    s = jnp.exp(m2 - m1)
    w1 = p_group / (1.0 + s)
    w2 = p_group * s / (1.0 + s)
    e1 = i1 - N_GROUPS
    e2 = i2 - N_GROUPS

    tm = logits.shape[0]
    oh1 = (lane == e1).astype(F32)
    oh2 = (lane == e2).astype(F32)
    both = oh1 + oh2
    ti = lax.broadcasted_iota(jnp.int32, (tm, tm), 0)
    tj = lax.broadcasted_iota(jnp.int32, (tm, tm), 1)
    base = _dot((ti > tj).astype(BF16), both.astype(BF16)) + cnt_ref[...]
    r1 = jnp.sum(oh1 * base, axis=-1, keepdims=True)
    r2 = jnp.sum(oh2 * base, axis=-1, keepdims=True)
    cnt_ref[...] = cnt_ref[...] + jnp.sum(both, axis=0, keepdims=True)

    vals = (e1, e2, w1, w2, r1, r2)
    route = jnp.zeros_like(logits)
    for idx, val in enumerate(vals):
        route = jnp.where(lane == idx, val, route)
    route_ref[...] = route
    route_t_ref[...] = route.T[0:SUBLANES, :]


def _mix_out(h2d, o2d, mixed2d, p, halo_c, seq, tm, alpha):
    n, d = h2d.shape
    assert seq % tm == 0
    slabs = d // (2 * LANES)
    conf_w = p['dw_w'].shape[1]

    def row(width):
        return pl.BlockSpec((tm, width), lambda i: (i, 0))

    consts = [p['w_out'], p['ln1_g'], p['ln1_b'], p['w_r_hi'], p['w_r_lo'], p['b_r'],
              p['dw_w'], p['dw_b'], p['cv_g'], p['cv_b'], halo_c]
    return pl.pallas_call(
        functools.partial(_mix_out_kernel, alpha=alpha, tiles_per_seq=seq // tm),
        grid=(n // tm,),
        in_specs=[row(d), row(o2d.shape[1]), pl.BlockSpec((tm, conf_w), lambda i: (i, COL_C))]
                 + [_full_spec(c.shape) for c in consts],
        out_specs=[row(d), pl.BlockSpec((tm * slabs, LANES), lambda i: (i, 0)), row(LANES),
                   pl.BlockSpec((SUBLANES, tm), lambda i: (0, i)), _full_spec((SUBLANES, LANES))],
        out_shape=[jax.ShapeDtypeStruct((n, d), F32), jax.ShapeDtypeStruct((n * slabs, LANES), jnp.uint32),
                   jax.ShapeDtypeStruct((n, LANES), F32),
                   jax.ShapeDtypeStruct((SUBLANES, n), F32), jax.ShapeDtypeStruct((SUBLANES, LANES), F32)],
        scratch_shapes=[pltpu.VMEM((1, LANES), F32), pltpu.VMEM((CONF_HALO + tm, conf_w), F32)],
        compiler_params=pltpu.CompilerParams(dimension_semantics=("arbitrary",), vmem_limit_bytes=VMEM_LIMIT),
        name="mix_out",
    )(h2d, o2d, mixed2d, *consts)


def _dispatch_kernel(dest_ref, pad_lo_ref, pad_hi_ref, h1s_hbm, xs_hbm, ring, zslab, fsem, ssem, zsem, *,
                     td, n_slabs, n_tokens):
    i = pl.program_id(0)
    nb = pl.num_programs(0)
    slot = i % RING
    tile_rows = td * n_slabs

    def fetch(step):
        start = pl.multiple_of(step * tile_rows, tile_rows)
        return pltpu.make_async_copy(h1s_hbm.at[pl.ds(start, tile_rows), :], ring.at[step % RING],
                                     fsem.at[step % RING])

    def wait_scatter(step):
        for _ in range(TOP_K):
            pltpu.make_async_copy(ring.at[step % RING], xs_hbm.at[pl.ds(0, tile_rows), :],
                                  ssem.at[step % RING]).wait()

    zrows = zslab.shape[0] // n_slabs

    def pad_copy(row, size):
        return pltpu.make_async_copy(
            zslab.at[pl.ds(0, size * n_slabs), :],
            xs_hbm.at[pl.ds(pl.multiple_of(row * n_slabs, n_slabs), size * n_slabs), :], zsem.at[0])

    def for_each_pad_copy(fn):
        def per_expert(e, carry):
            lo = pad_lo_ref[e]
            length = pad_hi_ref[e] - lo
            for b in range(zrows.bit_length()):
                size = 1 << b

                @pl.when(((length >> b) & 1) == 1)
                def _():
                    fn(pad_copy(lo + (length & (size - 1)), size))
            return carry
        lax.fori_loop(0, N_EXPERTS, per_expert, 0)
        tail_lo = pad_lo_ref[N_EXPERTS]

        def per_piece(piece, carry):
            fn(pad_copy(tail_lo + piece * zrows, zrows))
            return carry
        lax.fori_loop(0, (pad_hi_ref[N_EXPERTS] - tail_lo) // zrows, per_piece, 0)

    @pl.when(i == 0)
    def _():
        fetch(0).start()
        zslab[...] = jnp.zeros_like(zslab)
        for_each_pad_copy(lambda cp: cp.start())

    @pl.when((i == 0) & (nb > 1))
    def _():
        fetch(1).start()

    fetch(i).wait()

    def issue_body(r, carry):
        src = ring.at[slot, pl.ds(pl.multiple_of(r * n_slabs, n_slabs), n_slabs), :]
        for k in range(TOP_K):
            dst_row = pl.multiple_of(dest_ref[k * n_tokens + i * td + r], n_slabs)
            pltpu.make_async_copy(src, xs_hbm.at[pl.ds(dst_row, n_slabs), :], ssem.at[slot]).start(priority=k)
        return carry
    lax.fori_loop(0, td, issue_body, 0, unroll=DMA_UNROLL)

    @pl.when(i > 0)
    def _():
        wait_scatter(i - 1)

    @pl.when(i + 2 < nb)
    def _():
        fetch(i + 2).start()

    @pl.when(i == nb - 1)
    def _():
        wait_scatter(i)
        for_each_pad_copy(lambda cp: cp.wait())


def _dispatch(dest_rows, pad_lo, pad_hi, h1s, cap_rows, td, n_slabs, bm):
    n = h1s.shape[0] // n_slabs
    assert n % td == 0
    grid_spec = pltpu.PrefetchScalarGridSpec(
        num_scalar_prefetch=3,
        grid=(n // td,),
        in_specs=[pl.BlockSpec(memory_space=pl.ANY)],
        out_specs=pl.BlockSpec(memory_space=pl.ANY),
        scratch_shapes=[pltpu.VMEM((RING, td * n_slabs, LANES), h1s.dtype),
                        pltpu.VMEM((bm // 2 * n_slabs, LANES), h1s.dtype),
                        pltpu.SemaphoreType.DMA((RING,)), pltpu.SemaphoreType.DMA((RING,)),
                        pltpu.SemaphoreType.DMA((1,))],
    )
    return pl.pallas_call(
        functools.partial(_dispatch_kernel, td=td, n_slabs=n_slabs, n_tokens=n),
        grid_spec=grid_spec,
        out_shape=jax.ShapeDtypeStruct((cap_rows * n_slabs, LANES), h1s.dtype),
        compiler_params=pltpu.CompilerParams(dimension_semantics=("arbitrary",), disable_bounds_checks=True),
        name="dispatch",
    )(dest_rows, pad_lo, pad_hi, h1s)


def _expert_kernel(be_ref, nu_ref, xs_ref, wg_ref, wu_ref, wd_ref, y_ref, wg_bf, wu_bf, wd_bf, *, bm, n_slabs):
    i = pl.program_id(0)
    used = i < nu_ref[0]

    @pl.when(used & ((i == 0) | (be_ref[i] != be_ref[jnp.maximum(i - 1, 0)])))
    def _():
        wg_bf[...] = wg_ref[...].astype(BF16)
        wu_bf[...] = wu_ref[...].astype(BF16)
        wd_bf[...] = wd_ref[...].astype(BF16)

    @pl.when(used)
    def _():
        xb = _unpack_bf16_pairs(_load_slabs(xs_ref, bm, n_slabs)).astype(BF16)
        hid = _silu(_dot(xb, wg_bf[...])) * _dot(xb, wu_bf[...])
        _store_slabs(y_ref, _pack_bf16_pairs(_dot(hid.astype(BF16), wd_bf[...])))

    @pl.when(jnp.logical_not(used))
    def _():
        y_ref[...] = jnp.zeros_like(y_ref)


def _experts(block_expert, n_used, xs, w_gate, w_up, w_down, bm):
    n_blocks = block_expert.shape[0]
    d = w_gate.shape[1]
    ff = w_gate.shape[2]
    n_slabs = d // (2 * LANES)

    def blk(i, be, nu):
        return jnp.minimum(i, nu[0] - 1)

    grid_spec = pltpu.PrefetchScalarGridSpec(
        num_scalar_prefetch=2,
        grid=(n_blocks,),
        in_specs=[pl.BlockSpec((bm * n_slabs, LANES), lambda i, be, nu: (blk(i, be, nu), 0)),
                  pl.BlockSpec((None, d, ff), lambda i, be, nu: (be[blk(i, be, nu)], 0, 0)),
                  pl.BlockSpec((None, d, ff), lambda i, be, nu: (be[blk(i, be, nu)], 0, 0)),
                  pl.BlockSpec((None, ff, d), lambda i, be, nu: (be[blk(i, be, nu)], 0, 0))],
        out_specs=pl.BlockSpec((bm * n_slabs, LANES), lambda i, be, nu: (i, 0)),
        scratch_shapes=[pltpu.VMEM((d, ff), BF16), pltpu.VMEM((d, ff), BF16), pltpu.VMEM((ff, d), BF16)],
    )
    return pl.pallas_call(
        functools.partial(_expert_kernel, bm=bm, n_slabs=n_slabs),
        grid_spec=grid_spec,
        out_shape=jax.ShapeDtypeStruct(xs.shape, xs.dtype),
        compiler_params=pltpu.CompilerParams(dimension_semantics=("arbitrary",), vmem_limit_bytes=VMEM_LIMIT),
        name="experts",
    )(block_expert, n_used, xs, w_gate, w_up, w_down)


def _combine_kernel(dest_ref, y_hbm, h1_ref, route_ref, g2_ref, b2_ref, out_ref, ybuf, sem, *, tm, n_slabs, alpha):
    i = pl.program_id(0)
    nb = pl.num_programs(0)
    part = tm * n_slabs

    def issue(blk, slot):
        def body(r, carry):
            for k in range(TOP_K):
                src_row = pl.multiple_of(dest_ref[k * (nb * tm) + blk * tm + r], n_slabs)
                dst_row = pl.multiple_of((slot * TOP_K + k) * part + r * n_slabs, n_slabs)
                pltpu.make_async_copy(y_hbm.at[pl.ds(src_row, n_slabs), :], ybuf.at[pl.ds(dst_row, n_slabs), :],
                                      sem.at[slot]).start(priority=k)
            return carry
        lax.fori_loop(0, tm, body, 0, unroll=DMA_UNROLL)

    @pl.when(i == 0)
    def _():
        issue(0, 0)

    @pl.when(i + 1 < nb)
    def _():
        issue(i + 1, (i + 1) % 2)

    slot = i % 2
    base = pl.multiple_of(slot * (TOP_K * part), TOP_K * part)
    pltpu.make_async_copy(y_hbm.at[pl.ds(0, TOP_K * part), :], ybuf.at[pl.ds(base, TOP_K * part), :],
                          sem.at[slot]).wait()

    route = route_ref[...]
    ffn = (_unpack_bf16_pairs(_load_slabs(ybuf, tm, n_slabs, base)) * route[:, 2:3]
           + _unpack_bf16_pairs(_load_slabs(ybuf, tm, n_slabs, base + part)) * route[:, 3:4])
    out_ref[...] = _layer_norm(alpha * h1_ref[...] + ffn, g2_ref[...], b2_ref[...])


def _combine(dest_rows, y_sorted, h1, route, ln2_g, ln2_b, tm, alpha):
    n, d = h1.shape
    n_slabs = d // (2 * LANES)
    assert n % tm == 0
    grid_spec = pltpu.PrefetchScalarGridSpec(
        num_scalar_prefetch=1,
        grid=(n // tm,),
        in_specs=[pl.BlockSpec(memory_space=pl.ANY),
                  pl.BlockSpec((tm, d), lambda i, dest: (i, 0)),
                  pl.BlockSpec((tm, LANES), lambda i, dest: (i, 0)),
                  pl.BlockSpec((1, d), lambda i, dest: (0, 0)),
                  pl.BlockSpec((1, d), lambda i, dest: (0, 0))],
        out_specs=pl.BlockSpec((tm, d), lambda i, dest: (i, 0)),
        scratch_shapes=[pltpu.VMEM((2 * TOP_K * tm * n_slabs, LANES), y_sorted.dtype),
                        pltpu.SemaphoreType.DMA((2,))],
    )
    return pl.pallas_call(
        functools.partial(_combine_kernel, tm=tm, n_slabs=n_slabs, alpha=alpha),
        grid_spec=grid_spec,
        out_shape=jax.ShapeDtypeStruct((n, d), F32),
        compiler_params=pltpu.CompilerParams(dimension_semantics=("arbitrary",), vmem_limit_bytes=VMEM_LIMIT,
                                             disable_bounds_checks=True),
        name="combine",
    )(dest_rows, y_sorted, h1, route, ln2_g, ln2_b)


def _dispatch_plan(route_t, counts, bm, n_slabs):
    n = route_t.shape[1]
    expert_id = route_t[0:TOP_K].astype(jnp.int32)
    rank = route_t[4:4 + TOP_K].astype(jnp.int32)
    padded = (counts + bm - 1) // bm * bm
    pad_end = jnp.cumsum(padded)
    pad_start = pad_end - padded
    experts = jnp.arange(N_EXPERTS, dtype=jnp.int32)[:, None, None]
    dest = jnp.sum(jnp.where(expert_id[None] == experts, pad_start[:, None, None], 0), axis=0) + rank
    n_blocks = (n * TOP_K + bm - 1) // bm + N_EXPERTS
    block_start = jnp.arange(n_blocks, dtype=jnp.int32) * bm
    block_expert = jnp.minimum(jnp.sum((block_start[:, None] >= pad_end[None, :]).astype(jnp.int32), axis=1),
                               N_EXPERTS - 1).astype(jnp.int32)
    n_used = (pad_end[-1:] // bm).astype(jnp.int32)
    dest_rows = (dest * n_slabs).reshape(-1).astype(jnp.int32)
    pad_lo = jnp.concatenate([pad_start + counts, pad_end[-1:]]).astype(jnp.int32)
    pad_hi = jnp.concatenate([pad_end, jnp.full((1,), n_blocks * bm, pad_end.dtype)]).astype(jnp.int32)
    return dest_rows, pad_lo, pad_hi, block_expert, n_used, n_blocks


def _pad_lanes(w, width=LANES):
    return jnp.pad(w, [(0, 0)] * (w.ndim - 1) + [(0, width - w.shape[-1])])


def kernel(x, meta_tokens, ln_emb_g, ln_emb_b, w_in, conv_qkv_w, a_log, dt_bias, dn_norm_g, conv_dw_w, conv_dw_b, cv_norm_g, cv_norm_b, w_out, ln1_g, ln1_b, w_group, b_group, w_router, b_router, w_exp_gate, w_exp_up, w_exp_down, ln2_g, ln2_b):
    depth = w_in.shape[0]
    assert depth == 1, "single-layer block"
    bsz, seq, d = x.shape
    alpha = (2.0 * depth) ** 0.25
    qkv_w = 3 * DN_WIDTH
    w_in0 = w_in[0]
    glu_off = 4 * DN_WIDTH + 2 * DN_HEADS
    row = lambda a: a.reshape(1, -1).astype(F32)
    p = {
        'ln_emb_g': row(ln_emb_g), 'ln_emb_b': row(ln_emb_b),
        'w_qkv': w_in0[:, :qkv_w].astype(BF16),
        'w_z': w_in0[:, qkv_w:4 * DN_WIDTH].astype(BF16),
        'w_ba': _pad_lanes(w_in0[:, 4 * DN_WIDTH:glu_off]).astype(BF16),
        'w_glu': w_in0[:, glu_off:].astype(BF16),
        'conv_w': conv_qkv_w[0].astype(F32),
        'neg_a': _pad_lanes(jnp.concatenate([jnp.zeros((DN_HEADS,), F32), -jnp.exp(a_log[0].astype(F32))])[None]),
        'dt_b': _pad_lanes(jnp.concatenate([jnp.zeros((DN_HEADS,), F32), dt_bias[0].astype(F32)])[None]),
        'dw_w': conv_dw_w[0].astype(F32), 'dw_b': row(conv_dw_b[0]),
        'cv_g': row(cv_norm_g[0]), 'cv_b': row(cv_norm_b[0]),
        'w_out': w_out[0].astype(BF16), 'ln1_g': row(ln1_g[0]), 'ln1_b': row(ln1_b[0]),
    }
    w_r = _pad_lanes(jnp.concatenate([w_group[0], w_router[0]], axis=1).astype(F32))
    p['w_r_hi'] = w_r.astype(BF16)
    p['w_r_lo'] = (w_r - p['w_r_hi'].astype(F32)).astype(BF16)
    p['b_r'] = _pad_lanes(jnp.concatenate([b_group[0], b_router[0]])[None].astype(F32))
    gain = row(dn_norm_g[0])

    conf_w = p['dw_w'].shape[1]
    zero_hq = jnp.zeros((QKV_HALO, qkv_w), F32)
    zero_hc = jnp.zeros((CONF_HALO, conf_w), F32)
    m_mixed, mbg, halo_q, halo_c, _ = _mix_in(meta_tokens[None].astype(F32), p, zero_hq, zero_hc, N_META)
    front = lambda a: jnp.pad(a, [(0, 0), (CHUNK - N_META, 0), (0, 0)])
    s_zero = jnp.zeros((DN_HEADS, HEAD_DIM, HEAD_DIM), F32)
    _, s_meta = _delta(front(m_mixed), front(mbg), s_zero, gain, 1)

    mixed, bg, _, _, h = _mix_in(x, p, halo_q[0], halo_c[0], TM_IN)
    o, _ = _delta(mixed, bg, s_meta[0], gain, DELTA_CHUNKS)

    n = bsz * seq
    h1, h1p, route, route_t, cnt = _mix_out(h.reshape(n, d), o.reshape(n, DN_WIDTH),
                                            mixed.reshape(n, mixed.shape[2]), p, halo_c[0], seq, TM_OUT, alpha)

    n_slabs = d // (2 * LANES)
    counts = cnt[0, :N_EXPERTS].astype(jnp.int32)
    dest_rows, pad_lo, pad_hi, block_expert, n_used, n_blocks = _dispatch_plan(route_t, counts, BM_EXPERT, n_slabs)
    xs = _dispatch(dest_rows, pad_lo, pad_hi, h1p, n_blocks * BM_EXPERT, TD_DISPATCH, n_slabs, BM_EXPERT)
    first_layer = lambda w: w.reshape(w.shape[1:])
    y_sorted = _experts(block_expert, n_used, xs, first_layer(w_exp_gate), first_layer(w_exp_up),
                        first_layer(w_exp_down), BM_EXPERT)
    out = _combine(dest_rows, y_sorted, h1, route, row(ln2_g[0]), row(ln2_b[0]), TM_COMBINE, alpha)
    return out.reshape(bsz, seq, d)
```

```python
import functools

import jax
import jax.numpy as jnp
from jax import lax
from jax.experimental import pallas as pl
from jax.experimental.pallas import tpu as pltpu

F32 = jnp.float32
BF16 = jnp.bfloat16

NORM_EPS = 1e-5
N_META = 16
DN_HEADS = 4
HEAD_DIM = 128
DN_WIDTH = DN_HEADS * HEAD_DIM
CHUNK = 64
SHORT_CONV = 4
CONF_KERNEL = 31
N_GROUPS = 4
EXPERTS_PER_GROUP = 8
N_EXPERTS = N_GROUPS * EXPERTS_PER_GROUP
TOP_K = 2
LANES = 128
SUBLANES = 8
QKV_HALO = 8
CONF_HALO = 32
VMEM_LIMIT = 56 * 1024 * 1024

TM_IN = 1024
DELTA_CHUNKS = 8
TM_OUT = 1024
BM_EXPERT = 512
TM_COMBINE = 512
TD_DISPATCH = 512
RING = 3
SUB_ROWS = 256
SUB_ROWS_OUT = 512
CONV_BLOCK_ROWS = 64
CONV_BLOCK_COLS = 512
MIXED_COLS = 5
COL_Q, COL_K, COL_V, COL_Z, COL_C = range(MIXED_COLS)
DMA_UNROLL = 8


def _dot(a, b):
    return jnp.dot(a, b, preferred_element_type=F32)


def _split2(x):
    hi = x.astype(BF16)
    lo = (x - hi.astype(F32)).astype(BF16)
    return hi, lo


def _sigmoid(x):
    return 1.0 / (1.0 + jnp.exp(-x))


def _silu(x):
    return x * _sigmoid(x)


def _layer_norm(x, g, b):
    mu = jnp.mean(x, axis=-1, keepdims=True)
    xc = x - mu
    var = jnp.mean(xc * xc, axis=-1, keepdims=True)
    return xc * lax.rsqrt(var + NORM_EPS) * g + b


def _pack_bf16_pairs(x):
    half = x.shape[1] // 2
    lo = lax.bitcast_convert_type(x[:, :half].astype(BF16).astype(F32), jnp.uint32)
    hi = lax.bitcast_convert_type(x[:, half:].astype(BF16).astype(F32), jnp.uint32)
    return (lo >> 16) | (hi & jnp.uint32(0xFFFF0000))


def _unpack_bf16_pairs(p):
    lo = lax.bitcast_convert_type(p << 16, F32)
    hi = lax.bitcast_convert_type(p & jnp.uint32(0xFFFF0000), F32)
    return jnp.concatenate([lo, hi], axis=1)


def _store_slabs(ref, val):
    rows, d = val.shape
    n_slabs = d // LANES
    for s in range(n_slabs):
        ref[pl.ds(s, rows, stride=n_slabs), :] = val[:, s * LANES:(s + 1) * LANES]


def _load_slabs(ref, rows, n_slabs, base=0):
    return jnp.concatenate([ref[pl.ds(base + s, rows, stride=n_slabs), :] for s in range(n_slabs)], axis=1)


def _full_spec(shape):
    nd = len(shape)
    return pl.BlockSpec(shape, lambda *_: (0,) * nd)


def _causal_depthwise(ext_ref, w_ref, n_taps, halo, tm):
    ext = ext_ref[...]
    rows, cols = ext.shape
    first = halo - (n_taps - 1)
    groups = ext.reshape(rows // SUBLANES, SUBLANES, cols)
    sub = lax.broadcasted_iota(jnp.int32, (1, SUBLANES, 1), 1)

    def shifted(phase):
        rolled = pltpu.roll(groups, SUBLANES - phase, axis=1)
        nxt = jnp.concatenate([rolled[1:], rolled[:1]], axis=0)
        return jnp.where(sub < SUBLANES - phase, rolled, nxt).reshape(rows, cols)

    phases = {}
    for k in range(n_taps):
        phase = (first + k) % SUBLANES
        if phase not in phases:
            phases[phase] = ext if phase == 0 else shifted(phase)

    rb = min(tm, CONV_BLOCK_ROWS)
    cb = min(cols, CONV_BLOCK_COLS)
    out_rows = []
    for r0 in range(0, tm, rb):
        out_cols = []
        for c0 in range(0, cols, cb):
            acc = None
            for k in range(n_taps):
                phase = (first + k) % SUBLANES
                base = first + k - phase + r0
                term = phases[phase][base:base + rb, c0:c0 + cb] * w_ref[k:k + 1, c0:c0 + cb]
                acc = term if acc is None else acc + term
            out_cols.append(acc)
        out_rows.append(out_cols[0] if len(out_cols) == 1 else jnp.concatenate(out_cols, axis=1))
    return out_rows[0] if len(out_rows) == 1 else jnp.concatenate(out_rows, axis=0)


def _row_views(refs, r0, rows, lead=0):
    return [ref.at[pl.ds(r0, lead + rows), :] for ref in refs]


def _mix_in_kernel(x_ref, lng_ref, lnb_ref, wqkv_ref, wz_ref, wglu_ref, wba_ref, cw_ref, nega_ref,
                   dtb_ref, hq_in_ref, hc_in_ref,
                   m_ref, bg_ref, hq_out_ref, hc_out_ref, h_ref,
                   qkv_ext, c_ext):
    tm = x_ref.shape[0]

    @pl.when(pl.program_id(1) == 0)
    def _():
        qkv_ext[0:QKV_HALO, :] = hq_in_ref[...]
        c_ext[0:CONF_HALO, :] = hc_in_ref[...]

    th = min(tm, SUB_ROWS)
    for r0 in range(0, tm, th):
        x_v, bg_v, h_v = _row_views((x_ref, bg_ref, h_ref), r0, th)
        q_v, k_v, v_v, z_v, c_v = (m_ref.at[pl.ds(r0, th), pl.ds(col * DN_WIDTH, DN_WIDTH)]
                                   for col in range(MIXED_COLS))
        qkv_v, = _row_views((qkv_ext,), r0, th, QKV_HALO)
        cext_v, = _row_views((c_ext,), r0, th, CONF_HALO)
        _mix_in_rows(x_v, lng_ref, lnb_ref, wqkv_ref, wz_ref, wglu_ref, wba_ref, cw_ref, nega_ref, dtb_ref,
                     q_v, k_v, v_v, z_v, c_v, bg_v, h_v, qkv_v, cext_v)

    q_tail = qkv_ext[tm:tm + QKV_HALO, :]
    c_tail = c_ext[tm:tm + CONF_HALO, :]
    qkv_ext[0:QKV_HALO, :] = q_tail
    c_ext[0:CONF_HALO, :] = c_tail
    hq_out_ref[...] = q_tail
    hc_out_ref[...] = c_tail


def _mix_in_rows(x_ref, lng_ref, lnb_ref, wqkv_ref, wz_ref, wglu_ref, wba_ref, cw_ref, nega_ref, dtb_ref,
                 q_ref, k_ref, v_ref, z_ref, c_ref, bg_ref, h_ref, qkv_ext, c_ext):
    tm = x_ref.shape[0]
    h = _layer_norm(x_ref[...], lng_ref[...], lnb_ref[...])
    h_ref[...] = h
    hb = h.astype(BF16)

    qkv_ext[QKV_HALO:QKV_HALO + tm, :] = _dot(hb, wqkv_ref[...])
    qkv = _silu(_causal_depthwise(qkv_ext, cw_ref, SHORT_CONV, QKV_HALO, tm))
    for hd in range(DN_HEADS):
        lo = hd * HEAD_DIM
        qh = qkv[:, lo:lo + HEAD_DIM]
        kh = qkv[:, DN_WIDTH + lo:DN_WIDTH + lo + HEAD_DIM]
        q_ref[:, lo:lo + HEAD_DIM] = (qh * (lax.rsqrt(jnp.sum(qh * qh, axis=-1, keepdims=True) + 1e-6)
                                            * (HEAD_DIM ** -0.5))).astype(q_ref.dtype)
        k_ref[:, lo:lo + HEAD_DIM] = (kh * lax.rsqrt(jnp.sum(kh * kh, axis=-1, keepdims=True) + 1e-6)
                                      ).astype(k_ref.dtype)
    v_ref[...] = qkv[:, 2 * DN_WIDTH:].astype(v_ref.dtype)
    z_ref[...] = _dot(hb, wz_ref[...]).astype(z_ref.dtype)

    ba = _dot(hb, wba_ref[...])
    lane = lax.broadcasted_iota(jnp.int32, ba.shape, 1)
    sp_in = ba + dtb_ref[...]
    softplus = jnp.maximum(sp_in, 0.0) + jnp.log(1.0 + jnp.exp(-jnp.abs(sp_in)))
    bg_ref[...] = jnp.where(lane < DN_HEADS, _sigmoid(ba), nega_ref[...] * softplus)

    glu = _dot(hb, wglu_ref[...])
    cw = glu.shape[1] // 2
    c_pre = glu[:, :cw] * _sigmoid(glu[:, cw:])
    c_ref[...] = c_pre.astype(c_ref.dtype)
    c_ext[CONF_HALO:CONF_HALO + tm, :] = c_pre


def _mix_in(x, p, halo_q, halo_c, tm):
    bsz, seq, d = x.shape
    assert seq % tm == 0
    qkv_w = 3 * DN_WIDTH
    conf_w = p['dw_w'].shape[1]

    def row(width):
        return pl.BlockSpec((None, tm, width), lambda b, t: (b, t, 0))

    def per_batch(rows, width):
        return pl.BlockSpec((None, rows, width), lambda b, t: (b, 0, 0))

    consts = [p['ln_emb_g'], p['ln_emb_b'], p['w_qkv'], p['w_z'], p['w_glu'], p['w_ba'], p['conv_w'],
              p['neg_a'], p['dt_b'], halo_q, halo_c]
    sds = jax.ShapeDtypeStruct
    assert conf_w == DN_WIDTH, "q, k, v, z and the conformer channels share one array of equal-width column blocks"
    out_shape = [sds((bsz, seq, MIXED_COLS * DN_WIDTH), BF16),
                 sds((bsz, seq, LANES), F32), sds((bsz, QKV_HALO, qkv_w), F32),
                 sds((bsz, CONF_HALO, conf_w), F32), sds((bsz, seq, d), F32)]
    out_specs = [row(MIXED_COLS * DN_WIDTH), row(LANES), per_batch(QKV_HALO, qkv_w),
                 per_batch(CONF_HALO, conf_w), row(d)]
    return pl.pallas_call(
        _mix_in_kernel,
        grid=(bsz, seq // tm),
        in_specs=[row(d)] + [_full_spec(c.shape) for c in consts],
        out_specs=out_specs,
        out_shape=out_shape,
        scratch_shapes=[pltpu.VMEM((QKV_HALO + tm, qkv_w), F32), pltpu.VMEM((CONF_HALO + tm, conf_w), F32)],
        compiler_params=pltpu.CompilerParams(dimension_semantics=("parallel", "arbitrary"),
                                             vmem_limit_bytes=VMEM_LIMIT),
        name="mix_in",
    )(x, *consts)


def _bmm(a, b):
    return jnp.einsum('nij,njk->nik', a, b, preferred_element_type=F32)


def _delta_kernel(m_ref, z_ref, bg_ref, s0_ref, gain_ref, o_ref, sfin_ref,
                  s_ref, u_s, wq_s, attn_s, kd_s, egl_s, *, chunks, blocks_per_seq):
    j = pl.program_id(0)
    slot = j % 2
    prev = 1 - slot
    gain = gain_ref[...]

    @pl.when(j == 0)
    def _():
        u_s[1] = jnp.zeros(u_s.shape[1:], u_s.dtype)
        wq_s[1] = jnp.zeros(wq_s.shape[1:], wq_s.dtype)
        attn_s[1] = jnp.zeros(attn_s.shape[1:], attn_s.dtype)
        kd_s[1] = jnp.zeros(kd_s.shape[1:], kd_s.dtype)
        egl_s[1] = jnp.zeros(egl_s.shape[1:], egl_s.dtype)

    def recurrence():
        live = j > 0
        restart = (jnp.maximum(j - 1, 0) % blocks_per_seq) == 0
        start = [jnp.where(restart, s0_ref[hd], s_ref[hd]) for hd in range(DN_HEADS)]
        state = list(start)
        for c in range(chunks):
            idx = [hd * chunks + c for hd in range(DN_HEADS)]
            wq = [_dot(wq_s[prev, n], state[hd].astype(BF16)) for hd, n in enumerate(idx)]
            yield
            v_new = [(u_s[prev, n] - wq[hd][:CHUNK]).astype(BF16) for hd, n in enumerate(idx)]
            o = [wq[hd][CHUNK:] + _dot(attn_s[prev, n], v_new[hd]) for hd, n in enumerate(idx)]
            state = [state[hd] * egl_s[prev, n][0:1, :]
                     + lax.dot_general(kd_s[prev, n], v_new[hd], (((0,), (0,)), ((), ())),
                                       preferred_element_type=F32)
                     for hd, n in enumerate(idx)]
            yield
            for hd in range(DN_HEADS):
                cols = slice(hd * HEAD_DIM, (hd + 1) * HEAD_DIM)
                rows = slice(c * CHUNK, (c + 1) * CHUNK)
                r = o[hd] * lax.rsqrt(jnp.mean(o[hd] * o[hd], axis=-1, keepdims=True) + 1e-6)
                o_ref[rows, cols] = (r * gain * _silu(z_ref[rows, cols].astype(F32))).astype(o_ref.dtype)
        for hd in range(DN_HEADS):
            kept = jnp.where(live, state[hd], start[hd])
            s_ref[hd] = kept
            sfin_ref[hd] = kept

    def preparation():
        yield from _delta_prepare(m_ref, bg_ref, u_s, wq_s, attn_s, kd_s, egl_s, slot, chunks)

    halves = [recurrence(), preparation()]
    while halves:
        for gen in list(halves):
            try:
                next(gen)
            except StopIteration:
                halves.remove(gen)


def _delta_prepare(m_ref, bg_ref, u_s, wq_s, attn_s, kd_s, egl_s, slot, chunks):
    ii = lax.broadcasted_iota(jnp.int32, (CHUNK, CHUNK), 0)
    jj = lax.broadcasted_iota(jnp.int32, (CHUNK, CHUNK), 1)
    causal = ii >= jj
    strict = ii > jj
    eye = (ii == jj).astype(F32)
    bg3 = bg_ref[...].reshape(chunks, CHUNK, LANES)
    tril_b = jnp.broadcast_to(causal.astype(BF16), (chunks, CHUNK, CHUNK))
    p1 = bg3.astype(BF16)
    r1 = bg3 - p1.astype(F32)
    p2 = r1.astype(BF16)
    p3 = (r1 - p2.astype(F32)).astype(BF16)
    gc3 = _bmm(tril_b, p1) + _bmm(tril_b, p2) + _bmm(tril_b, p3)
    yield

    def heads(col):
        lo = col * DN_WIDTH
        return jnp.concatenate([m_ref[:, lo + hd * HEAD_DIM:lo + (hd + 1) * HEAD_DIM].astype(F32)
                                .reshape(chunks, CHUNK, HEAD_DIM) for hd in range(DN_HEADS)], axis=0)
    q = heads(COL_Q)
    k = heads(COL_K)
    v = heads(COL_V)
    bet = jnp.concatenate([bg3[:, :, hd:hd + 1] for hd in range(DN_HEADS)], axis=0)
    gc = jnp.concatenate([gc3[:, :, DN_HEADS + hd:DN_HEADS + hd + 1] for hd in range(DN_HEADS)], axis=0)
    gc_t = [gc3[c].T for c in range(chunks)]
    decay = jnp.stack([
        jnp.exp(jnp.where(causal, gc3[c][:, DN_HEADS + hd:DN_HEADS + hd + 1]
                          - gc_t[c][DN_HEADS + hd:DN_HEADS + hd + 1, :], -jnp.inf))
        for hd in range(DN_HEADS) for c in range(chunks)], axis=0)

    kb = k * bet
    g_all = jnp.einsum('nid,njd->nij', jnp.concatenate([kb, q], axis=1).astype(BF16), k.astype(BF16),
                       preferred_element_type=F32)
    yield
    a_low = jnp.where(strict, g_all[:, :CHUNK] * decay, 0.0)
    attn = (g_all[:, CHUNK:] * decay).astype(BF16)

    l_mat = eye + a_low
    l_bf = l_mat.astype(BF16)
    t_mat = eye - a_low
    for _ in range(4):
        res = eye - _bmm(l_bf, t_mat.astype(BF16))
        yield
        t_mat = t_mat + _bmm(t_mat.astype(BF16), res.astype(BF16))
        yield
    l_lo = (l_mat - l_bf.astype(F32)).astype(BF16)
    t_hi = t_mat.astype(BF16)
    t_lo = (t_mat - t_hi.astype(F32)).astype(BF16)
    res = eye - (_bmm(l_bf, t_hi) + _bmm(l_lo, t_hi) + _bmm(l_bf, t_lo))
    yield
    t_mat = t_mat + _bmm(t_hi, res.astype(BF16))
    yield

    eg = jnp.exp(gc)
    uw = _bmm(t_mat.astype(BF16), jnp.concatenate([v * bet, kb * eg], axis=2).astype(BF16))
    yield
    u = uw[:, :, :HEAD_DIM]
    wq_lhs = jnp.concatenate([uw[:, :, HEAD_DIM:], q * eg], axis=1).astype(BF16)
    g_last = gc[:, CHUNK - 1:CHUNK, :]
    k_dec = (k * jnp.exp(g_last - gc)).astype(BF16)
    eg_last = jnp.exp(g_last)

    u_s[slot] = u
    wq_s[slot] = wq_lhs
    attn_s[slot] = attn
    kd_s[slot] = k_dec
    egl_s[slot] = jnp.broadcast_to(eg_last, egl_s.shape[1:])


def _delta(mixed, bg, s0, gain, chunks):
    bsz, seq, _ = mixed.shape
    rows = chunks * CHUNK
    assert seq % rows == 0

    nj = seq // rows
    nb = DN_HEADS * chunks
    total = bsz * nj

    def prep(width):
        def index(j):
            blk = jnp.minimum(j, total - 1)
            return (blk // nj, blk % nj, 0)
        return pl.BlockSpec((None, rows, width), index)

    def scan(width, col=0):
        def index(j):
            blk = jnp.maximum(j - 1, 0)
            return (blk // nj, blk % nj, col)
        return pl.BlockSpec((None, rows, width), index)

    state_shape = (DN_HEADS, HEAD_DIM, HEAD_DIM)
    return pl.pallas_call(
        functools.partial(_delta_kernel, chunks=chunks, blocks_per_seq=nj),
        grid=(total + 1,),
        in_specs=[prep(mixed.shape[2]), scan(DN_WIDTH, COL_Z), prep(LANES), _full_spec(state_shape),
                  _full_spec(gain.shape)],
        out_specs=[scan(DN_WIDTH),
                   pl.BlockSpec((None,) + state_shape, lambda j: (jnp.maximum(j - 1, 0) // nj, 0, 0, 0))],
        out_shape=[jax.ShapeDtypeStruct((bsz, seq, DN_WIDTH), BF16),
                   jax.ShapeDtypeStruct((bsz,) + state_shape, F32)],
        scratch_shapes=[pltpu.VMEM(state_shape, F32),
                        pltpu.VMEM((2, nb, CHUNK, HEAD_DIM), F32),
                        pltpu.VMEM((2, nb, 2 * CHUNK, HEAD_DIM), BF16),
                        pltpu.VMEM((2, nb, CHUNK, CHUNK), BF16),
                        pltpu.VMEM((2, nb, CHUNK, HEAD_DIM), BF16),
                        pltpu.VMEM((2, nb, SUBLANES, HEAD_DIM), F32)],
        compiler_params=pltpu.CompilerParams(dimension_semantics=("arbitrary",), vmem_limit_bytes=VMEM_LIMIT),
        name="delta",
    )(mixed, mixed, bg, s0, gain)


def _mix_out_kernel(h_ref, o_ref, c_ref, wo_ref, g1_ref, b1_ref, wrh_ref, wrl_ref, br_ref,
                    dww_ref, dwb_ref, cvg_ref, cvb_ref, hc_in_ref, tri_ref,
                    h1_ref, route_ref, route_t_ref, cnt_out_ref, cnt_ref, c_ext, *, alpha, tiles_per_seq):
    tm = h_ref.shape[0]

    @pl.when(pl.program_id(0) == 0)
    def _():
        cnt_ref[...] = jnp.zeros_like(cnt_ref)

    @pl.when(pl.program_id(0) % tiles_per_seq == 0)
    def _():
        c_ext[0:CONF_HALO, :] = hc_in_ref[...]

    th = min(tm, SUB_ROWS_OUT)
    for r0 in range(0, tm, th):
        h_v, o_v, c_v, h1_v, route_v = _row_views((h_ref, o_ref, c_ref, h1_ref, route_ref), r0, th)
        cext_v, = _row_views((c_ext,), r0, th, CONF_HALO)
        _mix_out_rows(h_v, o_v, c_v, wo_ref, g1_ref, b1_ref, wrh_ref, wrl_ref, br_ref, dww_ref, dwb_ref, cvg_ref,
                      cvb_ref, h1_v, route_v, route_t_ref.at[:, pl.ds(r0, th)], cnt_ref, cext_v, tri_ref, alpha)
    c_ext[0:CONF_HALO, :] = c_ext[tm:tm + CONF_HALO, :]
    cnt_out_ref[...] = jnp.broadcast_to(cnt_ref[...], cnt_out_ref.shape)


def _mix_out_rows(h_ref, o_ref, c_ref, wo_ref, g1_ref, b1_ref, wrh_ref, wrl_ref, br_ref, dww_ref, dwb_ref, cvg_ref,
                  cvb_ref, h1_ref, route_ref, route_t_ref, cnt_ref, c_ext, tri_ref, alpha):
    tm = h_ref.shape[0]

    c_ext[CONF_HALO:CONF_HALO + tm, :] = c_ref[...].astype(F32)
    conv = _causal_depthwise(c_ext, dww_ref, CONF_KERNEL, CONF_HALO, tm) + dwb_ref[...]
    conf = _silu(_layer_norm(conv, cvg_ref[...], cvb_ref[...]))

    h = h_ref[...]
    dn = o_ref.shape[1]
    mix = _dot(o_ref[...].astype(BF16), wo_ref[0:dn, :]) + _dot(conf.astype(BF16), wo_ref[dn:, :])
    h1 = _layer_norm(alpha * h + mix, g1_ref[...], b1_ref[...])
    h1_ref[...] = h1

    hh, hl = _split2(h1)
    logits = _dot(hh, wrh_ref[...]) + _dot(hl, wrh_ref[...]) + _dot(hh, wrl_ref[...]) + br_ref[...]
    lane = lax.broadcasted_iota(jnp.int32, logits.shape, 1).astype(F32)
    big = float(LANES)
    neg = -jnp.inf

    def first_argmax(vals):
        top = jnp.max(vals, axis=-1, keepdims=True)
        return top, jnp.min(jnp.where(vals == top, lane, big), axis=-1, keepdims=True)

    grp = jnp.where(lane < N_GROUPS, logits, neg)
    g_top, g_sel = first_argmax(grp)
    p_group = 1.0 / jnp.sum(jnp.exp(grp - g_top), axis=-1, keepdims=True)
    lo = N_GROUPS + EXPERTS_PER_GROUP * g_sel
    in_grp = jnp.where((lane >= lo) & (lane < lo + EXPERTS_PER_GROUP), logits, neg)
    m1, i1 = first_argmax(in_grp)
    m2, i2 = first_argmax(jnp.where(lane == i1, neg, in_grp))
    s = jnp.exp(m2 - m1)
    w1 = p_group / (1.0 + s)
    w2 = p_group * s / (1.0 + s)
    e1 = i1 - N_GROUPS
    e2 = i2 - N_GROUPS

    tm = logits.shape[0]
    oh1 = (lane == e1).astype(F32)
    oh2 = (lane == e2).astype(F32)
    both = oh1 + oh2
    base = _dot(tri_ref[...], both.astype(BF16)) + cnt_ref[...]
    r1 = jnp.sum(oh1 * base, axis=-1, keepdims=True)
    r2 = jnp.sum(oh2 * base, axis=-1, keepdims=True)
    cnt_ref[...] = cnt_ref[...] + jnp.sum(both, axis=0, keepdims=True)

    vals = (e1, e2, w1, w2, r1, r2)
    route = jnp.zeros_like(logits)
    for idx, val in enumerate(vals):
        route = jnp.where(lane == idx, val, route)
    route_ref[...] = route
    route_t_ref[...] = route.T[0:SUBLANES, :]


def _mix_out(h2d, o2d, mixed2d, p, halo_c, seq, tm, alpha):
    n, d = h2d.shape
    assert seq % tm == 0
    conf_w = p['dw_w'].shape[1]
    th = min(tm, SUB_ROWS_OUT)
    tri = (jnp.arange(th)[:, None] > jnp.arange(th)[None, :]).astype(BF16)

    def row(width):
        return pl.BlockSpec((tm, width), lambda i: (i, 0))

    consts = [p['w_out'], p['ln1_g'], p['ln1_b'], p['w_r_hi'], p['w_r_lo'], p['b_r'],
              p['dw_w'], p['dw_b'], p['cv_g'], p['cv_b'], halo_c, tri]
    return pl.pallas_call(
        functools.partial(_mix_out_kernel, alpha=alpha, tiles_per_seq=seq // tm),
        grid=(n // tm,),
        in_specs=[row(d), row(o2d.shape[1]), pl.BlockSpec((tm, conf_w), lambda i: (i, COL_C))]
                 + [_full_spec(c.shape) for c in consts],
        out_specs=[row(d), row(LANES),
                   pl.BlockSpec((SUBLANES, tm), lambda i: (0, i)), _full_spec((SUBLANES, LANES))],
        out_shape=[jax.ShapeDtypeStruct((n, d), F32), jax.ShapeDtypeStruct((n, LANES), F32),
                   jax.ShapeDtypeStruct((SUBLANES, n), F32), jax.ShapeDtypeStruct((SUBLANES, LANES), F32)],
        scratch_shapes=[pltpu.VMEM((1, LANES), F32), pltpu.VMEM((CONF_HALO + tm, conf_w), F32)],
        compiler_params=pltpu.CompilerParams(dimension_semantics=("arbitrary",), vmem_limit_bytes=VMEM_LIMIT),
        name="mix_out",
    )(h2d, o2d, mixed2d, *consts)


def _dispatch_kernel(dest_ref, pad_lo_ref, pad_hi_ref, h1_hbm, xs_hbm, raw, ring, zslab, fsem, ssem, zsem, *,
                     td, n_slabs, n_tokens):
    i = pl.program_id(0)
    nb = pl.num_programs(0)
    slot = i % RING
    tile_rows = td * n_slabs

    def fetch(step):
        start = pl.multiple_of(step * td, td)
        return pltpu.make_async_copy(h1_hbm.at[pl.ds(start, td), :], raw.at[step % RING], fsem.at[step % RING])

    def wait_scatter(step):
        for _ in range(TOP_K):
            pltpu.make_async_copy(ring.at[step % RING], xs_hbm.at[pl.ds(0, tile_rows), :],
                                  ssem.at[step % RING]).wait()

    zrows = zslab.shape[0] // n_slabs

    def pad_copy(row, size):
        return pltpu.make_async_copy(
            zslab.at[pl.ds(0, size * n_slabs), :],
            xs_hbm.at[pl.ds(pl.multiple_of(row * n_slabs, n_slabs), size * n_slabs), :], zsem.at[0])

    def for_each_pad_copy(fn):
        def per_expert(e, carry):
            lo = pad_lo_ref[e]
            length = pad_hi_ref[e] - lo
            for b in range(zrows.bit_length()):
                size = 1 << b

                @pl.when(((length >> b) & 1) == 1)
                def _():
                    fn(pad_copy(lo + (length & (size - 1)), size))
            return carry
        lax.fori_loop(0, N_EXPERTS, per_expert, 0)
        tail_lo = pad_lo_ref[N_EXPERTS]

        def per_piece(piece, carry):
            fn(pad_copy(tail_lo + piece * zrows, zrows))
            return carry
        lax.fori_loop(0, (pad_hi_ref[N_EXPERTS] - tail_lo) // zrows, per_piece, 0)

    @pl.when(i == 0)
    def _():
        fetch(0).start()
        zslab[...] = jnp.zeros_like(zslab)
        for_each_pad_copy(lambda cp: cp.start())

    @pl.when((i == 0) & (nb > 1))
    def _():
        fetch(1).start()

    fetch(i).wait()
    _store_slabs(ring.at[slot], _pack_bf16_pairs(raw[slot]))

    def issue_body(r, carry):
        src = ring.at[slot, pl.ds(pl.multiple_of(r * n_slabs, n_slabs), n_slabs), :]
        for k in range(TOP_K):
            dst_row = pl.multiple_of(dest_ref[k * n_tokens + i * td + r], n_slabs)
            pltpu.make_async_copy(src, xs_hbm.at[pl.ds(dst_row, n_slabs), :], ssem.at[slot]).start(priority=k)
        return carry
    lax.fori_loop(0, td, issue_body, 0, unroll=DMA_UNROLL)

    @pl.when(i > 0)
    def _():
        wait_scatter(i - 1)

    @pl.when(i + 2 < nb)
    def _():
        fetch(i + 2).start()

    @pl.when(i == nb - 1)
    def _():
        wait_scatter(i)
        for_each_pad_copy(lambda cp: cp.wait())


def _dispatch(dest_rows, pad_lo, pad_hi, h1, cap_rows, td, n_slabs, bm):
    n, d = h1.shape
    assert n % td == 0
    grid_spec = pltpu.PrefetchScalarGridSpec(
        num_scalar_prefetch=3,
        grid=(n // td,),
        in_specs=[pl.BlockSpec(memory_space=pl.ANY)],
        out_specs=pl.BlockSpec(memory_space=pl.ANY),
        scratch_shapes=[pltpu.VMEM((RING, td, d), h1.dtype),
                        pltpu.VMEM((RING, td * n_slabs, LANES), jnp.uint32),
                        pltpu.VMEM((bm // 2 * n_slabs, LANES), jnp.uint32),
                        pltpu.SemaphoreType.DMA((RING,)), pltpu.SemaphoreType.DMA((RING,)),
                        pltpu.SemaphoreType.DMA((1,))],
    )
    return pl.pallas_call(
        functools.partial(_dispatch_kernel, td=td, n_slabs=n_slabs, n_tokens=n),
        grid_spec=grid_spec,
        out_shape=jax.ShapeDtypeStruct((cap_rows * n_slabs, LANES), jnp.uint32),
        compiler_params=pltpu.CompilerParams(dimension_semantics=("arbitrary",), disable_bounds_checks=True),
        name="dispatch",
    )(dest_rows, pad_lo, pad_hi, h1)


def _expert_kernel(be_ref, nu_ref, xs_ref, wg_ref, wu_ref, wd_ref, y_ref, wg_bf, wu_bf, wd_bf, *, bm, n_slabs):
    i = pl.program_id(0)
    used = i < nu_ref[0]

    @pl.when(used & ((i == 0) | (be_ref[i] != be_ref[jnp.maximum(i - 1, 0)])))
    def _():
        wg_bf[...] = wg_ref[...].astype(BF16)
        wu_bf[...] = wu_ref[...].astype(BF16)
        wd_bf[...] = wd_ref[...].astype(BF16)

    @pl.when(used)
    def _():
        xb = _unpack_bf16_pairs(_load_slabs(xs_ref, bm, n_slabs)).astype(BF16)
        hid = _silu(_dot(xb, wg_bf[...])) * _dot(xb, wu_bf[...])
        _store_slabs(y_ref, _pack_bf16_pairs(_dot(hid.astype(BF16), wd_bf[...])))

    @pl.when(jnp.logical_not(used))
    def _():
        y_ref[...] = jnp.zeros_like(y_ref)


def _experts(block_expert, n_used, xs, w_gate, w_up, w_down, bm):
    n_blocks = block_expert.shape[0]
    d = w_gate.shape[1]
    ff = w_gate.shape[2]
    n_slabs = d // (2 * LANES)

    def blk(i, be, nu):
        return jnp.minimum(i, nu[0] - 1)

    grid_spec = pltpu.PrefetchScalarGridSpec(
        num_scalar_prefetch=2,
        grid=(n_blocks,),
        in_specs=[pl.BlockSpec((bm * n_slabs, LANES), lambda i, be, nu: (blk(i, be, nu), 0)),
                  pl.BlockSpec((None, d, ff), lambda i, be, nu: (be[blk(i, be, nu)], 0, 0)),
                  pl.BlockSpec((None, d, ff), lambda i, be, nu: (be[blk(i, be, nu)], 0, 0)),
                  pl.BlockSpec((None, ff, d), lambda i, be, nu: (be[blk(i, be, nu)], 0, 0))],
        out_specs=pl.BlockSpec((bm * n_slabs, LANES), lambda i, be, nu: (i, 0)),
        scratch_shapes=[pltpu.VMEM((d, ff), BF16), pltpu.VMEM((d, ff), BF16), pltpu.VMEM((ff, d), BF16)],
    )
    return pl.pallas_call(
        functools.partial(_expert_kernel, bm=bm, n_slabs=n_slabs),
        grid_spec=grid_spec,
        out_shape=jax.ShapeDtypeStruct(xs.shape, xs.dtype),
        compiler_params=pltpu.CompilerParams(dimension_semantics=("arbitrary",), vmem_limit_bytes=VMEM_LIMIT),
        name="experts",
    )(block_expert, n_used, xs, w_gate, w_up, w_down)


def _combine_kernel(dest_ref, y_hbm, h1_ref, route_ref, g2_ref, b2_ref, out_ref, ybuf, sem, *, tm, n_slabs, alpha):
    i = pl.program_id(0)
    nb = pl.num_programs(0)
    part = tm * n_slabs

    def issue(blk, slot):
        def body(r, carry):
            for k in range(TOP_K):
                src_row = pl.multiple_of(dest_ref[k * (nb * tm) + blk * tm + r], n_slabs)
                dst_row = pl.multiple_of((slot * TOP_K + k) * part + r * n_slabs, n_slabs)
                pltpu.make_async_copy(y_hbm.at[pl.ds(src_row, n_slabs), :], ybuf.at[pl.ds(dst_row, n_slabs), :],
                                      sem.at[slot]).start(priority=k)
            return carry
        lax.fori_loop(0, tm, body, 0, unroll=DMA_UNROLL)

    @pl.when(i == 0)
    def _():
        issue(0, 0)

    @pl.when(i + 1 < nb)
    def _():
        issue(i + 1, (i + 1) % 2)

    slot = i % 2
    base = pl.multiple_of(slot * (TOP_K * part), TOP_K * part)
    pltpu.make_async_copy(y_hbm.at[pl.ds(0, TOP_K * part), :], ybuf.at[pl.ds(base, TOP_K * part), :],
                          sem.at[slot]).wait()

    route = route_ref[...]
    ffn = (_unpack_bf16_pairs(_load_slabs(ybuf, tm, n_slabs, base)) * route[:, 2:3]
           + _unpack_bf16_pairs(_load_slabs(ybuf, tm, n_slabs, base + part)) * route[:, 3:4])
    out_ref[...] = _layer_norm(alpha * h1_ref[...] + ffn, g2_ref[...], b2_ref[...])


def _combine(dest_rows, y_sorted, h1, route, ln2_g, ln2_b, tm, alpha):
    n, d = h1.shape
    n_slabs = d // (2 * LANES)
    assert n % tm == 0
    grid_spec = pltpu.PrefetchScalarGridSpec(
        num_scalar_prefetch=1,
        grid=(n // tm,),
        in_specs=[pl.BlockSpec(memory_space=pl.ANY),
                  pl.BlockSpec((tm, d), lambda i, dest: (i, 0)),
                  pl.BlockSpec((tm, LANES), lambda i, dest: (i, 0)),
                  pl.BlockSpec((1, d), lambda i, dest: (0, 0)),
                  pl.BlockSpec((1, d), lambda i, dest: (0, 0))],
        out_specs=pl.BlockSpec((tm, d), lambda i, dest: (i, 0)),
        scratch_shapes=[pltpu.VMEM((2 * TOP_K * tm * n_slabs, LANES), y_sorted.dtype),
                        pltpu.SemaphoreType.DMA((2,))],
    )
    return pl.pallas_call(
        functools.partial(_combine_kernel, tm=tm, n_slabs=n_slabs, alpha=alpha),
        grid_spec=grid_spec,
        out_shape=jax.ShapeDtypeStruct((n, d), F32),
        compiler_params=pltpu.CompilerParams(dimension_semantics=("arbitrary",), vmem_limit_bytes=VMEM_LIMIT,
                                             disable_bounds_checks=True),
        name="combine",
    )(dest_rows, y_sorted, h1, route, ln2_g, ln2_b)


def _dispatch_plan(route_t, counts, bm, n_slabs):
    n = route_t.shape[1]
    expert_id = route_t[0:TOP_K].astype(jnp.int32)
    rank = route_t[4:4 + TOP_K].astype(jnp.int32)
    padded = (counts + bm - 1) // bm * bm
    pad_end = jnp.cumsum(padded)
    pad_start = pad_end - padded
    experts = jnp.arange(N_EXPERTS, dtype=jnp.int32)[:, None, None]
    dest = jnp.sum(jnp.where(expert_id[None] == experts, pad_start[:, None, None], 0), axis=0) + rank
    n_blocks = (n * TOP_K + bm - 1) // bm + N_EXPERTS
    block_start = jnp.arange(n_blocks, dtype=jnp.int32) * bm
    block_expert = jnp.minimum(jnp.sum((block_start[:, None] >= pad_end[None, :]).astype(jnp.int32), axis=1),
                               N_EXPERTS - 1).astype(jnp.int32)
    n_used = (pad_end[-1:] // bm).astype(jnp.int32)
    dest_rows = (dest * n_slabs).reshape(-1).astype(jnp.int32)
    pad_lo = jnp.concatenate([pad_start + counts, pad_end[-1:]]).astype(jnp.int32)
    pad_hi = jnp.concatenate([pad_end, jnp.full((1,), n_blocks * bm, pad_end.dtype)]).astype(jnp.int32)
    return dest_rows, pad_lo, pad_hi, block_expert, n_used, n_blocks


def _pad_lanes(w, width=LANES):
    return jnp.pad(w, [(0, 0)] * (w.ndim - 1) + [(0, width - w.shape[-1])])


def kernel(x, meta_tokens, ln_emb_g, ln_emb_b, w_in, conv_qkv_w, a_log, dt_bias, dn_norm_g, conv_dw_w, conv_dw_b, cv_norm_g, cv_norm_b, w_out, ln1_g, ln1_b, w_group, b_group, w_router, b_router, w_exp_gate, w_exp_up, w_exp_down, ln2_g, ln2_b):
    depth = w_in.shape[0]
    assert depth == 1, "single-layer block"
    bsz, seq, d = x.shape
    alpha = (2.0 * depth) ** 0.25
    qkv_w = 3 * DN_WIDTH
    w_in0 = w_in[0]
    glu_off = 4 * DN_WIDTH + 2 * DN_HEADS
    row = lambda a: a.reshape(1, -1).astype(F32)
    p = {
        'ln_emb_g': row(ln_emb_g), 'ln_emb_b': row(ln_emb_b),
        'w_qkv': w_in0[:, :qkv_w].astype(BF16),
        'w_z': w_in0[:, qkv_w:4 * DN_WIDTH].astype(BF16),
        'w_ba': _pad_lanes(w_in0[:, 4 * DN_WIDTH:glu_off]).astype(BF16),
        'w_glu': w_in0[:, glu_off:].astype(BF16),
        'conv_w': conv_qkv_w[0].astype(F32),
        'neg_a': _pad_lanes(jnp.concatenate([jnp.zeros((DN_HEADS,), F32), -jnp.exp(a_log[0].astype(F32))])[None]),
        'dt_b': _pad_lanes(jnp.concatenate([jnp.zeros((DN_HEADS,), F32), dt_bias[0].astype(F32)])[None]),
        'dw_w': conv_dw_w[0].astype(F32), 'dw_b': row(conv_dw_b[0]),
        'cv_g': row(cv_norm_g[0]), 'cv_b': row(cv_norm_b[0]),
        'w_out': w_out[0].astype(BF16), 'ln1_g': row(ln1_g[0]), 'ln1_b': row(ln1_b[0]),
    }
    w_r = _pad_lanes(jnp.concatenate([w_group[0], w_router[0]], axis=1).astype(F32))
    p['w_r_hi'] = w_r.astype(BF16)
    p['w_r_lo'] = (w_r - p['w_r_hi'].astype(F32)).astype(BF16)
    p['b_r'] = _pad_lanes(jnp.concatenate([b_group[0], b_router[0]])[None].astype(F32))
    gain = row(dn_norm_g[0])

    conf_w = p['dw_w'].shape[1]
    zero_hq = jnp.zeros((QKV_HALO, qkv_w), F32)
    zero_hc = jnp.zeros((CONF_HALO, conf_w), F32)
    m_mixed, mbg, halo_q, halo_c, _ = _mix_in(meta_tokens[None].astype(F32), p, zero_hq, zero_hc, N_META)
    front = lambda a: jnp.pad(a, [(0, 0), (CHUNK - N_META, 0), (0, 0)])
    s_zero = jnp.zeros((DN_HEADS, HEAD_DIM, HEAD_DIM), F32)
    _, s_meta = _delta(front(m_mixed), front(mbg), s_zero, gain, 1)

    mixed, bg, _, _, h = _mix_in(x, p, halo_q[0], halo_c[0], TM_IN)
    o, _ = _delta(mixed, bg, s_meta[0], gain, DELTA_CHUNKS)

    n = bsz * seq
    h1, route, route_t, cnt = _mix_out(h.reshape(n, d), o.reshape(n, DN_WIDTH),
                                       mixed.reshape(n, mixed.shape[2]), p, halo_c[0], seq, TM_OUT, alpha)

    n_slabs = d // (2 * LANES)
    counts = cnt[0, :N_EXPERTS].astype(jnp.int32)
    dest_rows, pad_lo, pad_hi, block_expert, n_used, n_blocks = _dispatch_plan(route_t, counts, BM_EXPERT, n_slabs)
    xs = _dispatch(dest_rows, pad_lo, pad_hi, h1, n_blocks * BM_EXPERT, TD_DISPATCH, n_slabs, BM_EXPERT)
    first_layer = lambda w: w.reshape(w.shape[1:])
    y_sorted = _experts(block_expert, n_used, xs, first_layer(w_exp_gate), first_layer(w_exp_up),
                        first_layer(w_exp_down), BM_EXPERT)
    out = _combine(dest_rows, y_sorted, h1, route, row(ln2_g[0]), row(ln2_b[0]), TM_COMBINE, alpha)
    return out.reshape(bsz, seq, d)
```

```python
import functools

import jax
import jax.numpy as jnp
from jax import lax
from jax.experimental import pallas as pl
from jax.experimental.pallas import tpu as pltpu

F32 = jnp.float32
BF16 = jnp.bfloat16

NORM_EPS = 1e-5
N_META = 16
DN_HEADS = 4
HEAD_DIM = 128
DN_WIDTH = DN_HEADS * HEAD_DIM
CHUNK = 64
SHORT_CONV = 4
CONF_KERNEL = 31
N_GROUPS = 4
EXPERTS_PER_GROUP = 8
N_EXPERTS = N_GROUPS * EXPERTS_PER_GROUP
TOP_K = 2
LANES = 128
SUBLANES = 8
QKV_HALO = 8
CONF_HALO = 32
VMEM_LIMIT = 56 * 1024 * 1024

TM_IN = 1024
DELTA_CHUNKS = 8
TM_OUT = 1024
BM_EXPERT = 512
TM_COMBINE = 512
TD_DISPATCH = 512
RING = 3
SUB_ROWS = 256
SUB_ROWS_OUT = 512
CONV_BLOCK_ROWS = 64
CONV_BLOCK_COLS = 512
MIXED_COLS = 5
COL_Q, COL_K, COL_V, COL_Z, COL_C = range(MIXED_COLS)
DMA_UNROLL = 8


def _dot(a, b):
    return jnp.dot(a, b, preferred_element_type=F32)


def _split2(x):
    hi = x.astype(BF16)
    lo = (x - hi.astype(F32)).astype(BF16)
    return hi, lo


def _sigmoid(x):
    return 0.5 + 0.5 * jnp.tanh(0.5 * x)


def _silu(x):
    half = 0.5 * x
    return half + half * jnp.tanh(half)


def _layer_norm(x, g, b):
    mu = jnp.mean(x, axis=-1, keepdims=True)
    xc = x - mu
    var = jnp.mean(xc * xc, axis=-1, keepdims=True)
    return xc * lax.rsqrt(var + NORM_EPS) * g + b


def _pack_bf16_pairs(x):
    half = x.shape[1] // 2
    lo = lax.bitcast_convert_type(x[:, :half].astype(BF16).astype(F32), jnp.uint32)
    hi = lax.bitcast_convert_type(x[:, half:].astype(BF16).astype(F32), jnp.uint32)
    return (lo >> 16) | (hi & jnp.uint32(0xFFFF0000))


def _unpack_bf16_pairs(p):
    lo = lax.bitcast_convert_type(p << 16, F32)
    hi = lax.bitcast_convert_type(p & jnp.uint32(0xFFFF0000), F32)
    return jnp.concatenate([lo, hi], axis=1)


def _store_slabs(ref, val):
    rows, d = val.shape
    n_slabs = d // LANES
    for s in range(n_slabs):
        ref[pl.ds(s, rows, stride=n_slabs), :] = val[:, s * LANES:(s + 1) * LANES]


def _load_slabs(ref, rows, n_slabs, base=0):
    return jnp.concatenate([ref[pl.ds(base + s, rows, stride=n_slabs), :] for s in range(n_slabs)], axis=1)


def _full_spec(shape):
    nd = len(shape)
    return pl.BlockSpec(shape, lambda *_: (0,) * nd)


def _causal_depthwise(ext_ref, w_ref, n_taps, halo, tm):
    ext = ext_ref[...]
    rows, cols = ext.shape
    first = halo - (n_taps - 1)
    groups = ext.reshape(rows // SUBLANES, SUBLANES, cols)
    sub = lax.broadcasted_iota(jnp.int32, (1, SUBLANES, 1), 1)

    def shifted(phase):
        rolled = pltpu.roll(groups, SUBLANES - phase, axis=1)
        nxt = jnp.concatenate([rolled[1:], rolled[:1]], axis=0)
        return jnp.where(sub < SUBLANES - phase, rolled, nxt).reshape(rows, cols)

    phases = {}
    for k in range(n_taps):
        phase = (first + k) % SUBLANES
        if phase not in phases:
            phases[phase] = ext if phase == 0 else shifted(phase)

    rb = min(tm, CONV_BLOCK_ROWS)
    cb = min(cols, CONV_BLOCK_COLS)
    out_rows = []
    for r0 in range(0, tm, rb):
        out_cols = []
        for c0 in range(0, cols, cb):
            acc = None
            for k in range(n_taps):
                phase = (first + k) % SUBLANES
                base = first + k - phase + r0
                term = phases[phase][base:base + rb, c0:c0 + cb] * w_ref[k:k + 1, c0:c0 + cb]
                acc = term if acc is None else acc + term
            out_cols.append(acc)
        out_rows.append(out_cols[0] if len(out_cols) == 1 else jnp.concatenate(out_cols, axis=1))
    return out_rows[0] if len(out_rows) == 1 else jnp.concatenate(out_rows, axis=0)


def _row_views(refs, r0, rows, lead=0):
    return [ref.at[pl.ds(r0, lead + rows), :] for ref in refs]


def _mix_in_kernel(x_ref, lng_ref, lnb_ref, wqkv_ref, wz_ref, wglu_ref, wba_ref, cw_ref, nega_ref,
                   dtb_ref, hq_in_ref, hc_in_ref,
                   m_ref, bg_ref, hq_out_ref, hc_out_ref, h_ref,
                   qkv_ext, c_ext):
    tm = x_ref.shape[0]

    @pl.when(pl.program_id(1) == 0)
    def _():
        qkv_ext[0:QKV_HALO, :] = hq_in_ref[...]
        c_ext[0:CONF_HALO, :] = hc_in_ref[...]

    th = min(tm, SUB_ROWS)
    for r0 in range(0, tm, th):
        x_v, bg_v, h_v = _row_views((x_ref, bg_ref, h_ref), r0, th)
        q_v, k_v, v_v, z_v, c_v = (m_ref.at[pl.ds(r0, th), pl.ds(col * DN_WIDTH, DN_WIDTH)]
                                   for col in range(MIXED_COLS))
        qkv_v, = _row_views((qkv_ext,), r0, th, QKV_HALO)
        cext_v, = _row_views((c_ext,), r0, th, CONF_HALO)
        _mix_in_rows(x_v, lng_ref, lnb_ref, wqkv_ref, wz_ref, wglu_ref, wba_ref, cw_ref, nega_ref, dtb_ref,
                     q_v, k_v, v_v, z_v, c_v, bg_v, h_v, qkv_v, cext_v)

    q_tail = qkv_ext[tm:tm + QKV_HALO, :]
    c_tail = c_ext[tm:tm + CONF_HALO, :]
    qkv_ext[0:QKV_HALO, :] = q_tail
    c_ext[0:CONF_HALO, :] = c_tail
    hq_out_ref[...] = q_tail
    hc_out_ref[...] = c_tail


def _mix_in_rows(x_ref, lng_ref, lnb_ref, wqkv_ref, wz_ref, wglu_ref, wba_ref, cw_ref, nega_ref, dtb_ref,
                 q_ref, k_ref, v_ref, z_ref, c_ref, bg_ref, h_ref, qkv_ext, c_ext):
    tm = x_ref.shape[0]
    h = _layer_norm(x_ref[...], lng_ref[...], lnb_ref[...])
    h_ref[...] = h
    hb = h.astype(BF16)

    qkv_ext[QKV_HALO:QKV_HALO + tm, :] = _dot(hb, wqkv_ref[...])
    qkv = _silu(_causal_depthwise(qkv_ext, cw_ref, SHORT_CONV, QKV_HALO, tm))
    for hd in range(DN_HEADS):
        lo = hd * HEAD_DIM
        qh = qkv[:, lo:lo + HEAD_DIM]
        kh = qkv[:, DN_WIDTH + lo:DN_WIDTH + lo + HEAD_DIM]
        q_ref[:, lo:lo + HEAD_DIM] = (qh * (lax.rsqrt(jnp.sum(qh * qh, axis=-1, keepdims=True) + 1e-6)
                                            * (HEAD_DIM ** -0.5))).astype(q_ref.dtype)
        k_ref[:, lo:lo + HEAD_DIM] = (kh * lax.rsqrt(jnp.sum(kh * kh, axis=-1, keepdims=True) + 1e-6)
                                      ).astype(k_ref.dtype)
    v_ref[...] = qkv[:, 2 * DN_WIDTH:].astype(v_ref.dtype)
    z_ref[...] = _dot(hb, wz_ref[...]).astype(z_ref.dtype)

    ba = _dot(hb, wba_ref[...])
    lane = lax.broadcasted_iota(jnp.int32, ba.shape, 1)
    sp_in = ba + dtb_ref[...]
    softplus = jnp.maximum(sp_in, 0.0) + jnp.log(1.0 + jnp.exp(-jnp.abs(sp_in)))
    bg_ref[...] = jnp.where(lane < DN_HEADS, _sigmoid(ba), nega_ref[...] * softplus)

    glu = _dot(hb, wglu_ref[...])
    cw = glu.shape[1] // 2
    c_pre = glu[:, :cw] * _sigmoid(glu[:, cw:])
    c_ref[...] = c_pre.astype(c_ref.dtype)
    c_ext[CONF_HALO:CONF_HALO + tm, :] = c_pre


def _mix_in(x, p, halo_q, halo_c, tm):
    bsz, seq, d = x.shape
    assert seq % tm == 0
    qkv_w = 3 * DN_WIDTH
    conf_w = p['dw_w'].shape[1]

    def row(width):
        return pl.BlockSpec((None, tm, width), lambda b, t: (b, t, 0))

    def per_batch(rows, width):
        return pl.BlockSpec((None, rows, width), lambda b, t: (b, 0, 0))

    consts = [p['ln_emb_g'], p['ln_emb_b'], p['w_qkv'], p['w_z'], p['w_glu'], p['w_ba'], p['conv_w'],
              p['neg_a'], p['dt_b'], halo_q, halo_c]
    sds = jax.ShapeDtypeStruct
    assert conf_w == DN_WIDTH, "q, k, v, z and the conformer channels share one array of equal-width column blocks"
    out_shape = [sds((bsz, seq, MIXED_COLS * DN_WIDTH), BF16),
                 sds((bsz, seq, LANES), F32), sds((bsz, QKV_HALO, qkv_w), F32),
                 sds((bsz, CONF_HALO, conf_w), F32), sds((bsz, seq, d), F32)]
    out_specs = [row(MIXED_COLS * DN_WIDTH), row(LANES), per_batch(QKV_HALO, qkv_w),
                 per_batch(CONF_HALO, conf_w), row(d)]
    return pl.pallas_call(
        _mix_in_kernel,
        grid=(bsz, seq // tm),
        in_specs=[row(d)] + [_full_spec(c.shape) for c in consts],
        out_specs=out_specs,
        out_shape=out_shape,
        scratch_shapes=[pltpu.VMEM((QKV_HALO + tm, qkv_w), F32), pltpu.VMEM((CONF_HALO + tm, conf_w), F32)],
        compiler_params=pltpu.CompilerParams(dimension_semantics=("parallel", "arbitrary"),
                                             vmem_limit_bytes=VMEM_LIMIT),
        name="mix_in",
    )(x, *consts)


def _bmm(a, b):
    return jnp.einsum('nij,njk->nik', a, b, preferred_element_type=F32)


def _delta_kernel(m_ref, z_ref, bg_ref, s0_ref, gain_ref, o_ref, sfin_ref,
                  s_ref, u_s, wq_s, attn_s, kd_s, egl_s, *, chunks, blocks_per_seq):
    j = pl.program_id(0)
    slot = j % 2
    prev = 1 - slot
    gain = gain_ref[...]

    @pl.when(j == 0)
    def _():
        u_s[1] = jnp.zeros(u_s.shape[1:], u_s.dtype)
        wq_s[1] = jnp.zeros(wq_s.shape[1:], wq_s.dtype)
        attn_s[1] = jnp.zeros(attn_s.shape[1:], attn_s.dtype)
        kd_s[1] = jnp.zeros(kd_s.shape[1:], kd_s.dtype)
        egl_s[1] = jnp.zeros(egl_s.shape[1:], egl_s.dtype)

    def recurrence():
        live = j > 0
        restart = (jnp.maximum(j - 1, 0) % blocks_per_seq) == 0
        start = [jnp.where(restart, s0_ref[hd], s_ref[hd]) for hd in range(DN_HEADS)]
        state = list(start)
        for c in range(chunks):
            idx = [hd * chunks + c for hd in range(DN_HEADS)]
            wq = [_dot(wq_s[prev, n], state[hd].astype(BF16)) for hd, n in enumerate(idx)]
            yield
            v_new = [(u_s[prev, n] - wq[hd][:CHUNK]).astype(BF16) for hd, n in enumerate(idx)]
            o = [wq[hd][CHUNK:] + _dot(attn_s[prev, n], v_new[hd]) for hd, n in enumerate(idx)]
            state = [state[hd] * egl_s[prev, n][0:1, :]
                     + lax.dot_general(kd_s[prev, n], v_new[hd], (((0,), (0,)), ((), ())),
                                       preferred_element_type=F32)
                     for hd, n in enumerate(idx)]
            yield
            for hd in range(DN_HEADS):
                cols = slice(hd * HEAD_DIM, (hd + 1) * HEAD_DIM)
                rows = slice(c * CHUNK, (c + 1) * CHUNK)
                r = o[hd] * lax.rsqrt(jnp.mean(o[hd] * o[hd], axis=-1, keepdims=True) + 1e-6)
                o_ref[rows, cols] = (r * gain * _silu(z_ref[rows, cols].astype(F32))).astype(o_ref.dtype)
        for hd in range(DN_HEADS):
            kept = jnp.where(live, state[hd], start[hd])
            s_ref[hd] = kept
            sfin_ref[hd] = kept

    def preparation():
        yield from _delta_prepare(m_ref, bg_ref, u_s, wq_s, attn_s, kd_s, egl_s, slot, chunks)

    halves = [recurrence(), preparation()]
    while halves:
        for gen in list(halves):
            try:
                next(gen)
            except StopIteration:
                halves.remove(gen)


def _delta_prepare(m_ref, bg_ref, u_s, wq_s, attn_s, kd_s, egl_s, slot, chunks):
    ii = lax.broadcasted_iota(jnp.int32, (CHUNK, CHUNK), 0)
    jj = lax.broadcasted_iota(jnp.int32, (CHUNK, CHUNK), 1)
    causal = ii >= jj
    strict = ii > jj
    eye = (ii == jj).astype(F32)
    bg3 = bg_ref[...].reshape(chunks, CHUNK, LANES)
    tril_b = jnp.broadcast_to(causal.astype(BF16), (chunks, CHUNK, CHUNK))
    p1 = bg3.astype(BF16)
    r1 = bg3 - p1.astype(F32)
    p2 = r1.astype(BF16)
    p3 = (r1 - p2.astype(F32)).astype(BF16)
    gc3 = _bmm(tril_b, p1) + _bmm(tril_b, p2) + _bmm(tril_b, p3)
    yield

    def heads(col):
        lo = col * DN_WIDTH
        return jnp.concatenate([m_ref[:, lo + hd * HEAD_DIM:lo + (hd + 1) * HEAD_DIM].astype(F32)
                                .reshape(chunks, CHUNK, HEAD_DIM) for hd in range(DN_HEADS)], axis=0)
    q = heads(COL_Q)
    k = heads(COL_K)
    v = heads(COL_V)
    bet = jnp.concatenate([bg3[:, :, hd:hd + 1] for hd in range(DN_HEADS)], axis=0)
    gc = jnp.concatenate([gc3[:, :, DN_HEADS + hd:DN_HEADS + hd + 1] for hd in range(DN_HEADS)], axis=0)
    gc_t = [gc3[c].T for c in range(chunks)]
    decay = jnp.stack([
        jnp.exp(jnp.where(causal, gc3[c][:, DN_HEADS + hd:DN_HEADS + hd + 1]
                          - gc_t[c][DN_HEADS + hd:DN_HEADS + hd + 1, :], -jnp.inf))
        for hd in range(DN_HEADS) for c in range(chunks)], axis=0)

    kb = k * bet
    g_all = jnp.einsum('nid,njd->nij', jnp.concatenate([kb, q], axis=1).astype(BF16), k.astype(BF16),
                       preferred_element_type=F32)
    yield
    a_low = jnp.where(strict, g_all[:, :CHUNK] * decay, 0.0)
    attn = (g_all[:, CHUNK:] * decay).astype(BF16)

    l_mat = eye + a_low
    l_bf = l_mat.astype(BF16)
    t_mat = eye - a_low
    for _ in range(4):
        res = eye - _bmm(l_bf, t_mat.astype(BF16))
        yield
        t_mat = t_mat + _bmm(t_mat.astype(BF16), res.astype(BF16))
        yield
    l_lo = (l_mat - l_bf.astype(F32)).astype(BF16)
    t_hi = t_mat.astype(BF16)
    t_lo = (t_mat - t_hi.astype(F32)).astype(BF16)
    res = eye - (_bmm(l_bf, t_hi) + _bmm(l_lo, t_hi) + _bmm(l_bf, t_lo))
    yield
    t_mat = t_mat + _bmm(t_hi, res.astype(BF16))
    yield

    eg = jnp.exp(gc)
    uw = _bmm(t_mat.astype(BF16), jnp.concatenate([v * bet, kb * eg], axis=2).astype(BF16))
    yield
    u = uw[:, :, :HEAD_DIM]
    wq_lhs = jnp.concatenate([uw[:, :, HEAD_DIM:], q * eg], axis=1).astype(BF16)
    g_last = gc[:, CHUNK - 1:CHUNK, :]
    k_dec = (k * jnp.exp(g_last - gc)).astype(BF16)
    eg_last = jnp.exp(g_last)

    u_s[slot] = u
    wq_s[slot] = wq_lhs
    attn_s[slot] = attn
    kd_s[slot] = k_dec
    egl_s[slot] = jnp.broadcast_to(eg_last, egl_s.shape[1:])


def _delta(mixed, bg, s0, gain, chunks):
    bsz, seq, _ = mixed.shape
    rows = chunks * CHUNK
    assert seq % rows == 0

    nj = seq // rows
    nb = DN_HEADS * chunks
    total = bsz * nj

    def prep(width):
        def index(j):
            blk = jnp.minimum(j, total - 1)
            return (blk // nj, blk % nj, 0)
        return pl.BlockSpec((None, rows, width), index)

    def scan(width, col=0):
        def index(j):
            blk = jnp.maximum(j - 1, 0)
            return (blk // nj, blk % nj, col)
        return pl.BlockSpec((None, rows, width), index)

    state_shape = (DN_HEADS, HEAD_DIM, HEAD_DIM)
    return pl.pallas_call(
        functools.partial(_delta_kernel, chunks=chunks, blocks_per_seq=nj),
        grid=(total + 1,),
        in_specs=[prep(mixed.shape[2]), scan(DN_WIDTH, COL_Z), prep(LANES), _full_spec(state_shape),
                  _full_spec(gain.shape)],
        out_specs=[scan(DN_WIDTH),
                   pl.BlockSpec((None,) + state_shape, lambda j: (jnp.maximum(j - 1, 0) // nj, 0, 0, 0))],
        out_shape=[jax.ShapeDtypeStruct((bsz, seq, DN_WIDTH), BF16),
                   jax.ShapeDtypeStruct((bsz,) + state_shape, F32)],
        scratch_shapes=[pltpu.VMEM(state_shape, F32),
                        pltpu.VMEM((2, nb, CHUNK, HEAD_DIM), F32),
                        pltpu.VMEM((2, nb, 2 * CHUNK, HEAD_DIM), BF16),
                        pltpu.VMEM((2, nb, CHUNK, CHUNK), BF16),
                        pltpu.VMEM((2, nb, CHUNK, HEAD_DIM), BF16),
                        pltpu.VMEM((2, nb, SUBLANES, HEAD_DIM), F32)],
        compiler_params=pltpu.CompilerParams(dimension_semantics=("arbitrary",), vmem_limit_bytes=VMEM_LIMIT),
        name="delta",
    )(mixed, mixed, bg, s0, gain)


def _mix_out_kernel(h_ref, o_ref, c_ref, wo_ref, g1_ref, b1_ref, wrh_ref, wrl_ref, br_ref,
                    dww_ref, dwb_ref, cvg_ref, cvb_ref, hc_in_ref, tri_ref,
                    h1_ref, route_ref, route_t_ref, cnt_out_ref, cnt_ref, c_ext, *, alpha, tiles_per_seq):
    tm = h_ref.shape[0]

    @pl.when(pl.program_id(0) == 0)
    def _():
        cnt_ref[...] = jnp.zeros_like(cnt_ref)

    @pl.when(pl.program_id(0) % tiles_per_seq == 0)
    def _():
        c_ext[0:CONF_HALO, :] = hc_in_ref[...]

    th = min(tm, SUB_ROWS_OUT)
    for r0 in range(0, tm, th):
        h_v, o_v, c_v, h1_v, route_v = _row_views((h_ref, o_ref, c_ref, h1_ref, route_ref), r0, th)
        cext_v, = _row_views((c_ext,), r0, th, CONF_HALO)
        _mix_out_rows(h_v, o_v, c_v, wo_ref, g1_ref, b1_ref, wrh_ref, wrl_ref, br_ref, dww_ref, dwb_ref, cvg_ref,
                      cvb_ref, h1_v, route_v, route_t_ref.at[:, pl.ds(r0, th)], cnt_ref, cext_v, tri_ref, alpha)
    c_ext[0:CONF_HALO, :] = c_ext[tm:tm + CONF_HALO, :]
    cnt_out_ref[...] = jnp.broadcast_to(cnt_ref[...], cnt_out_ref.shape)


def _mix_out_rows(h_ref, o_ref, c_ref, wo_ref, g1_ref, b1_ref, wrh_ref, wrl_ref, br_ref, dww_ref, dwb_ref, cvg_ref,
                  cvb_ref, h1_ref, route_ref, route_t_ref, cnt_ref, c_ext, tri_ref, alpha):
    tm = h_ref.shape[0]

    c_ext[CONF_HALO:CONF_HALO + tm, :] = c_ref[...].astype(F32)
    conv = _causal_depthwise(c_ext, dww_ref, CONF_KERNEL, CONF_HALO, tm) + dwb_ref[...]
    conf = _silu(_layer_norm(conv, cvg_ref[...], cvb_ref[...]))

    h = h_ref[...]
    dn = o_ref.shape[1]
    mix = _dot(o_ref[...].astype(BF16), wo_ref[0:dn, :]) + _dot(conf.astype(BF16), wo_ref[dn:, :])
    h1 = _layer_norm(alpha * h + mix, g1_ref[...], b1_ref[...])
    h1_ref[...] = h1

    hh, hl = _split2(h1)
    logits = _dot(hh, wrh_ref[...]) + _dot(hl, wrh_ref[...]) + _dot(hh, wrl_ref[...]) + br_ref[...]
    lane = lax.broadcasted_iota(jnp.int32, logits.shape, 1).astype(F32)
    big = float(LANES)
    neg = -jnp.inf

    def first_argmax(vals):
        top = jnp.max(vals, axis=-1, keepdims=True)
        return top, jnp.min(jnp.where(vals == top, lane, big), axis=-1, keepdims=True)

    grp = jnp.where(lane < N_GROUPS, logits, neg)
    g_top, g_sel = first_argmax(grp)
    p_group = 1.0 / jnp.sum(jnp.exp(grp - g_top), axis=-1, keepdims=True)
    lo = N_GROUPS + EXPERTS_PER_GROUP * g_sel
    in_grp = jnp.where((lane >= lo) & (lane < lo + EXPERTS_PER_GROUP), logits, neg)
    m1, i1 = first_argmax(in_grp)
    m2, i2 = first_argmax(jnp.where(lane == i1, neg, in_grp))
    s = jnp.exp(m2 - m1)
    w1 = p_group / (1.0 + s)
    w2 = p_group * s / (1.0 + s)
    e1 = i1 - N_GROUPS
    e2 = i2 - N_GROUPS

    tm = logits.shape[0]
    oh1 = (lane == e1).astype(F32)
    oh2 = (lane == e2).astype(F32)
    both = oh1 + oh2
    base = _dot(tri_ref[...], both.astype(BF16)) + cnt_ref[...]
    r1 = jnp.sum(oh1 * base, axis=-1, keepdims=True)
    r2 = jnp.sum(oh2 * base, axis=-1, keepdims=True)
    cnt_ref[...] = cnt_ref[...] + jnp.sum(both, axis=0, keepdims=True)

    vals = (e1, e2, w1, w2, r1, r2)
    route = jnp.zeros_like(logits)
    for idx, val in enumerate(vals):
        route = jnp.where(lane == idx, val, route)
    route_ref[...] = route
    route_t_ref[...] = route.T[0:SUBLANES, :]


def _mix_out(h2d, o2d, mixed2d, p, halo_c, seq, tm, alpha):
    n, d = h2d.shape
    assert seq % tm == 0
    conf_w = p['dw_w'].shape[1]
    th = min(tm, SUB_ROWS_OUT)
    tri = (jnp.arange(th)[:, None] > jnp.arange(th)[None, :]).astype(BF16)

    def row(width):
        return pl.BlockSpec((tm, width), lambda i: (i, 0))

    consts = [p['w_out'], p['ln1_g'], p['ln1_b'], p['w_r_hi'], p['w_r_lo'], p['b_r'],
              p['dw_w'], p['dw_b'], p['cv_g'], p['cv_b'], halo_c, tri]
    return pl.pallas_call(
        functools.partial(_mix_out_kernel, alpha=alpha, tiles_per_seq=seq // tm),
        grid=(n // tm,),
        in_specs=[row(d), row(o2d.shape[1]), pl.BlockSpec((tm, conf_w), lambda i: (i, COL_C))]
                 + [_full_spec(c.shape) for c in consts],
        out_specs=[row(d), row(LANES),
                   pl.BlockSpec((SUBLANES, tm), lambda i: (0, i)), _full_spec((SUBLANES, LANES))],
        out_shape=[jax.ShapeDtypeStruct((n, d), F32), jax.ShapeDtypeStruct((n, LANES), F32),
                   jax.ShapeDtypeStruct((SUBLANES, n), F32), jax.ShapeDtypeStruct((SUBLANES, LANES), F32)],
        scratch_shapes=[pltpu.VMEM((1, LANES), F32), pltpu.VMEM((CONF_HALO + tm, conf_w), F32)],
        compiler_params=pltpu.CompilerParams(dimension_semantics=("arbitrary",), vmem_limit_bytes=VMEM_LIMIT),
        name="mix_out",
    )(h2d, o2d, mixed2d, *consts)


def _dispatch_kernel(dest_ref, pad_lo_ref, pad_hi_ref, h1_hbm, xs_hbm, raw, ring, zslab, fsem, ssem, zsem, *,
                     td, n_slabs, n_tokens):
    i = pl.program_id(0)
    nb = pl.num_programs(0)
    slot = i % RING
    tile_rows = td * n_slabs

    def fetch(step):
        start = pl.multiple_of(step * td, td)
        return pltpu.make_async_copy(h1_hbm.at[pl.ds(start, td), :], raw.at[step % RING], fsem.at[step % RING])

    def wait_scatter(step):
        for _ in range(TOP_K):
            pltpu.make_async_copy(ring.at[step % RING], xs_hbm.at[pl.ds(0, tile_rows), :],
                                  ssem.at[step % RING]).wait()

    zrows = zslab.shape[0] // n_slabs

    def pad_copy(row, size):
        return pltpu.make_async_copy(
            zslab.at[pl.ds(0, size * n_slabs), :],
            xs_hbm.at[pl.ds(pl.multiple_of(row * n_slabs, n_slabs), size * n_slabs), :], zsem.at[0])

    def for_each_pad_copy(fn):
        def per_expert(e, carry):
            lo = pad_lo_ref[e]
            length = pad_hi_ref[e] - lo
            for b in range(zrows.bit_length()):
                size = 1 << b

                @pl.when(((length >> b) & 1) == 1)
                def _():
                    fn(pad_copy(lo + (length & (size - 1)), size))
            return carry
        lax.fori_loop(0, N_EXPERTS, per_expert, 0)
        tail_lo = pad_lo_ref[N_EXPERTS]

        def per_piece(piece, carry):
            fn(pad_copy(tail_lo + piece * zrows, zrows))
            return carry
        lax.fori_loop(0, (pad_hi_ref[N_EXPERTS] - tail_lo) // zrows, per_piece, 0)

    @pl.when(i == 0)
    def _():
        fetch(0).start()
        zslab[...] = jnp.zeros_like(zslab)
        for_each_pad_copy(lambda cp: cp.start())

    @pl.when((i == 0) & (nb > 1))
    def _():
        fetch(1).start()

    fetch(i).wait()
    _store_slabs(ring.at[slot], _pack_bf16_pairs(raw[slot]))

    def issue_body(r, carry):
        src = ring.at[slot, pl.ds(pl.multiple_of(r * n_slabs, n_slabs), n_slabs), :]
        for k in range(TOP_K):
            dst_row = pl.multiple_of(dest_ref[k * n_tokens + i * td + r], n_slabs)
            pltpu.make_async_copy(src, xs_hbm.at[pl.ds(dst_row, n_slabs), :], ssem.at[slot]).start(priority=k)
        return carry
    lax.fori_loop(0, td, issue_body, 0, unroll=DMA_UNROLL)

    @pl.when(i > 0)
    def _():
        wait_scatter(i - 1)

    @pl.when(i + 2 < nb)
    def _():
        fetch(i + 2).start()

    @pl.when(i == nb - 1)
    def _():
        wait_scatter(i)
        for_each_pad_copy(lambda cp: cp.wait())


def _dispatch(dest_rows, pad_lo, pad_hi, h1, cap_rows, td, n_slabs, bm):
    n, d = h1.shape
    assert n % td == 0
    grid_spec = pltpu.PrefetchScalarGridSpec(
        num_scalar_prefetch=3,
        grid=(n // td,),
        in_specs=[pl.BlockSpec(memory_space=pl.ANY)],
        out_specs=pl.BlockSpec(memory_space=pl.ANY),
        scratch_shapes=[pltpu.VMEM((RING, td, d), h1.dtype),
                        pltpu.VMEM((RING, td * n_slabs, LANES), jnp.uint32),
                        pltpu.VMEM((bm // 2 * n_slabs, LANES), jnp.uint32),
                        pltpu.SemaphoreType.DMA((RING,)), pltpu.SemaphoreType.DMA((RING,)),
                        pltpu.SemaphoreType.DMA((1,))],
    )
    return pl.pallas_call(
        functools.partial(_dispatch_kernel, td=td, n_slabs=n_slabs, n_tokens=n),
        grid_spec=grid_spec,
        out_shape=jax.ShapeDtypeStruct((cap_rows * n_slabs, LANES), jnp.uint32),
        compiler_params=pltpu.CompilerParams(dimension_semantics=("arbitrary",), disable_bounds_checks=True),
        name="dispatch",
    )(dest_rows, pad_lo, pad_hi, h1)


def _expert_kernel(be_ref, nu_ref, xs_ref, wg_ref, wu_ref, wd_ref, y_ref, wg_bf, wu_bf, wd_bf, *, bm, n_slabs):
    i = pl.program_id(0)
    used = i < nu_ref[0]

    @pl.when(used & ((i == 0) | (be_ref[i] != be_ref[jnp.maximum(i - 1, 0)])))
    def _():
        wg_bf[...] = wg_ref[...].astype(BF16)
        wu_bf[...] = wu_ref[...].astype(BF16)
        wd_bf[...] = wd_ref[...].astype(BF16)

    @pl.when(used)
    def _():
        xb = _unpack_bf16_pairs(_load_slabs(xs_ref, bm, n_slabs)).astype(BF16)
        hid = _silu(_dot(xb, wg_bf[...])) * _dot(xb, wu_bf[...])
        _store_slabs(y_ref, _pack_bf16_pairs(_dot(hid.astype(BF16), wd_bf[...])))

    @pl.when(jnp.logical_not(used))
    def _():
        y_ref[...] = jnp.zeros_like(y_ref)


def _experts(block_expert, n_used, xs, w_gate, w_up, w_down, bm):
    n_blocks = block_expert.shape[0]
    d = w_gate.shape[1]
    ff = w_gate.shape[2]
    n_slabs = d // (2 * LANES)

    def blk(i, be, nu):
        return jnp.minimum(i, nu[0] - 1)

    grid_spec = pltpu.PrefetchScalarGridSpec(
        num_scalar_prefetch=2,
        grid=(n_blocks,),
        in_specs=[pl.BlockSpec((bm * n_slabs, LANES), lambda i, be, nu: (blk(i, be, nu), 0)),
                  pl.BlockSpec((None, d, ff), lambda i, be, nu: (be[blk(i, be, nu)], 0, 0)),
                  pl.BlockSpec((None, d, ff), lambda i, be, nu: (be[blk(i, be, nu)], 0, 0)),
                  pl.BlockSpec((None, ff, d), lambda i, be, nu: (be[blk(i, be, nu)], 0, 0))],
        out_specs=pl.BlockSpec((bm * n_slabs, LANES), lambda i, be, nu: (i, 0)),
        scratch_shapes=[pltpu.VMEM((d, ff), BF16), pltpu.VMEM((d, ff), BF16), pltpu.VMEM((ff, d), BF16)],
    )
    return pl.pallas_call(
        functools.partial(_expert_kernel, bm=bm, n_slabs=n_slabs),
        grid_spec=grid_spec,
        out_shape=jax.ShapeDtypeStruct(xs.shape, xs.dtype),
        compiler_params=pltpu.CompilerParams(dimension_semantics=("arbitrary",), vmem_limit_bytes=VMEM_LIMIT),
        name="experts",
    )(block_expert, n_used, xs, w_gate, w_up, w_down)


def _combine_kernel(dest_ref, y_hbm, h1_ref, route_ref, g2_ref, b2_ref, out_ref, ybuf, sem, *, tm, n_slabs, alpha):
    i = pl.program_id(0)
    nb = pl.num_programs(0)
    part = tm * n_slabs

    def issue(blk, slot):
        def body(r, carry):
            for k in range(TOP_K):
                src_row = pl.multiple_of(dest_ref[k * (nb * tm) + blk * tm + r], n_slabs)
                dst_row = pl.multiple_of((slot * TOP_K + k) * part + r * n_slabs, n_slabs)
                pltpu.make_async_copy(y_hbm.at[pl.ds(src_row, n_slabs), :], ybuf.at[pl.ds(dst_row, n_slabs), :],
                                      sem.at[slot]).start(priority=k)
            return carry
        lax.fori_loop(0, tm, body, 0, unroll=DMA_UNROLL)

    @pl.when(i == 0)
    def _():
        issue(0, 0)

    @pl.when(i + 1 < nb)
    def _():
        issue(i + 1, (i + 1) % 2)

    slot = i % 2
    base = pl.multiple_of(slot * (TOP_K * part), TOP_K * part)
    pltpu.make_async_copy(y_hbm.at[pl.ds(0, TOP_K * part), :], ybuf.at[pl.ds(base, TOP_K * part), :],
                          sem.at[slot]).wait()

    route = route_ref[...]
    ffn = (_unpack_bf16_pairs(_load_slabs(ybuf, tm, n_slabs, base)) * route[:, 2:3]
           + _unpack_bf16_pairs(_load_slabs(ybuf, tm, n_slabs, base + part)) * route[:, 3:4])
    out_ref[...] = _layer_norm(alpha * h1_ref[...] + ffn, g2_ref[...], b2_ref[...])


def _combine(dest_rows, y_sorted, h1, route, ln2_g, ln2_b, tm, alpha):
    n, d = h1.shape
    n_slabs = d // (2 * LANES)
    assert n % tm == 0
    grid_spec = pltpu.PrefetchScalarGridSpec(
        num_scalar_prefetch=1,
        grid=(n // tm,),
        in_specs=[pl.BlockSpec(memory_space=pl.ANY),
                  pl.BlockSpec((tm, d), lambda i, dest: (i, 0)),
                  pl.BlockSpec((tm, LANES), lambda i, dest: (i, 0)),
                  pl.BlockSpec((1, d), lambda i, dest: (0, 0)),
                  pl.BlockSpec((1, d), lambda i, dest: (0, 0))],
        out_specs=pl.BlockSpec((tm, d), lambda i, dest: (i, 0)),
        scratch_shapes=[pltpu.VMEM((2 * TOP_K * tm * n_slabs, LANES), y_sorted.dtype),
                        pltpu.SemaphoreType.DMA((2,))],
    )
    return pl.pallas_call(
        functools.partial(_combine_kernel, tm=tm, n_slabs=n_slabs, alpha=alpha),
        grid_spec=grid_spec,
        out_shape=jax.ShapeDtypeStruct((n, d), F32),
        compiler_params=pltpu.CompilerParams(dimension_semantics=("arbitrary",), vmem_limit_bytes=VMEM_LIMIT,
                                             disable_bounds_checks=True),
        name="combine",
    )(dest_rows, y_sorted, h1, route, ln2_g, ln2_b)


def _dispatch_plan(route_t, counts, bm, n_slabs):
    n = route_t.shape[1]
    expert_id = route_t[0:TOP_K].astype(jnp.int32)
    rank = route_t[4:4 + TOP_K].astype(jnp.int32)
    padded = (counts + bm - 1) // bm * bm
    pad_end = jnp.cumsum(padded)
    pad_start = pad_end - padded
    experts = jnp.arange(N_EXPERTS, dtype=jnp.int32)[:, None, None]
    dest = jnp.sum(jnp.where(expert_id[None] == experts, pad_start[:, None, None], 0), axis=0) + rank
    n_blocks = (n * TOP_K + bm - 1) // bm + N_EXPERTS
    block_start = jnp.arange(n_blocks, dtype=jnp.int32) * bm
    block_expert = jnp.minimum(jnp.sum((block_start[:, None] >= pad_end[None, :]).astype(jnp.int32), axis=1),
                               N_EXPERTS - 1).astype(jnp.int32)
    n_used = (pad_end[-1:] // bm).astype(jnp.int32)
    dest_rows = (dest * n_slabs).reshape(-1).astype(jnp.int32)
    pad_lo = jnp.concatenate([pad_start + counts, pad_end[-1:]]).astype(jnp.int32)
    pad_hi = jnp.concatenate([pad_end, jnp.full((1,), n_blocks * bm, pad_end.dtype)]).astype(jnp.int32)
    return dest_rows, pad_lo, pad_hi, block_expert, n_used, n_blocks


def _pad_lanes(w, width=LANES):
    return jnp.pad(w, [(0, 0)] * (w.ndim - 1) + [(0, width - w.shape[-1])])


def kernel(x, meta_tokens, ln_emb_g, ln_emb_b, w_in, conv_qkv_w, a_log, dt_bias, dn_norm_g, conv_dw_w, conv_dw_b, cv_norm_g, cv_norm_b, w_out, ln1_g, ln1_b, w_group, b_group, w_router, b_router, w_exp_gate, w_exp_up, w_exp_down, ln2_g, ln2_b):
    depth = w_in.shape[0]
    assert depth == 1, "single-layer block"
    bsz, seq, d = x.shape
    alpha = (2.0 * depth) ** 0.25
    qkv_w = 3 * DN_WIDTH
    w_in0 = w_in[0]
    glu_off = 4 * DN_WIDTH + 2 * DN_HEADS
    row = lambda a: a.reshape(1, -1).astype(F32)
    p = {
        'ln_emb_g': row(ln_emb_g), 'ln_emb_b': row(ln_emb_b),
        'w_qkv': w_in0[:, :qkv_w].astype(BF16),
        'w_z': w_in0[:, qkv_w:4 * DN_WIDTH].astype(BF16),
        'w_ba': _pad_lanes(w_in0[:, 4 * DN_WIDTH:glu_off]).astype(BF16),
        'w_glu': w_in0[:, glu_off:].astype(BF16),
        'conv_w': conv_qkv_w[0].astype(F32),
        'neg_a': _pad_lanes(jnp.concatenate([jnp.zeros((DN_HEADS,), F32), -jnp.exp(a_log[0].astype(F32))])[None]),
        'dt_b': _pad_lanes(jnp.concatenate([jnp.zeros((DN_HEADS,), F32), dt_bias[0].astype(F32)])[None]),
        'dw_w': conv_dw_w[0].astype(F32), 'dw_b': row(conv_dw_b[0]),
        'cv_g': row(cv_norm_g[0]), 'cv_b': row(cv_norm_b[0]),
        'w_out': w_out[0].astype(BF16), 'ln1_g': row(ln1_g[0]), 'ln1_b': row(ln1_b[0]),
    }
    w_r = _pad_lanes(jnp.concatenate([w_group[0], w_router[0]], axis=1).astype(F32))
    p['w_r_hi'] = w_r.astype(BF16)
    p['w_r_lo'] = (w_r - p['w_r_hi'].astype(F32)).astype(BF16)
    p['b_r'] = _pad_lanes(jnp.concatenate([b_group[0], b_router[0]])[None].astype(F32))
    gain = row(dn_norm_g[0])

    conf_w = p['dw_w'].shape[1]
    zero_hq = jnp.zeros((QKV_HALO, qkv_w), F32)
    zero_hc = jnp.zeros((CONF_HALO, conf_w), F32)
    m_mixed, mbg, halo_q, halo_c, _ = _mix_in(meta_tokens[None].astype(F32), p, zero_hq, zero_hc, N_META)
    front = lambda a: jnp.pad(a, [(0, 0), (CHUNK - N_META, 0), (0, 0)])
    s_zero = jnp.zeros((DN_HEADS, HEAD_DIM, HEAD_DIM), F32)
    _, s_meta = _delta(front(m_mixed), front(mbg), s_zero, gain, 1)

    mixed, bg, _, _, h = _mix_in(x, p, halo_q[0], halo_c[0], TM_IN)
    o, _ = _delta(mixed, bg, s_meta[0], gain, DELTA_CHUNKS)

    n = bsz * seq
    h1, route, route_t, cnt = _mix_out(h.reshape(n, d), o.reshape(n, DN_WIDTH),
                                       mixed.reshape(n, mixed.shape[2]), p, halo_c[0], seq, TM_OUT, alpha)

    n_slabs = d // (2 * LANES)
    counts = cnt[0, :N_EXPERTS].astype(jnp.int32)
    dest_rows, pad_lo, pad_hi, block_expert, n_used, n_blocks = _dispatch_plan(route_t, counts, BM_EXPERT, n_slabs)
    xs = _dispatch(dest_rows, pad_lo, pad_hi, h1, n_blocks * BM_EXPERT, TD_DISPATCH, n_slabs, BM_EXPERT)
    first_layer = lambda w: w.reshape(w.shape[1:])
    y_sorted = _experts(block_expert, n_used, xs, first_layer(w_exp_gate), first_layer(w_exp_up),
                        first_layer(w_exp_down), BM_EXPERT)
    out = _combine(dest_rows, y_sorted, h1, route, row(ln2_g[0]), row(ln2_b[0]), TM_COMBINE, alpha)
    return out.reshape(bsz, seq, d)
```
